```python
import math
import jax
import jax.numpy as jnp
from jax import lax
import numpy as np

D_MODEL = 1024
BATCH = 16
SEQ = 2048
DEPTH = 4
DEC_BATCH = 128
DEC_SEQ = 8
PAST_LEN = 8192
PAGE_SIZE = 128

N_EVEN = (DEPTH + 1) // 2
N_ODD = DEPTH // 2
RMS_EPS = 1e-6
Q_BLOCK = 128

A_HEADS = 4
A_DK = 128
A_DV = 128
A_WIDTH = A_HEADS * A_DK
HGRN_CHUNK = 64
B_HEADS = 8
B_KV_HEADS = 4
B_DH = 64
IDX_HEADS = 8
IDX_DIM = 64
DSA_TOPK_MAX = 256
C_HEADS = 8
C_DH = 64
C_WIDTH = C_HEADS * C_DH
LORA_W = 64
LORA_A = 64
LORA_G = 128
GN_EPS = 64e-5
D_HEADS = 8
D_NOPE = 64
D_ROPE = 32
D_VDIM = 64
D_KV_RANK = 256
D_Q_RANK = 384
ROPE_THETA = 10000.0
MLA_SCALE = (D_NOPE + D_ROPE) ** -0.5
PEER_HEADS = 8
PEER_NKEYS = 128
PEER_N = PEER_NKEYS * PEER_NKEYS
PEER_DKEY = 256
PEER_TOPK = 16
PEER_BLOCK = 256

EVEN_SIZES = (A_WIDTH, A_WIDTH, A_WIDTH, A_WIDTH, B_HEADS * B_DH, B_KV_HEADS * B_DH, B_KV_HEADS * B_DH, IDX_HEADS * IDX_DIM, IDX_DIM, IDX_HEADS)
EVEN_IN = sum(EVEN_SIZES)
EVEN_OUT = A_WIDTH + B_HEADS * B_DH
RWKV_SIZES = (C_WIDTH, C_WIDTH, C_WIDTH, LORA_W, LORA_A, LORA_G)
RWKV_IN = sum(RWKV_SIZES)
ODD_SIZES = (RWKV_IN, D_Q_RANK, D_KV_RANK, D_ROPE)
ODD_IN = sum(ODD_SIZES)
ODD_OUT = C_WIDTH + D_HEADS * D_VDIM

kernel_name = 'hybrid_hgrn2_dsa_rwkv7_mla_peer_step'


def split_cols(a, sizes):
    out, start = [], 0
    for s in sizes:
        out.append(a[..., start:start + s])
        start += s
    return out


def rmsnorm(x, g):
    xf = x.astype(jnp.float32)
    y = xf * lax.rsqrt(jnp.mean(xf * xf, axis=-1, keepdims=True) + RMS_EPS)
    return (y * g.astype(jnp.float32)).astype(x.dtype)


def rope_tables(pos):
    inv = ROPE_THETA ** (-jnp.arange(0, D_ROPE, 2, dtype=jnp.float32) / D_ROPE)
    ang = pos.astype(jnp.float32)[:, None] * inv[None, :]
    return jnp.cos(ang), jnp.sin(ang)


def apply_rope(x, cos, sin):
    shp = (cos.shape[0],) + (1,) * (x.ndim - 3) + (cos.shape[1],)
    c, s = cos.reshape(shp), sin.reshape(shp)
    xf = x.astype(jnp.float32)
    x1, x2 = xf[..., :D_ROPE // 2], xf[..., D_ROPE // 2:]
    return jnp.concatenate([x1 * c - x2 * s, x1 * s + x2 * c], axis=-1).astype(x.dtype)


def hgrn2_chunked(q, k, v, logf, s0):
    B, T, H, DK = q.shape
    C = math.gcd(T, HGRN_CHUNK)
    n = T // C

    def split(a):
        return jnp.moveaxis(a.astype(jnp.float32).reshape(B, n, C, *a.shape[2:]), 1, 0)

    causal = jnp.tril(jnp.ones((C, C), bool))

    def step(S, inp):
        qb, kb, vb, gb = inp
        G = jnp.cumsum(gb, axis=1)
        diff = G[:, :, None] - G[:, None, :]
        decay = jnp.exp(jnp.where(causal[None, :, :, None, None], diff, -jnp.inf))
        attn = jnp.einsum('bthk,btshk,bshk->bhts', qb, decay, kb)
        o = jnp.einsum('bhts,bshv->bthv', attn, vb) + jnp.einsum('bthk,bhkv->bthv', qb * jnp.exp(G), S)
        g_last = G[:, -1]
        S_new = jnp.exp(g_last)[..., None] * S + jnp.einsum('bshk,bshv->bhkv', kb * jnp.exp(g_last[:, None] - G), vb)
        return S_new, o

    S, o = lax.scan(step, s0, (split(q), split(k), split(v), split(logf)))
    return jnp.moveaxis(o, 0, 1).reshape(B, T, H, v.shape[-1]), S


def rwkv7_scan(r, w, k, v, kk, a, s0):
    def step(S, inp):
        rt, wt, kt, vt, kkt, at = inp
        sa = jnp.einsum('bhvk,bhk->bhv', S, -kkt)
        S = S * wt[:, :, None, :] + sa[..., None] * (kkt * at)[:, :, None, :] + vt[..., None] * kt[:, :, None, :]
        return S, jnp.einsum('bhvk,bhk->bhv', S, rt)

    S, y = lax.scan(step, s0, tuple(jnp.moveaxis(t, 1, 0) for t in (r, w, k, v, kk, a)))
    return jnp.moveaxis(y, 0, 1), S


def head_groupnorm(y, w, b):
    mu = jnp.mean(y, axis=-1, keepdims=True)
    var = jnp.mean(jnp.square(y - mu), axis=-1, keepdims=True)
    yn = ((y - mu) * lax.rsqrt(var + GN_EPS)).reshape(y.shape[0], y.shape[1], -1)
    return yn * w.astype(jnp.float32) + b.astype(jnp.float32)


def dsa_index_scores(q_idx, w_idx, k_idx):
    dots = jnp.einsum('bthd,bsd->bths', q_idx, k_idx).astype(jnp.float32) * IDX_DIM ** -0.5
    return jnp.einsum('bths,bth->bts', jax.nn.relu(dots), w_idx.astype(jnp.float32))


def dsa_attend(q, k_sel, v_sel, valid):
    B, T = q.shape[:2]
    qg = q.reshape(B, T, B_KV_HEADS, B_HEADS // B_KV_HEADS, B_DH)
    s = jnp.einsum('btjrd,btnjd->btjrn', qg, k_sel).astype(jnp.float32) * B_DH ** -0.5
    s = jnp.where(valid[:, :, None, None, :], s, -jnp.inf)
    p = jax.nn.softmax(s, axis=-1).astype(v_sel.dtype)
    return jnp.einsum('btjrn,btnjd->btjrd', p, v_sel).reshape(B, T, B_HEADS * B_DH)


def gather_rows(a, idx):
    return jax.vmap(lambda ab, ib: ab[ib])(a, idx)


def dsa_prompt(q, k, v, q_idx, w_idx, k_idx):
    B, T = q.shape[:2]
    topk = min(DSA_TOPK_MAX, T // 4)
    kpos = jnp.arange(T)

    def block(i):
        start = i * Q_BLOCK
        sl = lambda a: lax.dynamic_slice_in_dim(a, start, Q_BLOCK, axis=1)
        qpos = start + jnp.arange(Q_BLOCK)
        sc = dsa_index_scores(sl(q_idx), sl(w_idx), k_idx)
        sc = jnp.where(kpos[None, None, :] <= qpos[None, :, None], sc, -jnp.inf)
        top_s, top_i = lax.top_k(sc, topk)
        return dsa_attend(sl(q), gather_rows(k, top_i), gather_rows(v, top_i), top_s > -jnp.inf)

    o = lax.map(block, jnp.arange(T // Q_BLOCK))
    return jnp.moveaxis(o, 0, 1).reshape(B, T, B_HEADS * B_DH)


def dsa_sample(q, k, v, q_idx, w_idx, k_idx, pool_k, pool_v, pool_idx, layer, page_table):
    B, T = q.shape[:2]
    n_past = page_table.shape[1] * PAGE_SIZE
    topk = min(DSA_TOPK_MAX, (n_past + T) // 4)
    kidx_past = pool_idx[layer, page_table].reshape(B, n_past, IDX_DIM)
    causal = jnp.tril(jnp.ones((T, T), bool))
    sc = jnp.concatenate([dsa_index_scores(q_idx, w_idx, kidx_past),
                          jnp.where(causal[None], dsa_index_scores(q_idx, w_idx, k_idx), -jnp.inf)], axis=-1)
    top_s, top_i = lax.top_k(sc, topk)
    from_past = (top_i < n_past)[..., None, None]
    pi = jnp.minimum(top_i, n_past - 1)
    phys = jax.vmap(lambda pt, ix: pt[ix])(page_table, pi // PAGE_SIZE)
    slot = pi % PAGE_SIZE
    ni = jnp.clip(top_i - n_past, 0, T - 1)
    k_sel = jnp.where(from_past, pool_k[layer, phys, slot], gather_rows(k, ni))
    v_sel = jnp.where(from_past, pool_v[layer, phys, slot], gather_rows(v, ni))
    return dsa_attend(q, k_sel, v_sel, top_s > -jnp.inf)


def mla_scores(q_lat, q_rope, c, kr):
    s = jnp.einsum('bthc,bsc->bhts', q_lat, c) + jnp.einsum('bthr,bsr->bhts', q_rope, kr)
    return s.astype(jnp.float32) * MLA_SCALE


def mla_prompt(q_lat, q_rope, c, kr):
    B, T = q_lat.shape[:2]
    kpos = jnp.arange(T)

    def block(i):
        start = i * Q_BLOCK
        ql = lax.dynamic_slice_in_dim(q_lat, start, Q_BLOCK, axis=1)
        qr = lax.dynamic_slice_in_dim(q_rope, start, Q_BLOCK, axis=1)
        qpos = start + jnp.arange(Q_BLOCK)
        s = jnp.where(kpos[None, :] <= qpos[:, None], mla_scores(ql, qr, c, kr), -jnp.inf)
        p = jax.nn.softmax(s, axis=-1).astype(c.dtype)
        return jnp.einsum('bhts,bsc->bthc', p, c)

    o = lax.map(block, jnp.arange(T // Q_BLOCK))
    return jnp.moveaxis(o, 0, 1).reshape(B, T, D_HEADS, D_KV_RANK)


def mla_sample(q_lat, q_rope, c_new, kr_new, pool_ckv, pool_kr, layer, page_table):
    B, T = q_lat.shape[:2]
    n_past = page_table.shape[1] * PAGE_SIZE
    c_past = pool_ckv[layer, page_table].reshape(B, n_past, D_KV_RANK)
    kr_past = pool_kr[layer, page_table].reshape(B, n_past, D_ROPE)
    causal = jnp.tril(jnp.ones((T, T), bool))
    s = jnp.concatenate([mla_scores(q_lat, q_rope, c_past, kr_past),
                         jnp.where(causal, mla_scores(q_lat, q_rope, c_new, kr_new), -jnp.inf)], axis=-1)
    p = jax.nn.softmax(s, axis=-1).astype(c_new.dtype)
    return (jnp.einsum('bhts,bsc->bthc', p[..., :n_past], c_past)
            + jnp.einsum('bhts,bsc->bthc', p[..., n_past:], c_new))


def peer_ffn(x, wq, subkeys, u, v):
    n = x.shape[0]
    pad = (-n) % PEER_BLOCK
    xb = jnp.pad(x, ((0, pad), (0, 0))).reshape(-1, PEER_BLOCK, D_MODEL)

    def block(xt):
        q = (xt @ wq).reshape(PEER_BLOCK, PEER_HEADS, 2, PEER_DKEY // 2)
        s = jnp.einsum('nhpd,pkd->nhpk', q, subkeys).astype(jnp.float32)
        s1, i1 = lax.top_k(s[:, :, 0], PEER_TOPK)
        s2, i2 = lax.top_k(s[:, :, 1], PEER_TOPK)
        cand_s = (s1[..., :, None] + s2[..., None, :]).reshape(PEER_BLOCK, PEER_HEADS, PEER_TOPK * PEER_TOPK)
        cand_i = (i1[..., :, None] * PEER_NKEYS + i2[..., None, :]).reshape(PEER_BLOCK, PEER_HEADS, PEER_TOPK * PEER_TOPK)
        top_s, pos = lax.top_k(cand_s, PEER_TOPK)
        idx = jnp.take_along_axis(cand_i, pos, axis=-1)
        gate = jax.nn.softmax(top_s, axis=-1)
        act = jax.nn.gelu(jnp.einsum('nhkd,nd->nhk', u[idx], xt).astype(jnp.float32), approximate=False)
        return jnp.einsum('nhk,nhkd->nd', (gate * act).astype(v.dtype), v[idx])

    y = lax.map(block, xb)
    return y.reshape(-1, D_MODEL)[:n]


def setup_inputs(seed: int = 0) -> dict:
    key = jax.random.key(seed)
    ks = iter(jax.random.split(key, 64))
    f32 = jnp.float32

    def nrm(shape, scale):
        return jax.random.normal(next(ks), shape, f32) * scale

    def gain(shape):
        return 1.0 + nrm(shape, 0.05)

    n_pages = PAST_LEN // PAGE_SIZE
    n_used = DEC_BATCH * n_pages
    n_pool = n_used + n_used // 4
    page_table = jax.random.permutation(next(ks), n_pool)[:n_used].reshape(DEC_BATCH, n_pages).astype(jnp.int32)
    return {
        'x_prompt': nrm((BATCH, SEQ, D_MODEL), 1.0),
        'x_sample': nrm((DEC_BATCH, DEC_SEQ, D_MODEL), 1.0),
        'cache_dsa_k': nrm((N_EVEN, n_pool, PAGE_SIZE, B_KV_HEADS, B_DH), 1.0),
        'cache_dsa_v': nrm((N_EVEN, n_pool, PAGE_SIZE, B_KV_HEADS, B_DH), 1.0),
        'cache_dsa_idx': nrm((N_EVEN, n_pool, PAGE_SIZE, IDX_DIM), 1.0),
        'cache_mla_ckv': nrm((N_ODD, n_pool, PAGE_SIZE, D_KV_RANK), 1.0),
        'cache_mla_krope': nrm((N_ODD, n_pool, PAGE_SIZE, D_ROPE), 1.0),
        'state_hgrn': nrm((N_EVEN, DEC_BATCH, A_HEADS, A_DK, A_DV), 0.5),
        'state_rwkv': nrm((N_ODD, DEC_BATCH, C_HEADS, C_DH, C_DH), 0.3),
        'state_shift': nrm((N_ODD, DEC_BATCH, RWKV_IN), 1.0),
        'page_table': page_table,
        'norm_mix': gain((DEPTH, D_MODEL)),
        'norm_ffn': gain((DEPTH, D_MODEL)),
        'norm_final': gain((D_MODEL,)),
        'w_in_even': nrm((N_EVEN, D_MODEL, EVEN_IN), D_MODEL ** -0.5),
        'w_out_even': nrm((N_EVEN, EVEN_OUT, D_MODEL), EVEN_OUT ** -0.5),
        'hgrn_lb': nrm((N_EVEN, A_WIDTH), 0.5),
        'hgrn_norm': gain((N_EVEN, A_WIDTH)),
        'w_in_odd': nrm((N_ODD, D_MODEL, ODD_IN), D_MODEL ** -0.5),
        'w_out_odd': nrm((N_ODD, ODD_OUT, D_MODEL), ODD_OUT ** -0.5),
        'rwkv_mu': jax.random.uniform(next(ks), (N_ODD, RWKV_IN), f32),
        'rwkv_w0': jax.random.uniform(next(ks), (N_ODD, C_WIDTH), f32, minval=-5.0, maxval=1.0),
        'rwkv_w_up': nrm((N_ODD, LORA_W, C_WIDTH), LORA_W ** -0.5),
        'rwkv_a0': nrm((N_ODD, C_WIDTH), 0.1),
        'rwkv_a_up': nrm((N_ODD, LORA_A, C_WIDTH), LORA_A ** -0.5),
        'rwkv_g_up': nrm((N_ODD, LORA_G, C_WIDTH), LORA_G ** -0.5),
        'rwkv_k_k': 0.85 + nrm((N_ODD, C_WIDTH), 0.05),
        'rwkv_k_a': gain((N_ODD, C_WIDTH)),
        'rwkv_r_k': nrm((N_ODD, C_HEADS, C_DH), 0.1),
        'rwkv_ln_w': gain((N_ODD, C_WIDTH)),
        'rwkv_ln_b': nrm((N_ODD, C_WIDTH), 0.02),
        'mla_q_norm': gain((N_ODD, D_Q_RANK)),
        'mla_w_uq': nrm((N_ODD, D_Q_RANK, D_HEADS * (D_NOPE + D_ROPE)), D_Q_RANK ** -0.5),
        'mla_kv_norm': gain((N_ODD, D_KV_RANK)),
        'mla_w_uk': nrm((N_ODD, D_KV_RANK, D_HEADS, D_NOPE), D_KV_RANK ** -0.5),
        'mla_w_uv': nrm((N_ODD, D_KV_RANK, D_HEADS, D_VDIM), D_KV_RANK ** -0.5),
        'peer_wq': nrm((DEPTH, D_MODEL, PEER_HEADS * PEER_DKEY), D_MODEL ** -0.5),
        'peer_subkeys': nrm((DEPTH, 2, PEER_NKEYS, PEER_DKEY // 2), (PEER_DKEY // 2) ** -0.5),
        'peer_u': nrm((DEPTH, PEER_N, D_MODEL), D_MODEL ** -0.5),
        'peer_v': nrm((DEPTH, PEER_N, D_MODEL), (PEER_HEADS * PEER_TOPK) ** -0.5),
    }


def reference(x_prompt, x_sample, cache_dsa_k, cache_dsa_v, cache_dsa_idx, cache_mla_ckv, cache_mla_krope,
              state_hgrn, state_rwkv, state_shift, page_table,
              norm_mix, norm_ffn, norm_final, w_in_even, w_out_even, hgrn_lb, hgrn_norm,
              w_in_odd, w_out_odd, rwkv_mu, rwkv_w0, rwkv_w_up, rwkv_a0, rwkv_a_up, rwkv_g_up,
              rwkv_k_k, rwkv_k_a, rwkv_r_k, rwkv_ln_w, rwkv_ln_b,
              mla_q_norm, mla_w_uq, mla_kv_norm, mla_w_uk, mla_w_uv,
              peer_wq, peer_subkeys, peer_u, peer_v):
    f32 = jnp.float32
    lb_cum = jnp.cumsum(jax.nn.softmax(hgrn_lb.astype(f32), axis=0), axis=0)
    lower_bounds = lb_cum - lb_cum[:1]

    def trunk(x, sample):
        B, T, _ = x.shape
        dt = x.dtype
        cos, sin = rope_tables((PAST_LEN if sample else 0) + jnp.arange(T))
        n_k, n_v, n_i, n_c, n_r, n_h, n_s, n_sh = [], [], [], [], [], [], [], []
        for l in range(DEPTH):
            j = l // 2
            h = rmsnorm(x, norm_mix[l])
            if l % 2 == 0:
                qa, fa, ia, ga, qb, kb, vb, qi, ki, wi = split_cols(h @ w_in_even[j], EVEN_SIZES)
                heads = lambda t: t.reshape(B, T, A_HEADS, -1)
                lb = lower_bounds[j]
                f = lb + (1.0 - lb) * jax.nn.sigmoid(fa.astype(f32))
                s0 = state_hgrn[j].astype(f32) if sample else jnp.zeros((B, A_HEADS, A_DK, A_DV), f32)
                oa, s_a = hgrn2_chunked(heads(jax.nn.silu(qa.astype(f32))), heads(1.0 - f),
                                        heads(ia.astype(f32)), heads(jnp.log(f)), s0)
                oa = rmsnorm(oa, hgrn_norm[j].reshape(A_HEADS, A_DV)).reshape(B, T, A_WIDTH)
                oa = (oa * jax.nn.silu(ga.astype(f32))).astype(dt)
                qb = qb.reshape(B, T, B_HEADS, B_DH)
                kb = kb.reshape(B, T, B_KV_HEADS, B_DH)
                vb = vb.reshape(B, T, B_KV_HEADS, B_DH)
                qi = qi.reshape(B, T, IDX_HEADS, IDX_DIM)
                wi = wi * IDX_HEADS ** -0.5
                if sample:
                    ob = dsa_sample(qb, kb, vb, qi, wi, ki, cache_dsa_k, cache_dsa_v, cache_dsa_idx, j, page_table)
                else:
                    ob = dsa_prompt(qb, kb, vb, qi, wi, ki)
                y = jnp.concatenate([oa, ob.astype(dt)], axis=-1) @ w_out_even[j]
                n_k.append(kb)
                n_v.append(vb)
                n_i.append(ki)
                n_h.append(s_a.astype(dt))
            else:
                pc, qd, ckv, kr = split_cols(h @ w_in_odd[j], ODD_SIZES)
                prev0 = state_shift[j].astype(dt) if sample else jnp.zeros((B, RWKV_IN), dt)
                prev = jnp.concatenate([prev0[:, None], pc[:, :-1]], axis=1)
                xm = pc.astype(f32) + (prev - pc).astype(f32) * rwkv_mu[j].astype(f32)
                r, kc, vc, wd, ad, gd = split_cols(xm, RWKV_SIZES)
                w_log = -jax.nn.softplus(-(rwkv_w0[j].astype(f32) + jnp.tanh(wd) @ rwkv_w_up[j].astype(f32))) - 0.5
                decay = jnp.exp(-jnp.exp(w_log))
                a = jax.nn.sigmoid(rwkv_a0[j].astype(f32) + ad @ rwkv_a_up[j].astype(f32))
                g = jax.nn.sigmoid(gd) @ rwkv_g_up[j].astype(f32)
                ch = lambda t: t.reshape(B, T, C_HEADS, C_DH)
                kk = ch(kc * rwkv_k_k[j].astype(f32))
                kk = kk / jnp.maximum(jnp.sqrt(jnp.sum(kk * kk, axis=-1, keepdims=True)), 1e-12)
                kc = kc * (1.0 + (a - 1.0) * rwkv_k_a[j].astype(f32))
                r4, k4, v4 = ch(r), ch(kc), ch(vc)
                s0 = state_rwkv[j].astype(f32) if sample else jnp.zeros((B, C_HEADS, C_DH, C_DH), f32)
                yc, s_c = rwkv7_scan(r4, ch(decay), k4, v4, kk, ch(a), s0)
                yc = head_groupnorm(yc, rwkv_ln_w[j], rwkv_ln_b[j])
                bonus = jnp.sum(r4 * k4 * rwkv_r_k[j].astype(f32), axis=-1, keepdims=True) * v4
                yc = ((yc + bonus.reshape(B, T, C_WIDTH)) * g).astype(dt)
                cq = rmsnorm(qd, mla_q_norm[j])
                qh = (cq @ mla_w_uq[j]).reshape(B, T, D_HEADS, D_NOPE + D_ROPE)
                q_rope = apply_rope(qh[..., D_NOPE:], cos, sin)
                q_lat = jnp.einsum('bthn,chn->bthc', qh[..., :D_NOPE], mla_w_uk[j])
                c_new = rmsnorm(ckv, mla_kv_norm[j])
                kr_new = apply_rope(kr, cos, sin)
                if sample:
                    o_lat = mla_sample(q_lat, q_rope, c_new, kr_new, cache_mla_ckv, cache_mla_krope, j, page_table)
                else:
                    o_lat = mla_prompt(q_lat, q_rope, c_new, kr_new)
                od = jnp.einsum('bthc,chv->bthv', o_lat, mla_w_uv[j]).reshape(B, T, D_HEADS * D_VDIM)
                y = jnp.concatenate([yc, od.astype(dt)], axis=-1) @ w_out_odd[j]
                n_c.append(c_new)
                n_r.append(kr_new)
                n_s.append(s_c.astype(dt))
                n_sh.append(pc[:, -1])
            x = x + y.astype(dt)
            hf = rmsnorm(x, norm_ffn[l]).reshape(B * T, D_MODEL)
            x = x + peer_ffn(hf, peer_wq[l], peer_subkeys[l], peer_u[l], peer_v[l]).reshape(B, T, D_MODEL).astype(dt)
        return (rmsnorm(x, norm_final), jnp.stack(n_k), jnp.stack(n_v), jnp.stack(n_i), jnp.stack(n_c),
                jnp.stack(n_r), jnp.stack(n_h), jnp.stack(n_s), jnp.stack(n_sh))

    y_prompt, p_dsa_k, p_dsa_v, p_dsa_idx, p_mla_ckv, p_mla_krope, p_hgrn, p_rwkv, p_shift = trunk(x_prompt, False)
    y_sample, s_dsa_k, s_dsa_v, s_dsa_idx, s_mla_ckv, s_mla_krope, s_hgrn, s_rwkv, s_shift = trunk(x_sample, True)
    return (y_prompt, y_sample,
            p_dsa_k, p_dsa_v, p_dsa_idx, p_mla_ckv, p_mla_krope, p_hgrn, p_rwkv, p_shift,
            s_dsa_k, s_dsa_v, s_dsa_idx, s_mla_ckv, s_mla_krope, s_hgrn, s_rwkv, s_shift)
```

```python
import functools
import math

import jax
import jax.numpy as jnp
from jax import lax
from jax.experimental import pallas as pl
from jax.experimental.pallas import tpu as pltpu

F32 = jnp.float32
I32 = jnp.int32
MXU_DTYPE = jnp.bfloat16
LANES = 128
SUBLANES = 8
VMEM_LIMIT_BYTES = 56 * 1024 * 1024
NEG_INF = float("-inf")

RMS_EPS = 1e-6
PAGE_SIZE = 128
ROPE_THETA = 10000.0
PEER_HEADS = 8
PEER_NKEYS = 128
PEER_TOPK = 16
PEER_TOKEN_BLOCK = 512
PEER_GROUP = 8


def _cparams(*sem):
    return pltpu.CompilerParams(dimension_semantics=sem, vmem_limit_bytes=VMEM_LIMIT_BYTES)


def _round_up(n, m):
    return -(-n // m) * m


def _rms(x, g):
    return x * lax.rsqrt(jnp.mean(x * x, axis=-1, keepdims=True) + RMS_EPS) * g


def _dot(a, b):
    return jnp.dot(a, b, preferred_element_type=F32)


def _dot_nt(a, b):
    return lax.dot_general(a, b, (((1,), (1,)), ((), ())), preferred_element_type=F32)


def _matmul_kernel(*refs, norm, residual):
    it = iter(refs)
    a_ref = next(it)
    g_ref = next(it) if norm else None
    w_ref = next(it)
    r_ref = next(it) if residual else None
    o_ref = next(it)
    a = a_ref[...]
    if norm:
        a = _rms(a, g_ref[...])
    acc = _dot(a.astype(w_ref.dtype), w_ref[...])
    if residual:
        acc = acc + r_ref[...]
    o_ref[...] = acc


def _matmul(a, w, g=None, res=None, tm=256):
    n, k = a.shape
    m = w.shape[1]
    ins, specs = [a], [pl.BlockSpec((tm, k), lambda i: (i, 0))]
    if g is not None:
        ins.append(g.reshape(1, k).astype(F32))
        specs.append(pl.BlockSpec((1, k), lambda i: (0, 0)))
    ins.append(w)
    specs.append(pl.BlockSpec((k, m), lambda i: (0, 0)))
    if res is not None:
        ins.append(res)
        specs.append(pl.BlockSpec((tm, m), lambda i: (i, 0)))
    return pl.pallas_call(
        functools.partial(_matmul_kernel, norm=g is not None, residual=res is not None),
        out_shape=jax.ShapeDtypeStruct((n, m), F32),
        grid=(n // tm,),
        in_specs=specs,
        out_specs=pl.BlockSpec((tm, m), lambda i: (i, 0)),
        compiler_params=_cparams("parallel"),
        name="proj_matmul",
    )(*ins)


def _rmsnorm_kernel(x_ref, g_ref, o_ref):
    o_ref[...] = _rms(x_ref[...], g_ref[...])


def _rmsnorm(x, g, tm=512):
    n, d = x.shape
    return pl.pallas_call(
        _rmsnorm_kernel,
        out_shape=jax.ShapeDtypeStruct((n, d), F32),
        grid=(n // tm,),
        in_specs=[pl.BlockSpec((tm, d), lambda i: (i, 0)), pl.BlockSpec((1, d), lambda i: (0, 0))],
        out_specs=pl.BlockSpec((tm, d), lambda i: (i, 0)),
        compiler_params=_cparams("parallel"),
        name="final_rmsnorm",
    )(x, g.reshape(1, d).astype(F32))


def _peer_fold_kernel(sub_ref, wq_ref, o_ref):
    o_ref[...] = _dot_nt(sub_ref[...], wq_ref[...])


def _peer_fold(subkeys, wq):
    d_model = wq.shape[0]
    half = subkeys.shape[2]
    return pl.pallas_call(
        _peer_fold_kernel,
        out_shape=jax.ShapeDtypeStruct((2, PEER_HEADS, PEER_NKEYS, d_model), F32),
        grid=(2, PEER_HEADS),
        in_specs=[pl.BlockSpec((None, PEER_NKEYS, half), lambda p, h: (p, 0, 0)),
                  pl.BlockSpec((d_model, half), lambda p, h: (0, h * 2 + p))],
        out_specs=pl.BlockSpec((None, None, PEER_NKEYS, d_model), lambda p, h: (p, h, 0, 0)),
        compiler_params=_cparams("parallel", "parallel"),
        name="peer_fold",
    )(subkeys.astype(MXU_DTYPE), wq.astype(MXU_DTYPE))


def _peer_scores_kernel(x_ref, g_ref, m_ref, s_ref, h_ref):
    hb = _rms(x_ref[...], g_ref[...]).astype(MXU_DTYPE)
    s_ref[...] = _dot_nt(m_ref[...], hb)
    h_ref[...] = hb.T


def _peer_scores(x, g, mcat, tm=PEER_TOKEN_BLOCK):
    n, d = x.shape
    rows = mcat.shape[0]
    return pl.pallas_call(
        _peer_scores_kernel,
        out_shape=(jax.ShapeDtypeStruct((rows, n), F32), jax.ShapeDtypeStruct((d, n), MXU_DTYPE)),
        grid=(n // tm,),
        in_specs=[pl.BlockSpec((tm, d), lambda i: (i, 0)),
                  pl.BlockSpec((1, d), lambda i: (0, 0)),
                  pl.BlockSpec((rows, d), lambda i: (0, 0))],
        out_specs=(pl.BlockSpec((rows, tm), lambda i: (0, i)), pl.BlockSpec((d, tm), lambda i: (0, i))),
        compiler_params=_cparams("parallel"),
        name="peer_scores",
    )(x, g.reshape(1, d).astype(F32), mcat)


def _peer_topk_kernel(s_ref, aux_ref, e2_ref):
    hk = PEER_HEADS * PEER_NKEYS
    tn = s_ref.shape[1]

    def top_values(base):
        vals, prev = [], None
        for _ in range(PEER_TOPK):
            def body(i, m, prev=prev):
                x = s_ref[pl.ds(pl.multiple_of(base + i * PEER_HEADS, PEER_HEADS), PEER_HEADS), :]
                if prev is not None:
                    x = jnp.where(x < prev, x, NEG_INF)
                return jnp.maximum(m, x)
            prev = lax.fori_loop(0, PEER_NKEYS, body, jnp.full((PEER_HEADS, tn), NEG_INF, F32), unroll=8)
            vals.append(prev)
        return vals

    a = top_values(0)
    b = top_values(hk)
    cands = [a[k] + b[l] for k in range(PEER_TOPK) for l in range(PEER_TOPK) if (k + 1) * (l + 1) <= PEER_TOPK]
    top = a[0] + b[0]
    tau = top
    for _ in range(PEER_TOPK - 1):
        nxt = jnp.full_like(tau, NEG_INF)
        for c in cands:
            nxt = jnp.maximum(nxt, jnp.where(c < tau, c, NEG_INF))
        tau = nxt
    z = jnp.zeros_like(tau)
    for c in cands:
        z = z + jnp.where(c >= tau, jnp.exp(c - top), 0.0)
    aux_ref[0:PEER_HEADS, :] = tau
    aux_ref[PEER_HEADS:2 * PEER_HEADS, :] = a[0]
    aux_ref[2 * PEER_HEADS:3 * PEER_HEADS, :] = 1.0 / z
    for h in range(PEER_HEADS):
        rows = pl.ds(2 * hk + h * PEER_NKEYS, PEER_NKEYS)
        e2_ref[h * PEER_NKEYS:(h + 1) * PEER_NKEYS, :] = jnp.exp(s_ref[rows, :] - b[0][h:h + 1, :])


def _peer_topk(s_t, tn=PEER_TOKEN_BLOCK):
    rows, n = s_t.shape
    hk = PEER_HEADS * PEER_NKEYS
    return pl.pallas_call(
        _peer_topk_kernel,
        out_shape=(jax.ShapeDtypeStruct((3 * PEER_HEADS, n), F32), jax.ShapeDtypeStruct((hk, n), F32)),
        grid=(n // tn,),
        in_specs=[pl.BlockSpec((rows, tn), lambda i: (0, i))],
        out_specs=(pl.BlockSpec((3 * PEER_HEADS, tn), lambda i: (0, i)), pl.BlockSpec((hk, tn), lambda i: (0, i))),
        compiler_params=_cparams("parallel"),
        name="peer_topk",
    )(s_t)


def _gelu(x):
    return 0.5 * x * (1.0 + lax.erf(x * (1.0 / math.sqrt(2.0))))


def _peer_experts_kernel(h_ref, s1_ref, s2_ref, e2_ref, aux_ref, u_ref, v_ref, o_ref):
    @pl.when(pl.program_id(1) == 0)
    def _():
        o_ref[...] = jnp.zeros_like(o_ref)

    tau = aux_ref[0:PEER_HEADS, :]
    m1 = aux_ref[PEER_HEADS:2 * PEER_HEADS, :]
    inv_z = aux_ref[2 * PEER_HEADS:3 * PEER_HEADS, :]
    hb = h_ref[...]

    def pair(p, carry):
        acts = []
        for q in range(2):
            li = p * 2 + q
            s1 = s1_ref[pl.ds(pl.multiple_of(li * PEER_HEADS, PEER_HEADS), PEER_HEADS), :]
            e1 = jnp.exp(s1 - m1) * inv_z
            st = _dot(u_ref[li], hb)
            gate = jnp.zeros_like(st)
            for h in range(PEER_HEADS):
                rows = slice(h * PEER_NKEYS, (h + 1) * PEER_NKEYS)
                sel = (s2_ref[rows, :] + s1[h:h + 1, :]) >= tau[h:h + 1, :]
                gate = gate + jnp.where(sel, e2_ref[rows, :], 0.0) * e1[h:h + 1, :]
            acts.append((_gelu(st) * gate).astype(MXU_DTYPE))
        o_ref[...] += _dot(v_ref[p], jnp.concatenate(acts, axis=0))
        return carry

    lax.fori_loop(0, PEER_GROUP // 2, pair, 0)


def _peer_experts(h_t, s_t, e2_t, aux, u3, v3, tn=PEER_TOKEN_BLOCK):
    d, n = h_t.shape
    hk = PEER_HEADS * PEER_NKEYS
    groups = PEER_NKEYS // PEER_GROUP
    return pl.pallas_call(
        _peer_experts_kernel,
        out_shape=jax.ShapeDtypeStruct((d, n), F32),
        grid=(n // tn, groups),
        in_specs=[pl.BlockSpec((d, tn), lambda i, g: (0, i)),
                  pl.BlockSpec((PEER_GROUP * PEER_HEADS, tn), lambda i, g: (g, i)),
                  pl.BlockSpec((hk, tn), lambda i, g: (2, i)),
                  pl.BlockSpec((hk, tn), lambda i, g: (0, i)),
                  pl.BlockSpec((3 * PEER_HEADS, tn), lambda i, g: (0, i)),
                  pl.BlockSpec((PEER_GROUP, PEER_NKEYS, d), lambda i, g: (g, 0, 0)),
                  pl.BlockSpec((PEER_GROUP // 2, d, 2 * PEER_NKEYS), lambda i, g: (g, 0, 0))],
        out_specs=pl.BlockSpec((d, tn), lambda i, g: (0, i)),
        compiler_params=_cparams("parallel", "arbitrary"),
        name="peer_experts",
    )(h_t, s_t, s_t, e2_t, aux, u3, v3)


def _peer_prepare(wq, subkeys, u, v):
    d = wq.shape[0]
    hk = PEER_HEADS * PEER_NKEYS
    mf = _peer_fold(subkeys, wq)
    inter = jnp.transpose(mf, (0, 2, 1, 3)).reshape(2 * hk, d)
    mcat = jnp.concatenate([inter, mf[1].reshape(hk, d)], axis=0).astype(MXU_DTYPE)
    u3 = u.astype(MXU_DTYPE).reshape(PEER_NKEYS, PEER_NKEYS, d)
    v3 = jnp.transpose(v.astype(MXU_DTYPE).reshape(PEER_NKEYS // 2, 2 * PEER_NKEYS, d), (0, 2, 1))
    return mcat, u3, v3


def _peer_ffn(x, g, prep):
    mcat, u3, v3 = prep
    s_t, h_t = _peer_scores(x, g, mcat)
    aux, e2_t = _peer_topk(s_t)
    y_t = _peer_experts(h_t, s_t, e2_t, aux, u3, v3)
    return x + y_t.T


HGRN_CHUNK = 64
HGRN_SUB = 16
MASKED_EXPONENT = -1e30


def _cumsum_rows(x):
    rows = x.shape[0]
    row = lax.broadcasted_iota(I32, x.shape, 0)
    d = 1
    while d < rows:
        x = x + jnp.where(row >= d, pltpu.roll(x, d, 0), 0.0)
        d *= 2
    return x


def _hgrn_chunk(q, k, v, g, s_t, sub):
    c = q.shape[0]
    cum = _cumsum_rows(g)
    o = _dot_nt((q * jnp.exp(cum)).astype(MXU_DTYPE), s_t.astype(MXU_DTYPE))
    outs = []
    for blk in range(c // sub):
        r0 = blk * sub
        q_b, cum_b, k_b, v_b = q[r0:r0 + sub], cum[r0:r0 + sub], k[r0:r0 + sub], v[r0:r0 + sub]
        o_b = o[r0:r0 + sub]
        if blk > 0:
            base = cum[r0 - 1:r0]
            qs = q_b * jnp.exp(cum_b - base)
            ks = k[0:r0] * jnp.exp(base - cum[0:r0])
            att = _dot_nt(qs.astype(MXU_DTYPE), ks.astype(MXU_DTYPE))
            o_b = o_b + _dot(att.astype(MXU_DTYPE), v[0:r0].astype(MXU_DTYPE))
        row = lax.broadcasted_iota(I32, (sub, q.shape[1]), 0)
        for s in range(sub):
            dec = jnp.exp(jnp.where(row >= s, cum_b - cum_b[s:s + 1], MASKED_EXPONENT))
            att = jnp.sum(q_b * k_b[s:s + 1] * dec, axis=1, keepdims=True)
            o_b = o_b + att * v_b[s:s + 1]
        outs.append(o_b)
    o = outs[0] if len(outs) == 1 else jnp.concatenate(outs, axis=0)
    last = cum[c - 1:c]
    kd = k * jnp.exp(last - cum)
    upd = lax.dot_general(v.astype(MXU_DTYPE), kd.astype(MXU_DTYPE), (((0,), (0,)), ((), ())),
                          preferred_element_type=F32)
    return o, s_t * jnp.exp(last) + upd


def _hgrn_kernel(q_ref, f_ref, i_ref, g_ref, lb_ref, gain_ref, s0_ref, o_ref, s_ref, *, chunk, sub):
    @pl.when(pl.program_id(2) == 0)
    def _():
        s_ref[...] = s0_ref[...]

    lb = lb_ref[...]
    s_t = s_ref[...]
    for c0 in range(0, q_ref.shape[0], chunk):
        rows = slice(c0, c0 + chunk)
        f = lb + (1.0 - lb) * jax.nn.sigmoid(f_ref[rows, :])
        o, s_t = _hgrn_chunk(jax.nn.silu(q_ref[rows, :]), 1.0 - f, i_ref[rows, :], jnp.log(f), s_t, sub)
        o_ref[rows, :] = _rms(o, gain_ref[...]) * jax.nn.silu(g_ref[rows, :])
    s_ref[...] = s_t


def _hgrn(proj, row_start, batch, t, heads, lb, gain, s0_t):
    chunk = math.gcd(t, HGRN_CHUNK)
    sub = min(HGRN_SUB, chunk)
    tc = min(t, 4 * chunk)
    nt = t // tc
    rb0 = row_start // tc
    dk = LANES

    def slab(k):
        return pl.BlockSpec((tc, dk), lambda b, h, i, k=k: (rb0 + b * nt + i, k * heads + h))

    vec = pl.BlockSpec((1, dk), lambda b, h, i: (0, h))
    st = pl.BlockSpec((None, None, dk, dk), lambda b, h, i: (b, h, 0, 0))
    return pl.pallas_call(
        functools.partial(_hgrn_kernel, chunk=chunk, sub=sub),
        out_shape=(jax.ShapeDtypeStruct((batch * t, heads * dk), F32),
                   jax.ShapeDtypeStruct((batch, heads, dk, dk), F32)),
        grid=(batch, heads, nt),
        in_specs=[slab(0), slab(1), slab(2), slab(3), vec, vec, st],
        out_specs=(pl.BlockSpec((tc, dk), lambda b, h, i: (b * nt + i, h)), st),
        compiler_params=_cparams("parallel", "parallel", "arbitrary"),
        name="hgrn2",
    )(proj, proj, proj, proj, lb.reshape(1, -1), gain.reshape(1, -1), s0_t)


DSA_TOPK_MAX = 256
INT32_MIN = -2 ** 31


def _count(m):
    return jnp.sum(jnp.where(m, 1.0, 0.0), axis=1, keepdims=True)


def _topk_mask(scores, k, idx_bits):
    u = lax.bitcast_convert_type(scores, I32)
    key = u ^ ((u >> 31) & I32(0x7FFFFFFF))
    kf = float(k)
    v = jnp.where(_count(key >= 0) >= kf, I32(0), I32(INT32_MIN))

    def value_bit(it, v):
        t = v | (I32(1) << (I32(30) - it))
        return jnp.where(_count(key >= t) >= kf, t, v)

    v = lax.fori_loop(0, 31, value_bit, v)
    above = key > v
    tied = key == v
    need = kf - _count(above)
    idx = lax.broadcasted_iota(I32, scores.shape, 1)

    def index_bit(it, j):
        t = j | (I32(1) << (I32(idx_bits - 1) - it))
        return jnp.where(_count(tied & (idx < t)) < need, t, j)

    j = lax.fori_loop(0, idx_bits, index_bit, jnp.zeros_like(v))
    return (above | (tied & (idx <= j))) & (scores > NEG_INF)


def _index_scores(qi, w, ki):
    heads, _, d = qi.shape
    acc = None
    for h in range(heads):
        dots = _dot_nt(qi[h], ki) * (d ** -0.5)
        term = jnp.maximum(dots, 0.0) * (w[:, h:h + 1] * (heads ** -0.5))
        acc = term if acc is None else acc + term
    return acc


def _masked_softmax_pv(s, mask, v):
    s = jnp.where(mask, s, NEG_INF)
    p = jnp.exp(s - jnp.max(s, axis=1, keepdims=True))
    return _dot(p.astype(MXU_DTYPE), v) / jnp.sum(p, axis=1, keepdims=True)


def _dsa_prompt_kernel(qi_ref, w_ref, ki_ref, q_ref, k_ref, v_ref, o_ref, *, topk):
    tq = q_ref.shape[1]
    t = ki_ref.shape[0]
    qpos = pl.program_id(1) * tq + lax.broadcasted_iota(I32, (tq, t), 0)
    kpos = lax.broadcasted_iota(I32, (tq, t), 1)
    scores = jnp.where(kpos <= qpos, _index_scores(qi_ref[...], w_ref[...], ki_ref[...]), NEG_INF)
    mask = _topk_mask(scores, topk, max(1, (t - 1).bit_length()))
    heads, kv_heads, dh = q_ref.shape[0], k_ref.shape[0], q_ref.shape[2]
    for h in range(heads):
        j = h // (heads // kv_heads)
        s = _dot_nt(q_ref[h], k_ref[j]) * (dh ** -0.5)
        o_ref[h] = _masked_softmax_pv(s, mask, v_ref[j])


def _dsa_prompt(qi4, wi, ki, q4, k4, v4, tq=128):
    b, hi, t, d = qi4.shape
    _, h, _, dh = q4.shape
    hkv = k4.shape[1]
    topk = min(DSA_TOPK_MAX, t // 4)
    return pl.pallas_call(
        functools.partial(_dsa_prompt_kernel, topk=topk),
        out_shape=jax.ShapeDtypeStruct((b, h, t, dh), F32),
        grid=(b, t // tq),
        in_specs=[pl.BlockSpec((None, hi, tq, d), lambda n, i: (n, 0, i, 0)),
                  pl.BlockSpec((None, tq, hi), lambda n, i: (n, i, 0)),
                  pl.BlockSpec((None, t, d), lambda n, i: (n, 0, 0)),
                  pl.BlockSpec((None, h, tq, dh), lambda n, i: (n, 0, i, 0)),
                  pl.BlockSpec((None, hkv, t, dh), lambda n, i: (n, 0, 0, 0)),
                  pl.BlockSpec((None, hkv, t, dh), lambda n, i: (n, 0, 0, 0))],
        out_specs=pl.BlockSpec((None, h, tq, dh), lambda n, i: (n, 0, i, 0)),
        compiler_params=_cparams("parallel", "arbitrary"),
        name="dsa_prompt",
    )(qi4, wi, ki, q4, k4, v4)


PAGES_PER_STEP = 8
ONLINE_SOFTMAX_FLOOR = -1e30


def _page_specs(block, layer, pages):
    zeros = (0,) * (len(block) - 2)
    return [pl.BlockSpec(block, lambda b, c, pt, r=r: (layer, pt[b, c * pages + r]) + zeros) for r in range(pages)]


def _dsa_select_kernel(pt_ref, qi_ref, w_ref, kn_ref, *rest, pages, topk, t_new):
    page_refs, (mask_ref, sc_ref) = rest[:pages], rest[pages:]
    c = pl.program_id(1)
    nc = pl.num_programs(1)
    kc = jnp.concatenate([r[...] for r in page_refs], axis=0).astype(MXU_DTYPE)
    sc_ref[c] = _index_scores(qi_ref[...], w_ref[...], kc)

    @pl.when(c == nc - 1)
    def _():
        n_chunks, rows, ch = sc_ref.shape
        new = _index_scores(qi_ref[...], w_ref[...], kn_ref[...])
        qpos = lax.broadcasted_iota(I32, new.shape, 0)
        kpos = lax.broadcasted_iota(I32, new.shape, 1)
        new = jnp.where((kpos <= qpos) & (kpos < t_new), new, NEG_INF)
        pieces = [sc_ref[i] for i in range(n_chunks)] + [new]
        if ch > new.shape[1]:
            pieces.append(jnp.full((rows, ch - new.shape[1]), NEG_INF, F32))
        scores = jnp.concatenate(pieces, axis=1)
        mask = _topk_mask(scores, topk, scores.shape[1].bit_length())
        for i in range(n_chunks + 1):
            mask_ref[i] = jnp.where(mask[:, i * ch:(i + 1) * ch], 1.0, 0.0)


def _dsa_select(page_table, qi4, wi, ki_new_pad, pool_idx, layer, topk, t_new):
    b, n_pages = page_table.shape
    pages = math.gcd(n_pages, PAGES_PER_STEP)
    nc = n_pages // pages
    ch = pages * PAGE_SIZE
    _, hi, t, d = qi4.shape
    grid_spec = pltpu.PrefetchScalarGridSpec(
        num_scalar_prefetch=1,
        grid=(b, nc),
        in_specs=[pl.BlockSpec((None, hi, t, d), lambda n, c, pt: (n, 0, 0, 0)),
                  pl.BlockSpec((None, t, hi), lambda n, c, pt: (n, 0, 0)),
                  pl.BlockSpec((None, LANES, d), lambda n, c, pt: (n, 0, 0))]
        + _page_specs((None, None, PAGE_SIZE, d), layer, pages),
        out_specs=pl.BlockSpec((None, nc + 1, t, ch), lambda n, c, pt: (n, 0, 0, 0)),
        scratch_shapes=[pltpu.VMEM((nc, t, ch), F32)],
    )
    return pl.pallas_call(
        functools.partial(_dsa_select_kernel, pages=pages, topk=topk, t_new=t_new),
        out_shape=jax.ShapeDtypeStruct((b, nc + 1, t, ch), F32),
        grid_spec=grid_spec,
        compiler_params=_cparams("parallel", "arbitrary"),
        name="dsa_sample_select",
    )(page_table, qi4, wi, ki_new_pad, *([pool_idx] * pages))


def _online_softmax_step(m_ref, l_ref, acc_ref, j, s, mask, v):
    m_old = m_ref[j]
    m_new = jnp.maximum(m_old, jnp.max(jnp.where(mask, s, ONLINE_SOFTMAX_FLOOR), axis=1, keepdims=True))
    alpha = jnp.exp(m_old - m_new)
    p = jnp.where(mask, jnp.exp(s - m_new), 0.0)
    l_ref[j] = alpha * l_ref[j] + jnp.sum(p, axis=1, keepdims=True)
    acc_ref[j] = alpha * acc_ref[j] + _dot(p.astype(MXU_DTYPE), v)
    m_ref[j] = m_new


def _dsa_attend_kernel(pt_ref, q_ref, mc_ref, mn_ref, kn_ref, vn_ref, *rest, pages):
    k_pages, v_pages = rest[:pages], rest[pages:2 * pages]
    o_ref, m_ref, l_ref, acc_ref = rest[2 * pages:]
    c = pl.program_id(1)
    kv_heads, rows, dh = q_ref.shape
    scale = dh ** -0.5

    @pl.when(c == 0)
    def _():
        m_ref[...] = jnp.full_like(m_ref, ONLINE_SOFTMAX_FLOOR)
        l_ref[...] = jnp.zeros_like(l_ref)
        acc_ref[...] = jnp.zeros_like(acc_ref)

    reps = rows // mc_ref.shape[0]
    mask = jnp.concatenate([mc_ref[...]] * reps, axis=0) > 0.0
    for j in range(kv_heads):
        kc = jnp.concatenate([r[:, j, :] for r in k_pages], axis=0).astype(MXU_DTYPE)
        vc = jnp.concatenate([r[:, j, :] for r in v_pages], axis=0).astype(MXU_DTYPE)
        _online_softmax_step(m_ref, l_ref, acc_ref, j, _dot_nt(q_ref[j], kc) * scale, mask, vc)

    @pl.when(c == pl.num_programs(1) - 1)
    def _():
        mask_new = jnp.concatenate([mn_ref[:, 0:LANES]] * reps, axis=0) > 0.0
        for j in range(kv_heads):
            _online_softmax_step(m_ref, l_ref, acc_ref, j, _dot_nt(q_ref[j], kn_ref[j]) * scale, mask_new, vn_ref[j])
            o_ref[j] = acc_ref[j] / l_ref[j]


def _dsa_attend(page_table, q4, mask, k_new_pad, v_new_pad, pool_k, pool_v, layer):
    b, n_pages = page_table.shape
    pages = math.gcd(n_pages, PAGES_PER_STEP)
    nc = n_pages // pages
    _, hkv, rows, dh = q4.shape
    _, _, t, ch = mask.shape
    grid_spec = pltpu.PrefetchScalarGridSpec(
        num_scalar_prefetch=1,
        grid=(b, nc),
        in_specs=[pl.BlockSpec((None, hkv, rows, dh), lambda n, c, pt: (n, 0, 0, 0)),
                  pl.BlockSpec((None, None, t, ch), lambda n, c, pt: (n, c, 0, 0)),
                  pl.BlockSpec((None, None, t, ch), lambda n, c, pt: (n, nc, 0, 0)),
                  pl.BlockSpec((None, hkv, LANES, dh), lambda n, c, pt: (n, 0, 0, 0)),
                  pl.BlockSpec((None, hkv, LANES, dh), lambda n, c, pt: (n, 0, 0, 0))]
        + _page_specs((None, None, PAGE_SIZE, hkv, dh), layer, pages)
        + _page_specs((None, None, PAGE_SIZE, hkv, dh), layer, pages),
        out_specs=pl.BlockSpec((None, hkv, rows, dh), lambda n, c, pt: (n, 0, 0, 0)),
        scratch_shapes=[pltpu.VMEM((hkv, rows, 1), F32), pltpu.VMEM((hkv, rows, 1), F32),
                        pltpu.VMEM((hkv, rows, dh), F32)],
    )
    return pl.pallas_call(
        functools.partial(_dsa_attend_kernel, pages=pages),
        out_shape=jax.ShapeDtypeStruct((b, hkv, rows, dh), F32),
        grid_spec=grid_spec,
        compiler_params=_cparams("parallel", "arbitrary"),
        name="dsa_sample_attend",
    )(page_table, q4, mask, mask, k_new_pad, v_new_pad, *([pool_k] * pages), *([pool_v] * pages))


GN_EPS = 64e-5
RWKV_HEAD = 64


def _dot_f32(a, b):
    return jnp.dot(a, b, preferred_element_type=F32, precision=lax.Precision.HIGHEST)


def _rwkv_prep_kernel(pc_ref, prev_ref, mu_ref, vec_ref, lora_ref, gup_ref, seg_ref,
                      r_ref, w_ref, k_ref, v_ref, kk_ref, b_ref, g_ref, bonus_ref):
    cw = r_ref.shape[1]
    pc = pc_ref[...]
    xm = pc + (prev_ref[...] - pc) * mu_ref[...]
    r, kc, vc = xm[:, 0:cw], xm[:, cw:2 * cw], xm[:, 2 * cw:3 * cw]
    wa = xm[:, 3 * cw:3 * cw + LANES]
    gd = xm[:, 3 * cw + LANES:]
    lane = lax.broadcasted_iota(I32, wa.shape, 1)
    wa = jnp.where(lane < LANES // 2, jnp.tanh(wa), wa)
    lo = _dot(wa.astype(MXU_DTYPE), lora_ref[...])
    w0, a0, k_k, k_a, r_k = (vec_ref[i:i + 1, :] for i in range(5))
    w_log = -jax.nn.softplus(-(w0 + lo[:, 0:cw])) - 0.5
    a = jax.nn.sigmoid(a0 + lo[:, cw:2 * cw])
    kk = kc * k_k
    norm = jnp.sqrt(_dot_f32(kk * kk, seg_ref[...]))
    kk = kk / jnp.maximum(norm, 1e-12)
    kc = kc * (1.0 + (a - 1.0) * k_a)
    r_ref[...] = r
    w_ref[...] = jnp.exp(-jnp.exp(w_log))
    k_ref[...] = kc
    v_ref[...] = vc
    kk_ref[...] = kk
    b_ref[...] = kk * a
    g_ref[...] = _dot(jax.nn.sigmoid(gd).astype(MXU_DTYPE), gup_ref[...])
    bonus_ref[...] = _dot_f32(r * kc * r_k, seg_ref[...]) * vc


def _rwkv_prep(pc, prev, mu, vecs, lora, g_up, seg, tm):
    n, width = prev.shape
    cw = vecs.shape[1]
    row = lambda w: pl.BlockSpec((tm, w), lambda i: (i, 0))
    full = lambda a: pl.BlockSpec(a.shape, lambda i: (0, 0))
    out = jax.ShapeDtypeStruct((n, cw), F32)
    return pl.pallas_call(
        _rwkv_prep_kernel,
        out_shape=(out,) * 8,
        grid=(n // tm,),
        in_specs=[row(width), row(width), full(mu), full(vecs), full(lora), full(g_up), full(seg)],
        out_specs=(row(cw),) * 8,
        compiler_params=_cparams("parallel"),
        name="rwkv_prep",
    )(pc, prev, mu, vecs, lora, g_up, seg)


def _rwkv_scan_kernel(r_ref, w_ref, k_ref, v_ref, kk_ref, b_ref, s0_ref, y_ref, s_ref):
    @pl.when(pl.program_id(1) == 0)
    def _():
        s_ref[...] = s0_ref[...]

    pairs, dv, width = s_ref.shape
    first = lax.broadcasted_iota(I32, (dv, width), 1) < RWKV_HEAD
    own = (lax.broadcasted_iota(I32, (2, width), 0) == 0) == (lax.broadcasted_iota(I32, (2, width), 1) < RWKV_HEAD)

    def group(g, carry):
        t0 = pl.multiple_of(g * SUBLANES, SUBLANES)
        rows = pl.ds(t0, SUBLANES)
        r8, w8, k8, kk8, b8 = (ref[rows, :] for ref in (r_ref, w_ref, k_ref, kk_ref, b_ref))
        v_t = v_ref[rows, :].T
        for p in range(pairs):
            lanes = slice(p * width, (p + 1) * width)
            s = s_ref[p]
            for i in range(SUBLANES):
                prod = s * kk8[i:i + 1, lanes]
                sa0 = jnp.sum(jnp.where(first, prod, 0.0), axis=1, keepdims=True)
                sa1 = jnp.sum(jnp.where(first, 0.0, prod), axis=1, keepdims=True)
                sa = -jnp.where(first, sa0, sa1)
                c0 = p * width
                vcol = jnp.where(first, v_t[c0:c0 + RWKV_HEAD, i:i + 1], v_t[c0 + RWKV_HEAD:c0 + width, i:i + 1])
                s = s * w8[i:i + 1, lanes] + sa * b8[i:i + 1, lanes] + vcol * k8[i:i + 1, lanes]
                r2 = jnp.where(own, r8[i:i + 1, lanes], 0.0)
                y = _dot_nt(r2.astype(MXU_DTYPE), s.astype(MXU_DTYPE))
                y_ref[pl.ds(t0 + i, 1), 2 * p:2 * p + 2, :] = y[None]
            s_ref[p] = s
        return carry

    lax.fori_loop(0, r_ref.shape[0] // SUBLANES, group, 0)


def _rwkv_scan(ins, row_start, batch, t, s0_packed):
    cw = ins[0].shape[1]
    heads = cw // RWKV_HEAD
    tb = min(t, 64)
    nt = t // tb
    rb0 = row_start // tb
    row = pl.BlockSpec((tb, cw), lambda b, i: (rb0 + b * nt + i, 0))
    st = pl.BlockSpec((None,) + s0_packed.shape[1:], lambda b, i: (b, 0, 0, 0))
    return pl.pallas_call(
        _rwkv_scan_kernel,
        out_shape=(jax.ShapeDtypeStruct((batch * t, heads, RWKV_HEAD), F32),
                   jax.ShapeDtypeStruct(s0_packed.shape, F32)),
        grid=(batch, nt),
        in_specs=[row] * 6 + [st],
        out_specs=(pl.BlockSpec((tb, heads, RWKV_HEAD), lambda b, i: (b * nt + i, 0, 0)), st),
        compiler_params=_cparams("parallel", "arbitrary"),
        name="rwkv_scan",
    )(*ins, s0_packed)


def _rwkv_post_kernel(y_ref, bonus_ref, g_ref, ln_ref, seg_ref, o_ref):
    y = y_ref[...]
    avg = seg_ref[...] * (1.0 / RWKV_HEAD)
    d = y - _dot_f32(y, avg)
    var = _dot_f32(d * d, avg)
    yn = d * lax.rsqrt(var + GN_EPS) * ln_ref[0:1, :] + ln_ref[1:2, :]
    o_ref[...] = (yn + bonus_ref[...]) * g_ref[...]


def _rwkv_post(y, bonus, g, ln, seg, row_start, tm):
    n, cw = y.shape
    rb0 = row_start // tm
    row = pl.BlockSpec((tm, cw), lambda i: (i, 0))
    off = pl.BlockSpec((tm, cw), lambda i: (rb0 + i, 0))
    full = lambda a: pl.BlockSpec(a.shape, lambda i: (0, 0))
    return pl.pallas_call(
        _rwkv_post_kernel,
        out_shape=jax.ShapeDtypeStruct((n, cw), F32),
        grid=(n // tm,),
        in_specs=[row, off, off, full(ln), full(seg)],
        out_specs=row,
        compiler_params=_cparams("parallel"),
        name="rwkv_post",
    )(y, bonus, g, ln, seg)


MLA_HEADS = 8
MLA_NOPE = 64
MLA_ROPE = 32
MLA_SCALE = (MLA_NOPE + MLA_ROPE) ** -0.5


def _rope_tile(x, cos, sin):
    lane = lax.broadcasted_iota(I32, x.shape, 1)
    half = MLA_ROPE // 2
    rot = jnp.where(lane < half, pltpu.roll(x, LANES - half, 1), pltpu.roll(x, half, 1))
    return x * cos + rot * sin


def _mla_prep_kernel(qd_ref, ckv_ref, kr_ref, cos_ref, sin_ref, qn_ref, kvn_ref, wuq_ref, wuk_ref,
                     ql_ref, qr_ref, c_ref, krn_ref):
    cq = _rms(qd_ref[...], qn_ref[...])
    qh = _dot(cq.astype(MXU_DTYPE), wuq_ref[...])
    nope = MLA_HEADS * MLA_NOPE
    ql_ref[...] = _dot(qh[:, 0:nope].astype(MXU_DTYPE), wuk_ref[...]).astype(ql_ref.dtype)
    cos, sin = cos_ref[...], sin_ref[...]
    for h in range(MLA_HEADS):
        lanes = slice(nope + h * LANES, nope + (h + 1) * LANES)
        qr_ref[:, h * LANES:(h + 1) * LANES] = _rope_tile(qh[:, lanes], cos, sin).astype(qr_ref.dtype)
    c_ref[...] = _rms(ckv_ref[...], kvn_ref[...])
    krn_ref[...] = _rope_tile(kr_ref[...], cos, sin)


def _mla_prep(proj, col_blocks, row_start, nrows, cos, sin, q_norm, kv_norm, wuq, wuk, tm):
    qd0, ckv0, kr0 = col_blocks
    d_q, d_kv = q_norm.shape[1], kv_norm.shape[1]
    rb0 = row_start // tm
    nper = cos.shape[0] // tm
    full = lambda a: pl.BlockSpec(a.shape, lambda i: (0, 0))
    rows = lambda w: pl.BlockSpec((tm, w), lambda i: (i, 0))
    tab = pl.BlockSpec((tm, LANES), lambda i: (i % nper, 0))
    return pl.pallas_call(
        _mla_prep_kernel,
        out_shape=(jax.ShapeDtypeStruct((nrows, MLA_HEADS * d_kv), MXU_DTYPE),
                   jax.ShapeDtypeStruct((nrows, MLA_HEADS * LANES), MXU_DTYPE),
                   jax.ShapeDtypeStruct((nrows, d_kv), F32),
                   jax.ShapeDtypeStruct((nrows, LANES), F32)),
        grid=(nrows // tm,),
        in_specs=[pl.BlockSpec((tm, d_q), lambda i: (rb0 + i, qd0 * LANES // d_q)),
                  pl.BlockSpec((tm, d_kv), lambda i: (rb0 + i, ckv0 * LANES // d_kv)),
                  pl.BlockSpec((tm, LANES), lambda i: (rb0 + i, kr0)),
                  tab, tab, full(q_norm), full(kv_norm), full(wuq), full(wuk)],
        out_specs=(rows(MLA_HEADS * d_kv), rows(MLA_HEADS * LANES), rows(d_kv), rows(LANES)),
        compiler_params=_cparams("parallel"),
        name="mla_prep",
    )(proj, proj, proj, cos, sin, q_norm, kv_norm, wuq, wuk)


def _mla_prompt_kernel(ql_ref, qr_ref, c_ref, kr_ref, wuv_ref, o_ref):
    tq = ql_ref.shape[0]
    t, d_kv = c_ref.shape
    qpos = pl.program_id(1) * tq + lax.broadcasted_iota(I32, (tq, t), 0)
    causal = lax.broadcasted_iota(I32, (tq, t), 1) <= qpos
    c, kr = c_ref[...], kr_ref[...]
    outs = []
    for h in range(MLA_HEADS):
        s = _dot_nt(ql_ref[:, h * d_kv:(h + 1) * d_kv], c) + _dot_nt(qr_ref[:, h * LANES:(h + 1) * LANES], kr)
        outs.append(_masked_softmax_pv(s * MLA_SCALE, causal, c).astype(MXU_DTYPE))
    o_ref[...] = _dot(jnp.concatenate(outs, axis=1), wuv_ref[...])


def _mla_prompt(q_lat, q_rope, c, kr, wuv, batch, t, tq=128):
    d_kv = c.shape[1]
    nq = t // tq
    return pl.pallas_call(
        _mla_prompt_kernel,
        out_shape=jax.ShapeDtypeStruct((batch * t, wuv.shape[1]), F32),
        grid=(batch, nq),
        in_specs=[pl.BlockSpec((tq, MLA_HEADS * d_kv), lambda b, i: (b * nq + i, 0)),
                  pl.BlockSpec((tq, MLA_HEADS * LANES), lambda b, i: (b * nq + i, 0)),
                  pl.BlockSpec((t, d_kv), lambda b, i: (b, 0)),
                  pl.BlockSpec((t, LANES), lambda b, i: (b, 0)),
                  pl.BlockSpec(wuv.shape, lambda b, i: (0, 0))],
        out_specs=pl.BlockSpec((tq, wuv.shape[1]), lambda b, i: (b * nq + i, 0)),
        compiler_params=_cparams("parallel", "arbitrary"),
        name="mla_prompt",
    )(q_lat, q_rope, c, kr, wuv)


def _mla_sample_kernel(pt_ref, ql_ref, qr_ref, cn_ref, krn_ref, *rest, pages, t_new):
    c_pages, kr_pages = rest[:pages], rest[pages:2 * pages]
    o_ref, m_ref, l_ref, acc_ref = rest[2 * pages:]
    step = pl.program_id(1)

    @pl.when(step == 0)
    def _():
        m_ref[...] = jnp.full_like(m_ref, ONLINE_SOFTMAX_FLOOR)
        l_ref[...] = jnp.zeros_like(l_ref)
        acc_ref[...] = jnp.zeros_like(acc_ref)

    ql, qr = ql_ref[...], qr_ref[...]
    cc = jnp.concatenate([r[...] for r in c_pages], axis=0).astype(MXU_DTYPE)
    kc = jnp.concatenate([r[...] for r in kr_pages], axis=0).astype(MXU_DTYPE)
    s = (_dot_nt(ql, cc) + _dot_nt(qr[:, 0:MLA_ROPE], kc)) * MLA_SCALE
    _online_softmax_step(m_ref, l_ref, acc_ref, 0, s, jnp.full(s.shape, True), cc)

    @pl.when(step == pl.num_programs(1) - 1)
    def _():
        cn = cn_ref[...]
        s_new = (_dot_nt(ql, cn) + _dot_nt(qr, krn_ref[...])) * MLA_SCALE
        qpos = lax.broadcasted_iota(I32, s_new.shape, 0) % t_new
        kpos = lax.broadcasted_iota(I32, s_new.shape, 1)
        _online_softmax_step(m_ref, l_ref, acc_ref, 0, s_new, (kpos <= qpos) & (kpos < t_new), cn)
        o_ref[...] = acc_ref[0] / l_ref[0]


def _mla_sample(page_table, q_lat, q_rope, c_new_pad, kr_new_pad, pool_c, pool_kr, layer, t_new):
    b, n_pages = page_table.shape
    pages = math.gcd(n_pages, PAGES_PER_STEP)
    _, rows, d_kv = q_lat.shape
    one = lambda a: pl.BlockSpec((None,) + a.shape[1:], lambda n, c, pt: (n, 0, 0))
    grid_spec = pltpu.PrefetchScalarGridSpec(
        num_scalar_prefetch=1,
        grid=(b, n_pages // pages),
        in_specs=[one(q_lat), one(q_rope), one(c_new_pad), one(kr_new_pad)]
        + _page_specs((None, None, PAGE_SIZE, d_kv), layer, pages)
        + _page_specs((None, None, PAGE_SIZE, MLA_ROPE), layer, pages),
        out_specs=pl.BlockSpec((None, rows, d_kv), lambda n, c, pt: (n, 0, 0)),
        scratch_shapes=[pltpu.VMEM((1, rows, 1), F32), pltpu.VMEM((1, rows, 1), F32),
                        pltpu.VMEM((1, rows, d_kv), F32)],
    )
    return pl.pallas_call(
        functools.partial(_mla_sample_kernel, pages=pages, t_new=t_new),
        out_shape=jax.ShapeDtypeStruct((b, rows, d_kv), F32),
        grid_spec=grid_spec,
        compiler_params=_cparams("parallel", "arbitrary"),
        name="mla_sample",
    )(page_table, q_lat, q_rope, c_new_pad, kr_new_pad, *([pool_c] * pages), *([pool_kr] * pages))


ROW_BLOCK = 256
A_HEADS = 4
B_HEADS, B_KV_HEADS, B_DH = 8, 4, 64
IDX_HEADS, IDX_DIM = 8, 64
C_WIDTH = 512
RWKV_IN = 3 * C_WIDTH + 64 + 64 + 128
D_Q_RANK, D_KV_RANK = 384, 256
ODD_COLS = RWKV_IN + D_KV_RANK + 2 * LANES + D_Q_RANK
ODD_SLABS = ((RWKV_IN + D_KV_RANK + 2 * LANES) // LANES, RWKV_IN // LANES, (RWKV_IN + D_KV_RANK) // LANES)


def _pad_rows(a, rows=LANES):
    return jnp.pad(a, ((0, 0),) * (a.ndim - 2) + ((0, rows - a.shape[-2]), (0, 0)))


def _head_major(a, b, t, h):
    return jnp.transpose(a.reshape(b, t, h, -1), (0, 2, 1, 3))


def _token_major(a):
    b, h, t, w = a.shape
    return jnp.transpose(a, (0, 2, 1, 3)).reshape(b * t, h * w)


def _rope_tables(pos):
    half = MLA_ROPE // 2
    inv = ROPE_THETA ** (-jnp.arange(0, MLA_ROPE, 2, dtype=F32) / MLA_ROPE)
    ang = pos.astype(F32)[:, None] * inv[None, :]
    zeros = jnp.zeros((pos.shape[0], LANES - 2 * half), F32)
    cos, sin = jnp.cos(ang), jnp.sin(ang)
    return jnp.concatenate([cos, cos, zeros], axis=1), jnp.concatenate([-sin, sin, zeros], axis=1)


def kernel(x_prompt, x_sample, cache_dsa_k, cache_dsa_v, cache_dsa_idx, cache_mla_ckv, cache_mla_krope, state_hgrn, state_rwkv, state_shift, page_table, norm_mix, norm_ffn, norm_final, w_in_even, w_out_even, hgrn_lb, hgrn_norm, w_in_odd, w_out_odd, rwkv_mu, rwkv_w0, rwkv_w_up, rwkv_a0, rwkv_a_up, rwkv_g_up, rwkv_k_k, rwkv_k_a, rwkv_r_k, rwkv_ln_w, rwkv_ln_b, mla_q_norm, mla_w_uq, mla_kv_norm, mla_w_uk, mla_w_uv, peer_wq, peer_subkeys, peer_u, peer_v):
    bp, tp, d_model = x_prompt.shape
    bs, ts, _ = x_sample.shape
    n_p, n_s = bp * tp, bs * ts
    n_all = n_p + n_s
    n_pad = _round_up(n_all, PEER_TOKEN_BLOCK)
    tm_s = min(ROW_BLOCK, n_s)
    n_past = page_table.shape[1] * PAGE_SIZE
    depth = norm_mix.shape[0]
    md = MXU_DTYPE

    def all_rows(p, s):
        return jnp.concatenate([p, s, jnp.zeros((n_pad - n_all, p.shape[1]), p.dtype)], axis=0)

    x = all_rows(x_prompt.reshape(n_p, d_model), x_sample.reshape(n_s, d_model))

    lb_cum = jnp.cumsum(jax.nn.softmax(hgrn_lb.astype(F32), axis=0), axis=0)
    lower_bounds = lb_cum - lb_cum[:1]
    seg = jnp.kron(jnp.eye(C_WIDTH // RWKV_HEAD, dtype=F32), jnp.ones((RWKV_HEAD, RWKV_HEAD), F32))
    eye_h = jnp.eye(MLA_HEADS, dtype=F32)
    cos_p, sin_p = _rope_tables(jnp.arange(tp))
    cos_s, sin_s = (jnp.tile(a, (tm_s // ts, 1)) for a in _rope_tables(n_past + jnp.arange(ts)))

    outs = {k: [] for k in ("pk", "pv", "pi", "pc", "pr", "ph", "ps", "psh", "sk", "sv", "si", "sc", "sr", "sh", "ss", "ssh")}

    for l in range(depth):
        j = l // 2
        if l % 2 == 0:
            w_in = jnp.pad(w_in_even[j], ((0, 0), (0, _round_up(w_in_even.shape[2], LANES) - w_in_even.shape[2]))).astype(md)
            proj = _matmul(x, w_in, g=norm_mix[l], tm=ROW_BLOCK)
            aw = A_HEADS * LANES
            c_q, c_k, c_v = 4 * aw, 4 * aw + B_HEADS * B_DH, 4 * aw + (B_HEADS + B_KV_HEADS) * B_DH
            c_qi = c_v + B_KV_HEADS * B_DH
            c_ki = c_qi + IDX_HEADS * IDX_DIM
            c_wi = c_ki + IDX_DIM
            mixes = []
            for rows, b, t, s0_t, kk, kv, ki_key, kh in ((slice(0, n_p), bp, tp, None, "pk", "pv", "pi", "ph"),
                                                    (slice(n_p, n_all), bs, ts, state_hgrn[j], "sk", "sv", "si", "sh")):
                sample = s0_t is not None
                s0_t = jnp.swapaxes(s0_t, -1, -2).astype(F32) if sample else jnp.zeros((b, A_HEADS, LANES, LANES), F32)
                oa, s_a = _hgrn(proj, rows.start, b, t, A_HEADS, lower_bounds[j], hgrn_norm[j], s0_t)
                pr = proj[rows]
                qb, kb, vb = pr[:, c_q:c_k], pr[:, c_k:c_v], pr[:, c_v:c_qi]
                qi, ki, wi = pr[:, c_qi:c_ki], pr[:, c_ki:c_wi], pr[:, c_wi:c_wi + IDX_HEADS]
                qi4 = _head_major(qi, b, t, IDX_HEADS).astype(md)
                q4 = _head_major(qb, b, t, B_HEADS).astype(md)
                k4 = _head_major(kb, b, t, B_KV_HEADS).astype(md)
                v4 = _head_major(vb, b, t, B_KV_HEADS).astype(md)
                wi3 = wi.reshape(b, t, IDX_HEADS)
                ki3 = ki.reshape(b, t, IDX_DIM).astype(md)
                if sample:
                    topk = min(DSA_TOPK_MAX, (n_past + t) // 4)
                    mask = _dsa_select(page_table, qi4, wi3, _pad_rows(ki3), cache_dsa_idx, j, topk, t)
                    group = B_HEADS // B_KV_HEADS
                    o4 = _dsa_attend(page_table, q4.reshape(b, B_KV_HEADS, group * t, B_DH), mask,
                                     _pad_rows(k4), _pad_rows(v4), cache_dsa_k, cache_dsa_v, j)
                    o4 = o4.reshape(b, B_HEADS, t, B_DH)
                else:
                    o4 = _dsa_prompt(qi4, wi3, ki3, q4, k4, v4)
                mixes.append(jnp.concatenate([oa, _token_major(o4)], axis=1))
                outs[kk].append(kb.reshape(b, t, B_KV_HEADS, B_DH))
                outs[kv].append(vb.reshape(b, t, B_KV_HEADS, B_DH))
                outs[ki_key].append(ki.reshape(b, t, IDX_DIM))
                outs[kh].append(jnp.swapaxes(s_a, -1, -2))
            x = _matmul(all_rows(*mixes), w_out_even[j].astype(md), res=x, tm=ROW_BLOCK)
        else:
            w = w_in_odd[j]
            c_qd, c_ckv = RWKV_IN, RWKV_IN + D_Q_RANK
            c_kr = c_ckv + D_KV_RANK
            w_in = jnp.concatenate([w[:, :RWKV_IN], w[:, c_ckv:c_kr], w[:, c_kr:c_kr + MLA_ROPE],
                                    jnp.zeros((d_model, 2 * LANES - MLA_ROPE), w.dtype), w[:, c_qd:c_ckv]], axis=1).astype(md)
            proj = _matmul(x, w_in, g=norm_mix[l], tm=ROW_BLOCK)
            pc_p = proj[:n_p, :RWKV_IN].reshape(bp, tp, RWKV_IN)
            pc_s = proj[n_p:n_all, :RWKV_IN].reshape(bs, ts, RWKV_IN)
            prev_p = jnp.concatenate([jnp.zeros((bp, 1, RWKV_IN), F32), pc_p[:, :-1]], axis=1)
            prev_s = jnp.concatenate([state_shift[j].astype(F32)[:, None], pc_s[:, :-1]], axis=1)
            prev = all_rows(prev_p.reshape(n_p, RWKV_IN), prev_s.reshape(n_s, RWKV_IN))
            zeros_l = jnp.zeros((rwkv_w_up.shape[1], C_WIDTH), F32)
            lora = jnp.concatenate([jnp.concatenate([rwkv_w_up[j], zeros_l], axis=1),
                                    jnp.concatenate([zeros_l, rwkv_a_up[j]], axis=1)], axis=0).astype(md)
            vecs = jnp.stack([rwkv_w0[j], rwkv_a0[j], rwkv_k_k[j], rwkv_k_a[j], rwkv_r_k[j].reshape(-1)]).astype(F32)
            prep = _rwkv_prep(proj, prev, rwkv_mu[j].reshape(1, -1).astype(F32), vecs, lora, rwkv_g_up[j].astype(md), seg, ROW_BLOCK)
            scan_in, g_all, bonus_all = prep[:6], prep[6], prep[7]
            ln = jnp.stack([rwkv_ln_w[j], rwkv_ln_b[j]]).astype(F32)

            wuq = mla_w_uq[j].reshape(D_Q_RANK, MLA_HEADS, MLA_NOPE + MLA_ROPE)
            wuq_rope = jnp.pad(wuq[:, :, MLA_NOPE:], ((0, 0), (0, 0), (0, LANES - MLA_ROPE))).reshape(D_Q_RANK, -1)
            wuq_p = jnp.concatenate([wuq[:, :, :MLA_NOPE].reshape(D_Q_RANK, -1), wuq_rope], axis=1).astype(md)
            wuk = jnp.einsum("chn,hg->hngc", mla_w_uk[j], eye_h).reshape(MLA_HEADS * MLA_NOPE, -1).astype(md)
            wuv = jnp.einsum("chv,hg->hcgv", mla_w_uv[j], eye_h).reshape(MLA_HEADS * D_KV_RANK, -1).astype(md)
            q_norm = mla_q_norm[j].reshape(1, -1).astype(F32)
            kv_norm = mla_kv_norm[j].reshape(1, -1).astype(F32)

            mixes = []
            for start, b, t, tm, cos, sin, sample in ((0, bp, tp, ROW_BLOCK, cos_p, sin_p, False),
                                                      (n_p, bs, ts, tm_s, cos_s, sin_s, True)):
                n = b * t
                if sample:
                    heads = C_WIDTH // RWKV_HEAD
                    s0 = state_rwkv[j].astype(F32).reshape(b, heads // 2, 2, RWKV_HEAD, RWKV_HEAD)
                    s0 = jnp.transpose(s0, (0, 1, 3, 2, 4)).reshape(b, heads // 2, RWKV_HEAD, 2 * RWKV_HEAD)
                else:
                    s0 = jnp.zeros((b, C_WIDTH // LANES, RWKV_HEAD, LANES), F32)
                y3, s_c = _rwkv_scan(scan_in, start, b, t, s0)
                yc = _rwkv_post(y3.reshape(n, C_WIDTH), bonus_all, g_all, ln, seg, start, tm)
                s_c = jnp.transpose(s_c.reshape(b, -1, RWKV_HEAD, 2, RWKV_HEAD), (0, 1, 3, 2, 4))
                s_c = s_c.reshape(b, -1, RWKV_HEAD, RWKV_HEAD)

                q_lat, q_rope, c_new, kr_new = _mla_prep(proj, ODD_SLABS, start, n, cos, sin, q_norm, kv_norm, wuq_p, wuk, tm)
                if sample:
                    hq = lambda a: _head_major(a, b, t, MLA_HEADS).reshape(b, MLA_HEADS * t, -1)
                    o_lat = _mla_sample(page_table, hq(q_lat), hq(q_rope),
                                        _pad_rows(c_new.reshape(b, t, -1)).astype(md),
                                        _pad_rows(kr_new.reshape(b, t, -1)).astype(md),
                                        cache_mla_ckv, cache_mla_krope, j, t)
                    o_lat = _token_major(o_lat.reshape(b, MLA_HEADS, t, -1))
                    od = _matmul(o_lat, wuv, tm=tm)
                else:
                    od = _mla_prompt(q_lat, q_rope, c_new.astype(md), kr_new.astype(md), wuv, b, t)
                mixes.append(jnp.concatenate([yc, od], axis=1))
                pre = "s" if sample else "p"
                outs[pre + "c"].append(c_new.reshape(b, t, D_KV_RANK))
                outs[pre + "r"].append(kr_new[:, :MLA_ROPE].reshape(b, t, MLA_ROPE))
                outs[pre + "s"].append(s_c)
                outs[pre + "sh"].append((pc_s if sample else pc_p)[:, -1])
            x = _matmul(all_rows(*mixes), w_out_odd[j].astype(md), res=x, tm=ROW_BLOCK)
        x = _peer_ffn(x, norm_ffn[l], _peer_prepare(peer_wq[l], peer_subkeys[l], peer_u[l], peer_v[l]))

    y = _rmsnorm(x, norm_final)
    y_prompt = y[:n_p].reshape(bp, tp, d_model)
    y_sample = y[n_p:n_all].reshape(bs, ts, d_model)
    st = lambda k: jnp.stack(outs[k])
    return (y_prompt, y_sample,
            st("pk"), st("pv"), st("pi"), st("pc"), st("pr"), st("ph"), st("ps"), st("psh"),
            st("sk"), st("sv"), st("si"), st("sc"), st("sr"), st("sh"), st("ss"), st("ssh"))
```

```python
import functools
import math

import jax
import jax.numpy as jnp
from jax import lax
from jax.experimental import pallas as pl
from jax.experimental.pallas import tpu as pltpu

F32 = jnp.float32
I32 = jnp.int32
MXU_DTYPE = jnp.bfloat16
GATE_DTYPE = jnp.bfloat16
LANES = 128
SUBLANES = 8
VMEM_LIMIT_BYTES = 56 * 1024 * 1024
NEG_INF = float("-inf")

RMS_EPS = 1e-6
PAGE_SIZE = 128
ROPE_THETA = 10000.0
PEER_HEADS = 8
PEER_NKEYS = 128
PEER_TOPK = 16
PEER_TOKEN_BLOCK = 512
PEER_GROUP = 8


def _cparams(*sem):
    return pltpu.CompilerParams(dimension_semantics=sem, vmem_limit_bytes=VMEM_LIMIT_BYTES)


def _round_up(n, m):
    return -(-n // m) * m


def _rms(x, g):
    return x * lax.rsqrt(jnp.mean(x * x, axis=-1, keepdims=True) + RMS_EPS) * g


def _dot(a, b):
    return jnp.dot(a, b, preferred_element_type=F32)


def _dot_nt(a, b):
    return lax.dot_general(a, b, (((1,), (1,)), ((), ())), preferred_element_type=F32)


def _matmul_kernel(*refs, norm, residual):
    it = iter(refs)
    a_ref = next(it)
    g_ref = next(it) if norm else None
    w_ref = next(it)
    r_ref = next(it) if residual else None
    o_ref = next(it)
    a = a_ref[...]
    if norm:
        a = _rms(a, g_ref[...])
    acc = _dot(a.astype(w_ref.dtype), w_ref[...])
    if residual:
        acc = acc + r_ref[...]
    o_ref[...] = acc


def _matmul(a, w, g=None, res=None, tm=256):
    n, k = a.shape
    m = w.shape[1]
    ins, specs = [a], [pl.BlockSpec((tm, k), lambda i: (i, 0))]
    if g is not None:
        ins.append(g.reshape(1, k).astype(F32))
        specs.append(pl.BlockSpec((1, k), lambda i: (0, 0)))
    ins.append(w)
    specs.append(pl.BlockSpec((k, m), lambda i: (0, 0)))
    if res is not None:
        ins.append(res)
        specs.append(pl.BlockSpec((tm, m), lambda i: (i, 0)))
    return pl.pallas_call(
        functools.partial(_matmul_kernel, norm=g is not None, residual=res is not None),
        out_shape=jax.ShapeDtypeStruct((n, m), F32),
        grid=(n // tm,),
        in_specs=specs,
        out_specs=pl.BlockSpec((tm, m), lambda i: (i, 0)),
        compiler_params=_cparams("parallel"),
        name="proj_matmul",
    )(*ins)


def _rmsnorm_kernel(x_ref, g_ref, o_ref):
    o_ref[...] = _rms(x_ref[...], g_ref[...])


def _rmsnorm(x, g, tm=512):
    n, d = x.shape
    return pl.pallas_call(
        _rmsnorm_kernel,
        out_shape=jax.ShapeDtypeStruct((n, d), F32),
        grid=(n // tm,),
        in_specs=[pl.BlockSpec((tm, d), lambda i: (i, 0)), pl.BlockSpec((1, d), lambda i: (0, 0))],
        out_specs=pl.BlockSpec((tm, d), lambda i: (i, 0)),
        compiler_params=_cparams("parallel"),
        name="final_rmsnorm",
    )(x, g.reshape(1, d).astype(F32))


def _peer_fold_kernel(sub_ref, wq_ref, o_ref):
    o_ref[...] = _dot_nt(sub_ref[...], wq_ref[...])


def _peer_fold(subkeys, wq):
    d_model = wq.shape[0]
    half = subkeys.shape[2]
    return pl.pallas_call(
        _peer_fold_kernel,
        out_shape=jax.ShapeDtypeStruct((2, PEER_HEADS, PEER_NKEYS, d_model), F32),
        grid=(2, PEER_HEADS),
        in_specs=[pl.BlockSpec((None, PEER_NKEYS, half), lambda p, h: (p, 0, 0)),
                  pl.BlockSpec((d_model, half), lambda p, h: (0, h * 2 + p))],
        out_specs=pl.BlockSpec((None, None, PEER_NKEYS, d_model), lambda p, h: (p, h, 0, 0)),
        compiler_params=_cparams("parallel", "parallel"),
        name="peer_fold",
    )(subkeys.astype(MXU_DTYPE), wq.astype(MXU_DTYPE))


def _peer_scores_kernel(x_ref, g_ref, m_ref, s_ref, h_ref):
    hb = _rms(x_ref[...], g_ref[...]).astype(MXU_DTYPE)
    s_ref[...] = _dot_nt(m_ref[...], hb)
    h_ref[...] = hb.T


def _peer_scores(x, g, mcat, tm=PEER_TOKEN_BLOCK):
    n, d = x.shape
    rows = mcat.shape[0]
    return pl.pallas_call(
        _peer_scores_kernel,
        out_shape=(jax.ShapeDtypeStruct((rows, n), F32), jax.ShapeDtypeStruct((d, n), MXU_DTYPE)),
        grid=(n // tm,),
        in_specs=[pl.BlockSpec((tm, d), lambda i: (i, 0)),
                  pl.BlockSpec((1, d), lambda i: (0, 0)),
                  pl.BlockSpec((rows, d), lambda i: (0, 0))],
        out_specs=(pl.BlockSpec((rows, tm), lambda i: (0, i)), pl.BlockSpec((d, tm), lambda i: (0, i))),
        compiler_params=_cparams("parallel"),
        name="peer_scores",
    )(x, g.reshape(1, d).astype(F32), mcat)


def _peer_topk_kernel(s_ref, c1_ref, e1_ref, r2_ref, e2_ref):
    hk = PEER_HEADS * PEER_NKEYS
    tn = s_ref.shape[1]

    def top_values(base):
        vals, prev = [], None
        for _ in range(PEER_TOPK):
            def body(i, m, prev=prev):
                x = s_ref[pl.ds(pl.multiple_of(base + i * PEER_HEADS, PEER_HEADS), PEER_HEADS), :]
                if prev is not None:
                    x = jnp.where(x < prev, x, NEG_INF)
                return jnp.maximum(m, x)
            prev = lax.fori_loop(0, PEER_NKEYS, body, jnp.full((PEER_HEADS, tn), NEG_INF, F32), unroll=8)
            vals.append(prev)
        return vals

    a = top_values(0)
    b = top_values(hk)
    cands = [a[k] + b[l] for k in range(PEER_TOPK) for l in range(PEER_TOPK) if (k + 1) * (l + 1) <= PEER_TOPK]
    top = a[0] + b[0]
    tau = top
    for _ in range(PEER_TOPK - 1):
        nxt = jnp.full_like(tau, NEG_INF)
        for c in cands:
            nxt = jnp.maximum(nxt, jnp.where(c < tau, c, NEG_INF))
        tau = nxt
    z = jnp.zeros_like(tau)
    for c in cands:
        z = z + jnp.where(c >= tau, jnp.exp(c - top), 0.0)
    inv_z = 1.0 / z

    twice = lambda x: jnp.concatenate([x, x], axis=0)
    tau2, a0_2, inv_z2, b2 = twice(tau), twice(a[0]), twice(inv_z), [twice(x) for x in b]

    def first_half(i, carry):
        rows = pl.ds(pl.multiple_of(i * 2 * PEER_HEADS, 2 * PEER_HEADS), 2 * PEER_HEADS)
        x = s_ref[rows, :]
        cnt = jnp.zeros_like(x)
        for bl in b2:
            cnt = cnt + jnp.where(x + bl >= tau2, 1.0, 0.0)
        c1_ref[rows, :] = cnt.astype(c1_ref.dtype)
        e1_ref[rows, :] = (jnp.exp(x - a0_2) * inv_z2).astype(e1_ref.dtype)
        return carry

    lax.fori_loop(0, PEER_NKEYS // 2, first_half, 0)
    for h in range(PEER_HEADS):
        x = s_ref[pl.ds(2 * hk + h * PEER_NKEYS, PEER_NKEYS), :]
        rank = jnp.zeros_like(x)
        for bl in b:
            rank = rank + jnp.where(bl[h:h + 1, :] > x, 1.0, 0.0)
        rows = slice(h * PEER_NKEYS, (h + 1) * PEER_NKEYS)
        r2_ref[rows, :] = rank.astype(r2_ref.dtype)
        e2_ref[rows, :] = jnp.exp(x - b[0][h:h + 1, :]).astype(e2_ref.dtype)


def _peer_topk(s_t, tn=PEER_TOKEN_BLOCK):
    rows, n = s_t.shape
    hk = PEER_HEADS * PEER_NKEYS
    out = jax.ShapeDtypeStruct((hk, n), GATE_DTYPE)
    spec = pl.BlockSpec((hk, tn), lambda i: (0, i))
    return pl.pallas_call(
        _peer_topk_kernel,
        out_shape=(out,) * 4,
        grid=(n // tn,),
        in_specs=[pl.BlockSpec((rows, tn), lambda i: (0, i))],
        out_specs=(spec,) * 4,
        compiler_params=_cparams("parallel"),
        name="peer_topk",
    )(s_t)


def _gelu(x):
    return 0.5 * x * (1.0 + lax.erf(x * (1.0 / math.sqrt(2.0))))


PEER_QUAD = 4 * PEER_NKEYS
GATE_TILE = (64, 256)


def _peer_experts_kernel(h_ref, c1_ref, e1_ref, r2_ref, e2_ref, u_ref, v_ref, o_ref, a_ref):
    @pl.when(pl.program_id(1) == 0)
    def _():
        o_ref[...] = jnp.zeros_like(o_ref)

    hb = h_ref[...]
    tn = hb.shape[1]
    te, tt = GATE_TILE
    for quad in range(u_ref.shape[0] // PEER_QUAD):
        st = _dot(u_ref[quad * PEER_QUAD:(quad + 1) * PEER_QUAD, :], hb)
        for sub in range(PEER_QUAD // PEER_NKEYS):
            i1 = quad * (PEER_QUAD // PEER_NKEYS) + sub
            c1 = c1_ref[i1 * PEER_HEADS:(i1 + 1) * PEER_HEADS, :]
            e1 = e1_ref[i1 * PEER_HEADS:(i1 + 1) * PEER_HEADS, :]
            for r0 in range(0, PEER_NKEYS, te):
                for l0 in range(0, tn, tt):
                    lanes = slice(l0, l0 + tt)
                    gate = None
                    for h in range(PEER_HEADS):
                        rows = slice(h * PEER_NKEYS + r0, h * PEER_NKEYS + r0 + te)
                        sel = r2_ref[rows, lanes] < c1[h:h + 1, lanes]
                        term = jnp.where(sel, e2_ref[rows, lanes], 0.0) * e1[h:h + 1, lanes]
                        gate = term if gate is None else gate + term
                    rows = slice(sub * PEER_NKEYS + r0, sub * PEER_NKEYS + r0 + te)
                    a_ref[rows, lanes] = (_gelu(st[rows, lanes]).astype(gate.dtype) * gate).astype(a_ref.dtype)
        o_ref[...] += _dot(v_ref[quad], a_ref[...])


def _peer_experts(h_t, c1, e1, r2, e2, u2, v3, tn=PEER_TOKEN_BLOCK):
    d, n = h_t.shape
    hk = PEER_HEADS * PEER_NKEYS
    ge = PEER_GROUP * PEER_NKEYS
    tok = lambda rows: pl.BlockSpec((rows, tn), lambda i, g: (0, i))
    grp = pl.BlockSpec((PEER_GROUP * PEER_HEADS, tn), lambda i, g: (g, i))
    return pl.pallas_call(
        _peer_experts_kernel,
        out_shape=jax.ShapeDtypeStruct((d, n), F32),
        grid=(n // tn, u2.shape[0] // ge),
        in_specs=[tok(d), grp, grp, tok(hk), tok(hk),
                  pl.BlockSpec((ge, d), lambda i, g: (g, 0)),
                  pl.BlockSpec((ge // PEER_QUAD, d, PEER_QUAD), lambda i, g: (g, 0, 0))],
        out_specs=tok(d),
        scratch_shapes=[pltpu.VMEM((PEER_QUAD, tn), MXU_DTYPE)],
        compiler_params=_cparams("parallel", "arbitrary"),
        name="peer_experts",
    )(h_t, c1, e1, r2, e2, u2, v3)


def _peer_prepare(wq, subkeys, u, v):
    d = wq.shape[0]
    hk = PEER_HEADS * PEER_NKEYS
    mf = _peer_fold(subkeys, wq)
    inter = jnp.transpose(mf, (0, 2, 1, 3)).reshape(2 * hk, d)
    mcat = jnp.concatenate([inter, mf[1].reshape(hk, d)], axis=0).astype(MXU_DTYPE)
    v3 = jnp.transpose(v.astype(MXU_DTYPE).reshape(-1, PEER_QUAD, d), (0, 2, 1))
    return mcat, u.astype(MXU_DTYPE), v3


def _peer_ffn(x, g, prep):
    mcat, u2, v3 = prep
    s_t, h_t = _peer_scores(x, g, mcat)
    y_t = _peer_experts(h_t, *_peer_topk(s_t), u2, v3)
    return x + y_t.T


HGRN_CHUNK = 64
HGRN_SUB = 16
MASKED_EXPONENT = -1e30


def _cumsum_rows(x):
    rows = x.shape[0]
    row = lax.broadcasted_iota(I32, x.shape, 0)
    d = 1
    while d < rows:
        x = x + jnp.where(row >= d, pltpu.roll(x, d, 0), 0.0)
        d *= 2
    return x


def _hgrn_chunk(q, k, v, g, s_t, sub):
    c = q.shape[0]
    cum = _cumsum_rows(g)
    o = _dot_nt((q * jnp.exp(cum)).astype(MXU_DTYPE), s_t.astype(MXU_DTYPE))
    outs = []
    for blk in range(c // sub):
        r0 = blk * sub
        q_b, cum_b, k_b, v_b = q[r0:r0 + sub], cum[r0:r0 + sub], k[r0:r0 + sub], v[r0:r0 + sub]
        o_b = o[r0:r0 + sub]
        if blk > 0:
            base = cum[r0 - 1:r0]
            qs = q_b * jnp.exp(cum_b - base)
            ks = k[0:r0] * jnp.exp(base - cum[0:r0])
            att = _dot_nt(qs.astype(MXU_DTYPE), ks.astype(MXU_DTYPE))
            o_b = o_b + _dot(att.astype(MXU_DTYPE), v[0:r0].astype(MXU_DTYPE))
        row = lax.broadcasted_iota(I32, (sub, q.shape[1]), 0)
        for s in range(sub):
            dec = jnp.exp(jnp.where(row >= s, cum_b - cum_b[s:s + 1], MASKED_EXPONENT))
            att = jnp.sum(q_b * k_b[s:s + 1] * dec, axis=1, keepdims=True)
            o_b = o_b + att * v_b[s:s + 1]
        outs.append(o_b)
    o = outs[0] if len(outs) == 1 else jnp.concatenate(outs, axis=0)
    last = cum[c - 1:c]
    kd = k * jnp.exp(last - cum)
    upd = lax.dot_general(v.astype(MXU_DTYPE), kd.astype(MXU_DTYPE), (((0,), (0,)), ((), ())),
                          preferred_element_type=F32)
    return o, s_t * jnp.exp(last) + upd


def _hgrn_kernel(q_ref, f_ref, i_ref, g_ref, lb_ref, gain_ref, s0_ref, o_ref, s_ref, *, chunk, sub):
    @pl.when(pl.program_id(2) == 0)
    def _():
        s_ref[...] = s0_ref[...]

    lb = lb_ref[...]
    s_t = s_ref[...]
    for c0 in range(0, q_ref.shape[0], chunk):
        rows = slice(c0, c0 + chunk)
        f = lb + (1.0 - lb) * jax.nn.sigmoid(f_ref[rows, :])
        o, s_t = _hgrn_chunk(jax.nn.silu(q_ref[rows, :]), 1.0 - f, i_ref[rows, :], jnp.log(f), s_t, sub)
        o_ref[rows, :] = _rms(o, gain_ref[...]) * jax.nn.silu(g_ref[rows, :])
    s_ref[...] = s_t


def _hgrn(proj, row_start, batch, t, heads, lb, gain, s0_t):
    chunk = math.gcd(t, HGRN_CHUNK)
    sub = min(HGRN_SUB, chunk)
    tc = min(t, 4 * chunk)
    nt = t // tc
    rb0 = row_start // tc
    dk = LANES

    def slab(k):
        return pl.BlockSpec((tc, dk), lambda b, h, i, k=k: (rb0 + b * nt + i, k * heads + h))

    vec = pl.BlockSpec((1, dk), lambda b, h, i: (0, h))
    st = pl.BlockSpec((None, None, dk, dk), lambda b, h, i: (b, h, 0, 0))
    return pl.pallas_call(
        functools.partial(_hgrn_kernel, chunk=chunk, sub=sub),
        out_shape=(jax.ShapeDtypeStruct((batch * t, heads * dk), F32),
                   jax.ShapeDtypeStruct((batch, heads, dk, dk), F32)),
        grid=(batch, heads, nt),
        in_specs=[slab(0), slab(1), slab(2), slab(3), vec, vec, st],
        out_specs=(pl.BlockSpec((tc, dk), lambda b, h, i: (b * nt + i, h)), st),
        compiler_params=_cparams("parallel", "parallel", "arbitrary"),
        name="hgrn2",
    )(proj, proj, proj, proj, lb.reshape(1, -1), gain.reshape(1, -1), s0_t)


DSA_TOPK_MAX = 256
INT32_MIN = -2 ** 31


def _count(m):
    return jnp.sum(jnp.where(m, 1.0, 0.0), axis=1, keepdims=True)


def _topk_mask(scores, k, idx_bits):
    u = lax.bitcast_convert_type(scores, I32)
    key = u ^ ((u >> 31) & I32(0x7FFFFFFF))
    kf = float(k)
    v = jnp.where(_count(key >= 0) >= kf, I32(0), I32(INT32_MIN))

    def value_bit(it, v):
        t = v | (I32(1) << (I32(30) - it))
        return jnp.where(_count(key >= t) >= kf, t, v)

    v = lax.fori_loop(0, 31, value_bit, v)
    above = key > v
    tied = key == v
    need = kf - _count(above)
    idx = lax.broadcasted_iota(I32, scores.shape, 1)

    def index_bit(it, j):
        t = j | (I32(1) << (I32(idx_bits - 1) - it))
        return jnp.where(_count(tied & (idx < t)) < need, t, j)

    j = lax.fori_loop(0, idx_bits, index_bit, jnp.zeros_like(v))
    return (above | (tied & (idx <= j))) & (scores > NEG_INF)


def _index_scores(qi, w, ki, keys_transposed=False):
    heads, _, d = qi.shape
    acc = None
    for h in range(heads):
        dots = (_dot(qi[h], ki) if keys_transposed else _dot_nt(qi[h], ki)) * (d ** -0.5)
        term = jnp.maximum(dots, 0.0) * (w[:, h:h + 1] * (heads ** -0.5))
        acc = term if acc is None else acc + term
    return acc


def _masked_softmax_pv(s, mask, v):
    s = jnp.where(mask, s, NEG_INF)
    p = jnp.exp(s - jnp.max(s, axis=1, keepdims=True))
    return _dot(p.astype(MXU_DTYPE), v) / jnp.sum(p, axis=1, keepdims=True)


def _dsa_prompt_kernel(qi_ref, w_ref, ki_ref, q_ref, k_ref, v_ref, o_ref, *, topk):
    tq = q_ref.shape[1]
    t = ki_ref.shape[0]
    qpos = pl.program_id(1) * tq + lax.broadcasted_iota(I32, (tq, t), 0)
    kpos = lax.broadcasted_iota(I32, (tq, t), 1)
    scores = jnp.where(kpos <= qpos, _index_scores(qi_ref[...], w_ref[...], ki_ref[...]), NEG_INF)
    mask = _topk_mask(scores, topk, max(1, (t - 1).bit_length()))
    heads, kv_heads, dh = q_ref.shape[0], k_ref.shape[0], q_ref.shape[2]
    for h in range(heads):
        j = h // (heads // kv_heads)
        s = _dot_nt(q_ref[h], k_ref[j]) * (dh ** -0.5)
        o_ref[h] = _masked_softmax_pv(s, mask, v_ref[j])


def _dsa_prompt(qi4, wi, ki, q4, k4, v4, tq=128):
    b, hi, t, d = qi4.shape
    _, h, _, dh = q4.shape
    hkv = k4.shape[1]
    topk = min(DSA_TOPK_MAX, t // 4)
    return pl.pallas_call(
        functools.partial(_dsa_prompt_kernel, topk=topk),
        out_shape=jax.ShapeDtypeStruct((b, h, t, dh), F32),
        grid=(b, t // tq),
        in_specs=[pl.BlockSpec((None, hi, tq, d), lambda n, i: (n, 0, i, 0)),
                  pl.BlockSpec((None, tq, hi), lambda n, i: (n, i, 0)),
                  pl.BlockSpec((None, t, d), lambda n, i: (n, 0, 0)),
                  pl.BlockSpec((None, h, tq, dh), lambda n, i: (n, 0, i, 0)),
                  pl.BlockSpec((None, hkv, t, dh), lambda n, i: (n, 0, 0, 0)),
                  pl.BlockSpec((None, hkv, t, dh), lambda n, i: (n, 0, 0, 0))],
        out_specs=pl.BlockSpec((None, h, tq, dh), lambda n, i: (n, 0, i, 0)),
        compiler_params=_cparams("parallel", "arbitrary"),
        name="dsa_prompt",
    )(qi4, wi, ki, q4, k4, v4)


PAGES_PER_STEP = 8
ONLINE_SOFTMAX_FLOOR = -1e30


def _page_specs(block, layer, pages):
    zeros = (0,) * (len(block) - 2)
    return [pl.BlockSpec(block, lambda b, c, pt, r=r: (layer, pt[b, c * pages + r]) + zeros) for r in range(pages)]


def _dsa_select_kernel(pt_ref, qi_ref, w_ref, kn_ref, *rest, pages, topk, t_new):
    page_refs, (mask_ref, sc_ref) = rest[:pages], rest[pages:]
    c = pl.program_id(1)
    nc = pl.num_programs(1)
    kc = jnp.concatenate([r[...] for r in page_refs], axis=1).astype(MXU_DTYPE)
    sc_ref[c] = _index_scores(qi_ref[...], w_ref[...], kc, keys_transposed=True)

    @pl.when(c == nc - 1)
    def _():
        n_chunks, rows, ch = sc_ref.shape
        new = _index_scores(qi_ref[...], w_ref[...], kn_ref[...])
        qpos = lax.broadcasted_iota(I32, new.shape, 0)
        kpos = lax.broadcasted_iota(I32, new.shape, 1)
        new = jnp.where((kpos <= qpos) & (kpos < t_new), new, NEG_INF)
        pieces = [sc_ref[i] for i in range(n_chunks)] + [new]
        if ch > new.shape[1]:
            pieces.append(jnp.full((rows, ch - new.shape[1]), NEG_INF, F32))
        scores = jnp.concatenate(pieces, axis=1)
        mask = _topk_mask(scores, topk, scores.shape[1].bit_length())
        for i in range(n_chunks + 1):
            mask_ref[i] = jnp.where(mask[:, i * ch:(i + 1) * ch], 1.0, 0.0)


def _dsa_select(page_table, qi4, wi, ki_new_pad, pool_idx, layer, topk, t_new):
    b, n_pages = page_table.shape
    pages = math.gcd(n_pages, PAGES_PER_STEP)
    nc = n_pages // pages
    ch = pages * PAGE_SIZE
    _, hi, t, d = qi4.shape
    grid_spec = pltpu.PrefetchScalarGridSpec(
        num_scalar_prefetch=1,
        grid=(b, nc),
        in_specs=[pl.BlockSpec((None, hi, t, d), lambda n, c, pt: (n, 0, 0, 0)),
                  pl.BlockSpec((None, t, hi), lambda n, c, pt: (n, 0, 0)),
                  pl.BlockSpec((None, LANES, d), lambda n, c, pt: (n, 0, 0))]
        + _page_specs((None, None, d, PAGE_SIZE), layer, pages),
        out_specs=pl.BlockSpec((None, nc + 1, t, ch), lambda n, c, pt: (n, 0, 0, 0)),
        scratch_shapes=[pltpu.VMEM((nc, t, ch), F32)],
    )
    return pl.pallas_call(
        functools.partial(_dsa_select_kernel, pages=pages, topk=topk, t_new=t_new),
        out_shape=jax.ShapeDtypeStruct((b, nc + 1, t, ch), F32),
        grid_spec=grid_spec,
        compiler_params=_cparams("parallel", "arbitrary"),
        name="dsa_sample_select",
    )(page_table, qi4, wi, ki_new_pad, *([pool_idx] * pages))


def _online_softmax_step(m_ref, l_ref, acc_ref, j, s, mask, v, values_transposed=False):
    m_old = m_ref[j]
    m_new = jnp.maximum(m_old, jnp.max(jnp.where(mask, s, ONLINE_SOFTMAX_FLOOR), axis=1, keepdims=True))
    alpha = jnp.exp(m_old - m_new)
    p = jnp.where(mask, jnp.exp(s - m_new), 0.0)
    l_ref[j] = alpha * l_ref[j] + jnp.sum(p, axis=1, keepdims=True)
    pv = _dot_nt(p.astype(MXU_DTYPE), v) if values_transposed else _dot(p.astype(MXU_DTYPE), v)
    acc_ref[j] = alpha * acc_ref[j] + pv
    m_ref[j] = m_new


def _dsa_attend_kernel(pt_ref, q_ref, mc_ref, mn_ref, kn_ref, vn_ref, *rest, pages):
    k_pages, v_pages = rest[:pages], rest[pages:2 * pages]
    o_ref, m_ref, l_ref, acc_ref = rest[2 * pages:]
    c = pl.program_id(1)
    kv_heads, rows, dh = q_ref.shape
    scale = dh ** -0.5

    @pl.when(c == 0)
    def _():
        m_ref[...] = jnp.full_like(m_ref, ONLINE_SOFTMAX_FLOOR)
        l_ref[...] = jnp.zeros_like(l_ref)
        acc_ref[...] = jnp.zeros_like(acc_ref)

    reps = rows // mc_ref.shape[0]
    mask = jnp.concatenate([mc_ref[...]] * reps, axis=0) > 0.0
    for j in range(kv_heads):
        kc = jnp.concatenate([r[j] for r in k_pages], axis=1).astype(MXU_DTYPE)
        vc = jnp.concatenate([r[j] for r in v_pages], axis=1).astype(MXU_DTYPE)
        _online_softmax_step(m_ref, l_ref, acc_ref, j, _dot(q_ref[j], kc) * scale, mask, vc, values_transposed=True)

    @pl.when(c == pl.num_programs(1) - 1)
    def _():
        mask_new = jnp.concatenate([mn_ref[:, 0:LANES]] * reps, axis=0) > 0.0
        for j in range(kv_heads):
            _online_softmax_step(m_ref, l_ref, acc_ref, j, _dot_nt(q_ref[j], kn_ref[j]) * scale, mask_new, vn_ref[j])
            o_ref[j] = acc_ref[j] / l_ref[j]


def _dsa_attend(page_table, q4, mask, k_new_pad, v_new_pad, pool_k, pool_v, layer):
    b, n_pages = page_table.shape
    pages = math.gcd(n_pages, PAGES_PER_STEP)
    nc = n_pages // pages
    _, hkv, rows, dh = q4.shape
    _, _, t, ch = mask.shape
    grid_spec = pltpu.PrefetchScalarGridSpec(
        num_scalar_prefetch=1,
        grid=(b, nc),
        in_specs=[pl.BlockSpec((None, hkv, rows, dh), lambda n, c, pt: (n, 0, 0, 0)),
                  pl.BlockSpec((None, None, t, ch), lambda n, c, pt: (n, c, 0, 0)),
                  pl.BlockSpec((None, None, t, ch), lambda n, c, pt: (n, nc, 0, 0)),
                  pl.BlockSpec((None, hkv, LANES, dh), lambda n, c, pt: (n, 0, 0, 0)),
                  pl.BlockSpec((None, hkv, LANES, dh), lambda n, c, pt: (n, 0, 0, 0))]
        + _page_specs((None, None, hkv, dh, PAGE_SIZE), layer, pages)
        + _page_specs((None, None, hkv, dh, PAGE_SIZE), layer, pages),
        out_specs=pl.BlockSpec((None, hkv, rows, dh), lambda n, c, pt: (n, 0, 0, 0)),
        scratch_shapes=[pltpu.VMEM((hkv, rows, 1), F32), pltpu.VMEM((hkv, rows, 1), F32),
                        pltpu.VMEM((hkv, rows, dh), F32)],
    )
    return pl.pallas_call(
        functools.partial(_dsa_attend_kernel, pages=pages),
        out_shape=jax.ShapeDtypeStruct((b, hkv, rows, dh), F32),
        grid_spec=grid_spec,
        compiler_params=_cparams("parallel", "arbitrary"),
        name="dsa_sample_attend",
    )(page_table, q4, mask, mask, k_new_pad, v_new_pad, *([pool_k] * pages), *([pool_v] * pages))


GN_EPS = 64e-5
RWKV_HEAD = 64
RWKV_BATCH_BLOCK = 4


def _dot_f32(a, b):
    return jnp.dot(a, b, preferred_element_type=F32, precision=lax.Precision.HIGHEST)


def _rwkv_prep_kernel(pc_ref, prev_ref, mu_ref, vec_ref, lora_ref, gup_ref, seg_ref,
                      r_ref, w_ref, k_ref, v_ref, kk_ref, b_ref, g_ref, bonus_ref):
    cw = r_ref.shape[1]
    pc = pc_ref[...]
    xm = pc + (prev_ref[...] - pc) * mu_ref[...]
    r, kc, vc = xm[:, 0:cw], xm[:, cw:2 * cw], xm[:, 2 * cw:3 * cw]
    wa = xm[:, 3 * cw:3 * cw + LANES]
    gd = xm[:, 3 * cw + LANES:]
    lane = lax.broadcasted_iota(I32, wa.shape, 1)
    wa = jnp.where(lane < LANES // 2, jnp.tanh(wa), wa)
    lo = _dot(wa.astype(MXU_DTYPE), lora_ref[...])
    w0, a0, k_k, k_a, r_k = (vec_ref[i:i + 1, :] for i in range(5))
    w_log = -jax.nn.softplus(-(w0 + lo[:, 0:cw])) - 0.5
    a = jax.nn.sigmoid(a0 + lo[:, cw:2 * cw])
    kk = kc * k_k
    norm = jnp.sqrt(_dot_f32(kk * kk, seg_ref[...]))
    kk = kk / jnp.maximum(norm, 1e-12)
    kc = kc * (1.0 + (a - 1.0) * k_a)
    r_ref[...] = r
    w_ref[...] = jnp.exp(-jnp.exp(w_log))
    k_ref[...] = kc
    v_ref[...] = vc
    kk_ref[...] = kk
    b_ref[...] = -(kk * a)
    g_ref[...] = _dot(jax.nn.sigmoid(gd).astype(MXU_DTYPE), gup_ref[...])
    bonus_ref[...] = _dot_f32(r * kc * r_k, seg_ref[...]) * vc


def _rwkv_prep(pc, prev, mu, vecs, lora, g_up, seg, tm):
    n, width = prev.shape
    cw = vecs.shape[1]
    row = lambda w: pl.BlockSpec((tm, w), lambda i: (i, 0))
    full = lambda a: pl.BlockSpec(a.shape, lambda i: (0, 0))
    out = jax.ShapeDtypeStruct((n, cw), F32)
    return pl.pallas_call(
        _rwkv_prep_kernel,
        out_shape=(out,) * 8,
        grid=(n // tm,),
        in_specs=[row(width), row(width), full(mu), full(vecs), full(lora), full(g_up), full(seg)],
        out_specs=(row(cw),) * 8,
        compiler_params=_cparams("parallel"),
        name="rwkv_prep",
    )(pc, prev, mu, vecs, lora, g_up, seg)


def _segment_sum(x, seg):
    hi = x.astype(MXU_DTYPE)
    lo = (x - hi.astype(F32)).astype(MXU_DTYPE)
    return _dot(hi, seg) + _dot(lo, seg)


def _rwkv_scan_kernel(*refs, nb):
    ins, (seg_ref, s0_ref, y_ref, s_ref) = refs[:6 * nb], refs[6 * nb:]

    @pl.when(pl.program_id(1) == 0)
    def _():
        s_ref[...] = s0_ref[...]

    _, rows, width = s_ref.shape
    pairs = rows // RWKV_HEAD
    lane = lax.broadcasted_iota(I32, (nb * rows, width), 1)
    sub = lax.broadcasted_iota(I32, (nb * rows, width), 0)
    diag = (sub & (RWKV_HEAD - 1)) == (lane & (RWKV_HEAD - 1))
    own = (lax.broadcasted_iota(I32, (2, width), 0) == 0) == (lax.broadcasted_iota(I32, (2, width), 1) < RWKV_HEAD)
    seg = seg_ref[...]

    def group(g, carry):
        t0 = pl.multiple_of(g * SUBLANES, SUBLANES)
        r8, w8, k8, v8, kk8, nb8 = ([ins[6 * bb + q][pl.ds(t0, SUBLANES), :] for bb in range(nb)] for q in range(6))
        s = s_ref[...].reshape(nb * rows, width)
        for i in range(SUBLANES):
            def per_row(x8):
                return jnp.concatenate([jnp.broadcast_to(x[i:i + 1, p * width:(p + 1) * width], (RWKV_HEAD, width))
                                        for x in x8 for p in range(pairs)], axis=0)
            sa = _segment_sum(s * per_row(kk8), seg)
            vcol = _segment_sum(jnp.where(diag, per_row(v8), 0.0), seg)
            s = s * per_row(w8) + sa * per_row(nb8) + vcol * per_row(k8)
            sb = s.astype(MXU_DTYPE)
            for bb in range(nb):
                for p in range(pairs):
                    r2 = jnp.where(own, r8[bb][i:i + 1, p * width:(p + 1) * width], 0.0)
                    r0 = (bb * pairs + p) * RWKV_HEAD
                    y = _dot_nt(r2.astype(MXU_DTYPE), sb[r0:r0 + RWKV_HEAD])
                    y_ref[bb, pl.ds(t0 + i, 1), 2 * p:2 * p + 2, :] = y[None]
        s_ref[...] = s.reshape(nb, rows, width)
        return carry

    lax.fori_loop(0, ins[0].shape[0] // SUBLANES, group, 0)


def _rwkv_scan(ins, row_start, batch, t, s0_packed, seg2, nb):
    cw = ins[0].shape[1]
    heads = cw // RWKV_HEAD
    tb = min(t, 64)
    nt = t // tb
    rb0 = row_start // tb
    rows = [pl.BlockSpec((tb, cw), lambda b, i, bb=bb: (rb0 + (b * nb + bb) * nt + i, 0)) for bb in range(nb)]
    st = pl.BlockSpec((nb,) + s0_packed.shape[1:], lambda b, i: (b, 0, 0))
    return pl.pallas_call(
        functools.partial(_rwkv_scan_kernel, nb=nb),
        out_shape=(jax.ShapeDtypeStruct((batch, t, heads, RWKV_HEAD), F32),
                   jax.ShapeDtypeStruct(s0_packed.shape, F32)),
        grid=(batch // nb, nt),
        in_specs=[spec for spec in rows for _ in range(6)] + [pl.BlockSpec(seg2.shape, lambda b, i: (0, 0)), st],
        out_specs=(pl.BlockSpec((nb, tb, heads, RWKV_HEAD), lambda b, i: (b, i, 0, 0)), st),
        compiler_params=_cparams("parallel", "arbitrary"),
        name="rwkv_scan",
    )(*(list(ins) * nb), seg2, s0_packed)


def _rwkv_post_kernel(y_ref, bonus_ref, g_ref, ln_ref, seg_ref, o_ref):
    y = y_ref[...]
    avg = seg_ref[...] * (1.0 / RWKV_HEAD)
    d = y - _dot_f32(y, avg)
    var = _dot_f32(d * d, avg)
    yn = d * lax.rsqrt(var + GN_EPS) * ln_ref[0:1, :] + ln_ref[1:2, :]
    o_ref[...] = (yn + bonus_ref[...]) * g_ref[...]


def _rwkv_post(y, bonus, g, ln, seg, row_start, tm):
    n, cw = y.shape
    rb0 = row_start // tm
    row = pl.BlockSpec((tm, cw), lambda i: (i, 0))
    off = pl.BlockSpec((tm, cw), lambda i: (rb0 + i, 0))
    full = lambda a: pl.BlockSpec(a.shape, lambda i: (0, 0))
    return pl.pallas_call(
        _rwkv_post_kernel,
        out_shape=jax.ShapeDtypeStruct((n, cw), F32),
        grid=(n // tm,),
        in_specs=[row, off, off, full(ln), full(seg)],
        out_specs=row,
        compiler_params=_cparams("parallel"),
        name="rwkv_post",
    )(y, bonus, g, ln, seg)


MLA_HEADS = 8
MLA_NOPE = 64
MLA_ROPE = 32
MLA_SCALE = (MLA_NOPE + MLA_ROPE) ** -0.5


def _rope_tile(x, cos, sin):
    lane = lax.broadcasted_iota(I32, x.shape, 1)
    half = MLA_ROPE // 2
    rot = jnp.where(lane < half, pltpu.roll(x, LANES - half, 1), pltpu.roll(x, half, 1))
    return x * cos + rot * sin


def _mla_prep_kernel(qd_ref, ckv_ref, kr_ref, cos_ref, sin_ref, qn_ref, kvn_ref, wuq_ref, wuk_ref,
                     ql_ref, qr_ref, c_ref, krn_ref):
    cq = _rms(qd_ref[...], qn_ref[...])
    qh = _dot(cq.astype(MXU_DTYPE), wuq_ref[...])
    nope = MLA_HEADS * MLA_NOPE
    ql_ref[...] = _dot(qh[:, 0:nope].astype(MXU_DTYPE), wuk_ref[...]).astype(ql_ref.dtype)
    cos, sin = cos_ref[...], sin_ref[...]
    for h in range(MLA_HEADS):
        lanes = slice(nope + h * LANES, nope + (h + 1) * LANES)
        qr_ref[:, h * LANES:(h + 1) * LANES] = _rope_tile(qh[:, lanes], cos, sin).astype(qr_ref.dtype)
    c_ref[...] = _rms(ckv_ref[...], kvn_ref[...])
    krn_ref[...] = _rope_tile(kr_ref[...], cos, sin)


def _mla_prep(proj, col_blocks, row_start, nrows, cos, sin, q_norm, kv_norm, wuq, wuk, tm):
    qd0, ckv0, kr0 = col_blocks
    d_q, d_kv = q_norm.shape[1], kv_norm.shape[1]
    rb0 = row_start // tm
    nper = cos.shape[0] // tm
    full = lambda a: pl.BlockSpec(a.shape, lambda i: (0, 0))
    rows = lambda w: pl.BlockSpec((tm, w), lambda i: (i, 0))
    tab = pl.BlockSpec((tm, LANES), lambda i: (i % nper, 0))
    return pl.pallas_call(
        _mla_prep_kernel,
        out_shape=(jax.ShapeDtypeStruct((nrows, MLA_HEADS * d_kv), MXU_DTYPE),
                   jax.ShapeDtypeStruct((nrows, MLA_HEADS * LANES), MXU_DTYPE),
                   jax.ShapeDtypeStruct((nrows, d_kv), F32),
                   jax.ShapeDtypeStruct((nrows, LANES), F32)),
        grid=(nrows // tm,),
        in_specs=[pl.BlockSpec((tm, d_q), lambda i: (rb0 + i, qd0 * LANES // d_q)),
                  pl.BlockSpec((tm, d_kv), lambda i: (rb0 + i, ckv0 * LANES // d_kv)),
                  pl.BlockSpec((tm, LANES), lambda i: (rb0 + i, kr0)),
                  tab, tab, full(q_norm), full(kv_norm), full(wuq), full(wuk)],
        out_specs=(rows(MLA_HEADS * d_kv), rows(MLA_HEADS * LANES), rows(d_kv), rows(LANES)),
        compiler_params=_cparams("parallel"),
        name="mla_prep",
    )(proj, proj, proj, cos, sin, q_norm, kv_norm, wuq, wuk)


def _mla_prompt_kernel(ql_ref, qr_ref, c_ref, kr_ref, wuv_ref, o_ref):
    tq = ql_ref.shape[0]
    t, d_kv = c_ref.shape
    qpos = pl.program_id(1) * tq + lax.broadcasted_iota(I32, (tq, t), 0)
    causal = lax.broadcasted_iota(I32, (tq, t), 1) <= qpos
    c, kr = c_ref[...], kr_ref[...]
    outs = []
    for h in range(MLA_HEADS):
        s = _dot_nt(ql_ref[:, h * d_kv:(h + 1) * d_kv], c) + _dot_nt(qr_ref[:, h * LANES:(h + 1) * LANES], kr)
        outs.append(_masked_softmax_pv(s * MLA_SCALE, causal, c).astype(MXU_DTYPE))
    o_ref[...] = _dot(jnp.concatenate(outs, axis=1), wuv_ref[...])


def _mla_prompt(q_lat, q_rope, c, kr, wuv, batch, t, tq=128):
    d_kv = c.shape[1]
    nq = t // tq
    return pl.pallas_call(
        _mla_prompt_kernel,
        out_shape=jax.ShapeDtypeStruct((batch * t, wuv.shape[1]), F32),
        grid=(batch, nq),
        in_specs=[pl.BlockSpec((tq, MLA_HEADS * d_kv), lambda b, i: (b * nq + i, 0)),
                  pl.BlockSpec((tq, MLA_HEADS * LANES), lambda b, i: (b * nq + i, 0)),
                  pl.BlockSpec((t, d_kv), lambda b, i: (b, 0)),
                  pl.BlockSpec((t, LANES), lambda b, i: (b, 0)),
                  pl.BlockSpec(wuv.shape, lambda b, i: (0, 0))],
        out_specs=pl.BlockSpec((tq, wuv.shape[1]), lambda b, i: (b * nq + i, 0)),
        compiler_params=_cparams("parallel", "arbitrary"),
        name="mla_prompt",
    )(q_lat, q_rope, c, kr, wuv)


def _mla_sample_kernel(pt_ref, ql_ref, qr_ref, cn_ref, krn_ref, *rest, pages, t_new):
    c_pages, kr_pages = rest[:pages], rest[pages:2 * pages]
    o_ref, m_ref, l_ref, acc_ref = rest[2 * pages:]
    step = pl.program_id(1)

    @pl.when(step == 0)
    def _():
        m_ref[...] = jnp.full_like(m_ref, ONLINE_SOFTMAX_FLOOR)
        l_ref[...] = jnp.zeros_like(l_ref)
        acc_ref[...] = jnp.zeros_like(acc_ref)

    ql, qr = ql_ref[...], qr_ref[...]
    cc = jnp.concatenate([r[...] for r in c_pages], axis=0).astype(MXU_DTYPE)
    kc = jnp.concatenate([r[...] for r in kr_pages], axis=1).astype(MXU_DTYPE)
    s = (_dot_nt(ql, cc) + _dot(qr[:, 0:MLA_ROPE], kc)) * MLA_SCALE
    _online_softmax_step(m_ref, l_ref, acc_ref, 0, s, jnp.full(s.shape, True), cc)

    @pl.when(step == pl.num_programs(1) - 1)
    def _():
        cn = cn_ref[...]
        s_new = (_dot_nt(ql, cn) + _dot_nt(qr, krn_ref[...])) * MLA_SCALE
        qpos = lax.broadcasted_iota(I32, s_new.shape, 0) % t_new
        kpos = lax.broadcasted_iota(I32, s_new.shape, 1)
        _online_softmax_step(m_ref, l_ref, acc_ref, 0, s_new, (kpos <= qpos) & (kpos < t_new), cn)
        o_ref[...] = acc_ref[0] / l_ref[0]


def _mla_sample(page_table, q_lat, q_rope, c_new_pad, kr_new_pad, pool_c, pool_kr, layer, t_new):
    b, n_pages = page_table.shape
    pages = math.gcd(n_pages, PAGES_PER_STEP)
    _, rows, d_kv = q_lat.shape
    one = lambda a: pl.BlockSpec((None,) + a.shape[1:], lambda n, c, pt: (n, 0, 0))
    grid_spec = pltpu.PrefetchScalarGridSpec(
        num_scalar_prefetch=1,
        grid=(b, n_pages // pages),
        in_specs=[one(q_lat), one(q_rope), one(c_new_pad), one(kr_new_pad)]
        + _page_specs((None, None, PAGE_SIZE, d_kv), layer, pages)
        + _page_specs((None, None, MLA_ROPE, PAGE_SIZE), layer, pages),
        out_specs=pl.BlockSpec((None, rows, d_kv), lambda n, c, pt: (n, 0, 0)),
        scratch_shapes=[pltpu.VMEM((1, rows, 1), F32), pltpu.VMEM((1, rows, 1), F32),
                        pltpu.VMEM((1, rows, d_kv), F32)],
    )
    return pl.pallas_call(
        functools.partial(_mla_sample_kernel, pages=pages, t_new=t_new),
        out_shape=jax.ShapeDtypeStruct((b, rows, d_kv), F32),
        grid_spec=grid_spec,
        compiler_params=_cparams("parallel", "arbitrary"),
        name="mla_sample",
    )(page_table, q_lat, q_rope, c_new_pad, kr_new_pad, *([pool_c] * pages), *([pool_kr] * pages))


ROW_BLOCK = 256
A_HEADS = 4
B_HEADS, B_KV_HEADS, B_DH = 8, 4, 64
IDX_HEADS, IDX_DIM = 8, 64
C_WIDTH = 512
RWKV_IN = 3 * C_WIDTH + 64 + 64 + 128
D_Q_RANK, D_KV_RANK = 384, 256
ODD_COLS = RWKV_IN + D_KV_RANK + 2 * LANES + D_Q_RANK
ODD_SLABS = ((RWKV_IN + D_KV_RANK + 2 * LANES) // LANES, RWKV_IN // LANES, (RWKV_IN + D_KV_RANK) // LANES)


def _pad_rows(a, rows=LANES):
    return jnp.pad(a, ((0, 0),) * (a.ndim - 2) + ((0, rows - a.shape[-2]), (0, 0)))


def _head_major(a, b, t, h):
    return jnp.transpose(a.reshape(b, t, h, -1), (0, 2, 1, 3))


def _token_major(a):
    b, h, t, w = a.shape
    return jnp.transpose(a, (0, 2, 1, 3)).reshape(b * t, h * w)


def _rope_tables(pos):
    half = MLA_ROPE // 2
    inv = ROPE_THETA ** (-jnp.arange(0, MLA_ROPE, 2, dtype=F32) / MLA_ROPE)
    ang = pos.astype(F32)[:, None] * inv[None, :]
    zeros = jnp.zeros((pos.shape[0], LANES - 2 * half), F32)
    cos, sin = jnp.cos(ang), jnp.sin(ang)
    return jnp.concatenate([cos, cos, zeros], axis=1), jnp.concatenate([-sin, sin, zeros], axis=1)


def kernel(x_prompt, x_sample, cache_dsa_k, cache_dsa_v, cache_dsa_idx, cache_mla_ckv, cache_mla_krope, state_hgrn, state_rwkv, state_shift, page_table, norm_mix, norm_ffn, norm_final, w_in_even, w_out_even, hgrn_lb, hgrn_norm, w_in_odd, w_out_odd, rwkv_mu, rwkv_w0, rwkv_w_up, rwkv_a0, rwkv_a_up, rwkv_g_up, rwkv_k_k, rwkv_k_a, rwkv_r_k, rwkv_ln_w, rwkv_ln_b, mla_q_norm, mla_w_uq, mla_kv_norm, mla_w_uk, mla_w_uv, peer_wq, peer_subkeys, peer_u, peer_v):
    bp, tp, d_model = x_prompt.shape
    bs, ts, _ = x_sample.shape
    n_p, n_s = bp * tp, bs * ts
    n_all = n_p + n_s
    n_pad = _round_up(n_all, PEER_TOKEN_BLOCK)
    tm_s = min(ROW_BLOCK, n_s)
    n_past = page_table.shape[1] * PAGE_SIZE
    depth = norm_mix.shape[0]
    md = MXU_DTYPE

    def all_rows(p, s):
        return jnp.concatenate([p, s, jnp.zeros((n_pad - n_all, p.shape[1]), p.dtype)], axis=0)

    x = all_rows(x_prompt.reshape(n_p, d_model), x_sample.reshape(n_s, d_model))

    lb_cum = jnp.cumsum(jax.nn.softmax(hgrn_lb.astype(F32), axis=0), axis=0)
    lower_bounds = lb_cum - lb_cum[:1]
    seg = jnp.kron(jnp.eye(C_WIDTH // RWKV_HEAD, dtype=F32), jnp.ones((RWKV_HEAD, RWKV_HEAD), F32))
    pool_dsa_k = jnp.transpose(cache_dsa_k, (0, 1, 3, 4, 2))
    pool_dsa_v = jnp.transpose(cache_dsa_v, (0, 1, 3, 4, 2))
    pool_dsa_idx = jnp.transpose(cache_dsa_idx, (0, 1, 3, 2))
    pool_mla_kr = jnp.transpose(cache_mla_krope, (0, 1, 3, 2))
    eye_h = jnp.eye(MLA_HEADS, dtype=F32)
    cos_p, sin_p = _rope_tables(jnp.arange(tp))
    cos_s, sin_s = (jnp.tile(a, (tm_s // ts, 1)) for a in _rope_tables(n_past + jnp.arange(ts)))

    outs = {k: [] for k in ("pk", "pv", "pi", "pc", "pr", "ph", "ps", "psh", "sk", "sv", "si", "sc", "sr", "sh", "ss", "ssh")}

    for l in range(depth):
        j = l // 2
        if l % 2 == 0:
            w_in = jnp.pad(w_in_even[j], ((0, 0), (0, _round_up(w_in_even.shape[2], LANES) - w_in_even.shape[2]))).astype(md)
            proj = _matmul(x, w_in, g=norm_mix[l], tm=ROW_BLOCK)
            aw = A_HEADS * LANES
            c_q, c_k, c_v = 4 * aw, 4 * aw + B_HEADS * B_DH, 4 * aw + (B_HEADS + B_KV_HEADS) * B_DH
            c_qi = c_v + B_KV_HEADS * B_DH
            c_ki = c_qi + IDX_HEADS * IDX_DIM
            c_wi = c_ki + IDX_DIM
            mixes = []
            for rows, b, t, s0_t, kk, kv, ki_key, kh in ((slice(0, n_p), bp, tp, None, "pk", "pv", "pi", "ph"),
                                                    (slice(n_p, n_all), bs, ts, state_hgrn[j], "sk", "sv", "si", "sh")):
                sample = s0_t is not None
                s0_t = jnp.swapaxes(s0_t, -1, -2).astype(F32) if sample else jnp.zeros((b, A_HEADS, LANES, LANES), F32)
                oa, s_a = _hgrn(proj, rows.start, b, t, A_HEADS, lower_bounds[j], hgrn_norm[j], s0_t)
                pr = proj[rows]
                qb, kb, vb = pr[:, c_q:c_k], pr[:, c_k:c_v], pr[:, c_v:c_qi]
                qi, ki, wi = pr[:, c_qi:c_ki], pr[:, c_ki:c_wi], pr[:, c_wi:c_wi + IDX_HEADS]
                qi4 = _head_major(qi, b, t, IDX_HEADS).astype(md)
                q4 = _head_major(qb, b, t, B_HEADS).astype(md)
                k4 = _head_major(kb, b, t, B_KV_HEADS).astype(md)
                v4 = _head_major(vb, b, t, B_KV_HEADS).astype(md)
                wi3 = wi.reshape(b, t, IDX_HEADS)
                ki3 = ki.reshape(b, t, IDX_DIM).astype(md)
                if sample:
                    topk = min(DSA_TOPK_MAX, (n_past + t) // 4)
                    mask = _dsa_select(page_table, qi4, wi3, _pad_rows(ki3), pool_dsa_idx, j, topk, t)
                    group = B_HEADS // B_KV_HEADS
                    o4 = _dsa_attend(page_table, q4.reshape(b, B_KV_HEADS, group * t, B_DH), mask,
                                     _pad_rows(k4), _pad_rows(v4), pool_dsa_k, pool_dsa_v, j)
                    o4 = o4.reshape(b, B_HEADS, t, B_DH)
                else:
                    o4 = _dsa_prompt(qi4, wi3, ki3, q4, k4, v4)
                mixes.append(jnp.concatenate([oa, _token_major(o4)], axis=1))
                outs[kk].append(kb.reshape(b, t, B_KV_HEADS, B_DH))
                outs[kv].append(vb.reshape(b, t, B_KV_HEADS, B_DH))
                outs[ki_key].append(ki.reshape(b, t, IDX_DIM))
                outs[kh].append(jnp.swapaxes(s_a, -1, -2))
            x = _matmul(all_rows(*mixes), w_out_even[j].astype(md), res=x, tm=ROW_BLOCK)
        else:
            w = w_in_odd[j]
            c_qd, c_ckv = RWKV_IN, RWKV_IN + D_Q_RANK
            c_kr = c_ckv + D_KV_RANK
            w_in = jnp.concatenate([w[:, :RWKV_IN], w[:, c_ckv:c_kr], w[:, c_kr:c_kr + MLA_ROPE],
                                    jnp.zeros((d_model, 2 * LANES - MLA_ROPE), w.dtype), w[:, c_qd:c_ckv]], axis=1).astype(md)
            proj = _matmul(x, w_in, g=norm_mix[l], tm=ROW_BLOCK)
            pc_p = proj[:n_p, :RWKV_IN].reshape(bp, tp, RWKV_IN)
            pc_s = proj[n_p:n_all, :RWKV_IN].reshape(bs, ts, RWKV_IN)
            prev_p = jnp.concatenate([jnp.zeros((bp, 1, RWKV_IN), F32), pc_p[:, :-1]], axis=1)
            prev_s = jnp.concatenate([state_shift[j].astype(F32)[:, None], pc_s[:, :-1]], axis=1)
            prev = all_rows(prev_p.reshape(n_p, RWKV_IN), prev_s.reshape(n_s, RWKV_IN))
            zeros_l = jnp.zeros((rwkv_w_up.shape[1], C_WIDTH), F32)
            lora = jnp.concatenate([jnp.concatenate([rwkv_w_up[j], zeros_l], axis=1),
                                    jnp.concatenate([zeros_l, rwkv_a_up[j]], axis=1)], axis=0).astype(md)
            vecs = jnp.stack([rwkv_w0[j], rwkv_a0[j], rwkv_k_k[j], rwkv_k_a[j], rwkv_r_k[j].reshape(-1)]).astype(F32)
            prep = _rwkv_prep(proj, prev, rwkv_mu[j].reshape(1, -1).astype(F32), vecs, lora, rwkv_g_up[j].astype(md), seg, ROW_BLOCK)
            scan_in, g_all, bonus_all = prep[:6], prep[6], prep[7]
            ln = jnp.stack([rwkv_ln_w[j], rwkv_ln_b[j]]).astype(F32)

            wuq = mla_w_uq[j].reshape(D_Q_RANK, MLA_HEADS, MLA_NOPE + MLA_ROPE)
            wuq_rope = jnp.pad(wuq[:, :, MLA_NOPE:], ((0, 0), (0, 0), (0, LANES - MLA_ROPE))).reshape(D_Q_RANK, -1)
            wuq_p = jnp.concatenate([wuq[:, :, :MLA_NOPE].reshape(D_Q_RANK, -1), wuq_rope], axis=1).astype(md)
            wuk = jnp.einsum("chn,hg->hngc", mla_w_uk[j], eye_h).reshape(MLA_HEADS * MLA_NOPE, -1).astype(md)
            wuv = jnp.einsum("chv,hg->hcgv", mla_w_uv[j], eye_h).reshape(MLA_HEADS * D_KV_RANK, -1).astype(md)
            q_norm = mla_q_norm[j].reshape(1, -1).astype(F32)
            kv_norm = mla_kv_norm[j].reshape(1, -1).astype(F32)

            mixes = []
            for start, b, t, tm, cos, sin, sample in ((0, bp, tp, ROW_BLOCK, cos_p, sin_p, False),
                                                      (n_p, bs, ts, tm_s, cos_s, sin_s, True)):
                n = b * t
                if sample:
                    heads = C_WIDTH // RWKV_HEAD
                    s0 = state_rwkv[j].astype(F32).reshape(b, heads // 2, 2, RWKV_HEAD, RWKV_HEAD)
                    s0 = jnp.transpose(s0, (0, 1, 3, 2, 4)).reshape(b, heads // 2 * RWKV_HEAD, 2 * RWKV_HEAD)
                else:
                    s0 = jnp.zeros((b, C_WIDTH // LANES * RWKV_HEAD, LANES), F32)
                nb = math.gcd(b, RWKV_BATCH_BLOCK)
                y3, s_c = _rwkv_scan(scan_in, start, b, t, s0, seg[:LANES, :LANES].astype(md), nb)
                yc = _rwkv_post(y3.reshape(n, C_WIDTH), bonus_all, g_all, ln, seg, start, tm)
                s_c = jnp.transpose(s_c.reshape(b, -1, RWKV_HEAD, 2, RWKV_HEAD), (0, 1, 3, 2, 4))
                s_c = s_c.reshape(b, -1, RWKV_HEAD, RWKV_HEAD)

                q_lat, q_rope, c_new, kr_new = _mla_prep(proj, ODD_SLABS, start, n, cos, sin, q_norm, kv_norm, wuq_p, wuk, tm)
                if sample:
                    hq = lambda a: _head_major(a, b, t, MLA_HEADS).reshape(b, MLA_HEADS * t, -1)
                    o_lat = _mla_sample(page_table, hq(q_lat), hq(q_rope),
                                        _pad_rows(c_new.reshape(b, t, -1)).astype(md),
                                        _pad_rows(kr_new.reshape(b, t, -1)).astype(md),
                                        cache_mla_ckv, pool_mla_kr, j, t)
                    o_lat = _token_major(o_lat.reshape(b, MLA_HEADS, t, -1))
                    od = _matmul(o_lat, wuv, tm=tm)
                else:
                    od = _mla_prompt(q_lat, q_rope, c_new.astype(md), kr_new.astype(md), wuv, b, t)
                mixes.append(jnp.concatenate([yc, od], axis=1))
                pre = "s" if sample else "p"
                outs[pre + "c"].append(c_new.reshape(b, t, D_KV_RANK))
                outs[pre + "r"].append(kr_new[:, :MLA_ROPE].reshape(b, t, MLA_ROPE))
                outs[pre + "s"].append(s_c)
                outs[pre + "sh"].append((pc_s if sample else pc_p)[:, -1])
            x = _matmul(all_rows(*mixes), w_out_odd[j].astype(md), res=x, tm=ROW_BLOCK)
        x = _peer_ffn(x, norm_ffn[l], _peer_prepare(peer_wq[l], peer_subkeys[l], peer_u[l], peer_v[l]))

    y = _rmsnorm(x, norm_final)
    y_prompt = y[:n_p].reshape(bp, tp, d_model)
    y_sample = y[n_p:n_all].reshape(bs, ts, d_model)
    st = lambda k: jnp.stack(outs[k])
    return (y_prompt, y_sample,
            st("pk"), st("pv"), st("pi"), st("pc"), st("pr"), st("ph"), st("ps"), st("psh"),
            st("sk"), st("sv"), st("si"), st("sc"), st("sr"), st("sh"), st("ss"), st("ssh"))
```

```python
import functools
import math

import jax
import jax.numpy as jnp
from jax import lax
from jax.experimental import pallas as pl
from jax.experimental.pallas import tpu as pltpu

F32 = jnp.float32
I32 = jnp.int32
MXU_DTYPE = jnp.bfloat16
GATE_DTYPE = jnp.bfloat16
LANES = 128
SUBLANES = 8
VMEM_LIMIT_BYTES = 56 * 1024 * 1024
NEG_INF = float("-inf")

RMS_EPS = 1e-6
PAGE_SIZE = 128
ROPE_THETA = 10000.0
PEER_HEADS = 8
PEER_NKEYS = 128
PEER_TOPK = 16
PEER_TOKEN_BLOCK = 512
PEER_GROUP = 16


def _cparams(*sem):
    return pltpu.CompilerParams(dimension_semantics=sem, vmem_limit_bytes=VMEM_LIMIT_BYTES)


def _round_up(n, m):
    return -(-n // m) * m


def _rms(x, g):
    return x * lax.rsqrt(jnp.mean(x * x, axis=-1, keepdims=True) + RMS_EPS) * g


def _dot(a, b):
    return jnp.dot(a, b, preferred_element_type=F32)


def _dot_nt(a, b):
    return lax.dot_general(a, b, (((1,), (1,)), ((), ())), preferred_element_type=F32)


def _matmul_kernel(*refs, norm, residual):
    it = iter(refs)
    a_ref = next(it)
    g_ref = next(it) if norm else None
    w_ref = next(it)
    r_ref = next(it) if residual else None
    o_ref = next(it)
    a = a_ref[...]
    if norm:
        a = _rms(a, g_ref[...])
    acc = _dot(a.astype(w_ref.dtype), w_ref[...])
    if residual:
        acc = acc + r_ref[...]
    o_ref[...] = acc


def _matmul(a, w, g=None, res=None, tm=256):
    n, k = a.shape
    m = w.shape[1]
    ins, specs = [a], [pl.BlockSpec((tm, k), lambda i: (i, 0))]
    if g is not None:
        ins.append(g.reshape(1, k).astype(F32))
        specs.append(pl.BlockSpec((1, k), lambda i: (0, 0)))
    ins.append(w)
    specs.append(pl.BlockSpec((k, m), lambda i: (0, 0)))
    if res is not None:
        ins.append(res)
        specs.append(pl.BlockSpec((tm, m), lambda i: (i, 0)))
    return pl.pallas_call(
        functools.partial(_matmul_kernel, norm=g is not None, residual=res is not None),
        out_shape=jax.ShapeDtypeStruct((n, m), F32),
        grid=(n // tm,),
        in_specs=specs,
        out_specs=pl.BlockSpec((tm, m), lambda i: (i, 0)),
        compiler_params=_cparams("parallel"),
        name="proj_matmul",
    )(*ins)


def _rmsnorm_kernel(x_ref, g_ref, o_ref):
    o_ref[...] = _rms(x_ref[...], g_ref[...])


def _rmsnorm(x, g, tm=512):
    n, d = x.shape
    return pl.pallas_call(
        _rmsnorm_kernel,
        out_shape=jax.ShapeDtypeStruct((n, d), F32),
        grid=(n // tm,),
        in_specs=[pl.BlockSpec((tm, d), lambda i: (i, 0)), pl.BlockSpec((1, d), lambda i: (0, 0))],
        out_specs=pl.BlockSpec((tm, d), lambda i: (i, 0)),
        compiler_params=_cparams("parallel"),
        name="final_rmsnorm",
    )(x, g.reshape(1, d).astype(F32))


def _peer_fold_kernel(sub_ref, wq_ref, o_ref):
    o_ref[...] = _dot_nt(sub_ref[...], wq_ref[...])


def _peer_fold(subkeys, wq):
    d_model = wq.shape[0]
    half = subkeys.shape[2]
    return pl.pallas_call(
        _peer_fold_kernel,
        out_shape=jax.ShapeDtypeStruct((2, PEER_HEADS, PEER_NKEYS, d_model), F32),
        grid=(2, PEER_HEADS),
        in_specs=[pl.BlockSpec((None, PEER_NKEYS, half), lambda p, h: (p, 0, 0)),
                  pl.BlockSpec((d_model, half), lambda p, h: (0, h * 2 + p))],
        out_specs=pl.BlockSpec((None, None, PEER_NKEYS, d_model), lambda p, h: (p, h, 0, 0)),
        compiler_params=_cparams("parallel", "parallel"),
        name="peer_fold",
    )(subkeys.astype(MXU_DTYPE), wq.astype(MXU_DTYPE))


def _peer_scores_kernel(x_ref, g_ref, m_ref, s_ref, h_ref):
    hb = _rms(x_ref[...], g_ref[...]).astype(MXU_DTYPE)
    s_ref[...] = _dot_nt(m_ref[...], hb)
    h_ref[...] = hb.T


def _peer_scores(x, g, mcat, tm=PEER_TOKEN_BLOCK):
    n, d = x.shape
    rows = mcat.shape[0]
    return pl.pallas_call(
        _peer_scores_kernel,
        out_shape=(jax.ShapeDtypeStruct((rows, n), F32), jax.ShapeDtypeStruct((d, n), MXU_DTYPE)),
        grid=(n // tm,),
        in_specs=[pl.BlockSpec((tm, d), lambda i: (i, 0)),
                  pl.BlockSpec((1, d), lambda i: (0, 0)),
                  pl.BlockSpec((rows, d), lambda i: (0, 0))],
        out_specs=(pl.BlockSpec((rows, tm), lambda i: (0, i)), pl.BlockSpec((d, tm), lambda i: (0, i))),
        compiler_params=_cparams("parallel"),
        name="peer_scores",
    )(x, g.reshape(1, d).astype(F32), mcat)


def _peer_topk_kernel(s_ref, c1_ref, e1_ref, r2_ref, e2_ref):
    hk = PEER_HEADS * PEER_NKEYS
    tn = s_ref.shape[1]

    a, b, prev = [], [], None
    for _ in range(PEER_TOPK):
        def body(i, m, prev=prev):
            out = []
            for half in range(2):
                x = s_ref[pl.ds(pl.multiple_of(half * hk + i * PEER_HEADS, PEER_HEADS), PEER_HEADS), :]
                if prev is not None:
                    x = jnp.where(x < prev[half], x, NEG_INF)
                out.append(jnp.maximum(m[half], x))
            return tuple(out)
        init = jnp.full((PEER_HEADS, tn), NEG_INF, F32)
        prev = lax.fori_loop(0, PEER_NKEYS, body, (init, init), unroll=8)
        a.append(prev[0])
        b.append(prev[1])
    cands = [a[k] + b[l] for k in range(PEER_TOPK) for l in range(PEER_TOPK) if (k + 1) * (l + 1) <= PEER_TOPK]
    top = a[0] + b[0]
    tau = top
    for _ in range(PEER_TOPK - 1):
        nxt = jnp.full_like(tau, NEG_INF)
        for c in cands:
            nxt = jnp.maximum(nxt, jnp.where(c < tau, c, NEG_INF))
        tau = nxt
    z = jnp.zeros_like(tau)
    for c in cands:
        z = z + jnp.where(c >= tau, jnp.exp(c - top), 0.0)
    inv_z = 1.0 / z

    twice = lambda x: jnp.concatenate([x, x], axis=0)
    tau2, a0_2, inv_z2, b2 = twice(tau), twice(a[0]), twice(inv_z), [twice(x) for x in b]

    def first_half(i, carry):
        rows = pl.ds(pl.multiple_of(i * 2 * PEER_HEADS, 2 * PEER_HEADS), 2 * PEER_HEADS)
        x = s_ref[rows, :]
        cnt = jnp.zeros_like(x)
        for bl in b2:
            cnt = cnt + jnp.where(x + bl >= tau2, 1.0, 0.0)
        c1_ref[rows, :] = cnt.astype(c1_ref.dtype)
        e1_ref[rows, :] = (jnp.exp(x - a0_2) * inv_z2).astype(e1_ref.dtype)
        return carry

    lax.fori_loop(0, PEER_NKEYS // 2, first_half, 0)
    for h in range(PEER_HEADS):
        x = s_ref[pl.ds(2 * hk + h * PEER_NKEYS, PEER_NKEYS), :]
        rank = jnp.zeros_like(x)
        for bl in b:
            rank = rank + jnp.where(bl[h:h + 1, :] > x, 1.0, 0.0)
        rows = slice(h * PEER_NKEYS, (h + 1) * PEER_NKEYS)
        r2_ref[rows, :] = rank.astype(r2_ref.dtype)
        e2_ref[rows, :] = jnp.exp(x - b[0][h:h + 1, :]).astype(e2_ref.dtype)


def _peer_topk(s_t, tn=PEER_TOKEN_BLOCK):
    rows, n = s_t.shape
    hk = PEER_HEADS * PEER_NKEYS
    out = jax.ShapeDtypeStruct((hk, n), GATE_DTYPE)
    spec = pl.BlockSpec((hk, tn), lambda i: (0, i))
    return pl.pallas_call(
        _peer_topk_kernel,
        out_shape=(out,) * 4,
        grid=(n // tn,),
        in_specs=[pl.BlockSpec((rows, tn), lambda i: (0, i))],
        out_specs=(spec,) * 4,
        compiler_params=_cparams("parallel"),
        name="peer_topk",
    )(s_t)


def _gelu(x):
    return 0.5 * x * (1.0 + lax.erf(x * (1.0 / math.sqrt(2.0))))


PEER_QUAD = 4 * PEER_NKEYS
GATE_TILE = (64, 256)


def _peer_experts_kernel(h_ref, c1_ref, e1_ref, r2_ref, e2_ref, u_ref, v_ref, o_ref, a_ref):
    @pl.when(pl.program_id(1) == 0)
    def _():
        o_ref[...] = jnp.zeros_like(o_ref)

    hb = h_ref[...]
    tn = hb.shape[1]
    te, tt = GATE_TILE
    for quad in range(u_ref.shape[0] // PEER_QUAD):
        st = _dot(u_ref[quad * PEER_QUAD:(quad + 1) * PEER_QUAD, :], hb)
        for sub in range(PEER_QUAD // PEER_NKEYS):
            i1 = quad * (PEER_QUAD // PEER_NKEYS) + sub
            c1 = c1_ref[i1 * PEER_HEADS:(i1 + 1) * PEER_HEADS, :]
            e1 = e1_ref[i1 * PEER_HEADS:(i1 + 1) * PEER_HEADS, :]
            for r0 in range(0, PEER_NKEYS, te):
                for l0 in range(0, tn, tt):
                    lanes = slice(l0, l0 + tt)
                    gate = None
                    for h in range(PEER_HEADS):
                        rows = slice(h * PEER_NKEYS + r0, h * PEER_NKEYS + r0 + te)
                        sel = r2_ref[rows, lanes] < c1[h:h + 1, lanes]
                        term = jnp.where(sel, e2_ref[rows, lanes], 0.0) * e1[h:h + 1, lanes]
                        gate = term if gate is None else gate + term
                    rows = slice(sub * PEER_NKEYS + r0, sub * PEER_NKEYS + r0 + te)
                    a_ref[rows, lanes] = (_gelu(st[rows, lanes]).astype(gate.dtype) * gate).astype(a_ref.dtype)
        o_ref[...] += _dot(v_ref[quad], a_ref[...])


def _peer_experts(h_t, c1, e1, r2, e2, u2, v3, tn=PEER_TOKEN_BLOCK):
    d, n = h_t.shape
    hk = PEER_HEADS * PEER_NKEYS
    ge = PEER_GROUP * PEER_NKEYS
    tok = lambda rows: pl.BlockSpec((rows, tn), lambda i, g: (0, i))
    grp = pl.BlockSpec((PEER_GROUP * PEER_HEADS, tn), lambda i, g: (g, i))
    return pl.pallas_call(
        _peer_experts_kernel,
        out_shape=jax.ShapeDtypeStruct((d, n), F32),
        grid=(n // tn, u2.shape[0] // ge),
        in_specs=[tok(d), grp, grp, tok(hk), tok(hk),
                  pl.BlockSpec((ge, d), lambda i, g: (g, 0)),
                  pl.BlockSpec((ge // PEER_QUAD, d, PEER_QUAD), lambda i, g: (g, 0, 0))],
        out_specs=tok(d),
        scratch_shapes=[pltpu.VMEM((PEER_QUAD, tn), MXU_DTYPE)],
        compiler_params=_cparams("parallel", "arbitrary"),
        name="peer_experts",
    )(h_t, c1, e1, r2, e2, u2, v3)


def _peer_prepare(wq, subkeys, u, v):
    d = wq.shape[0]
    hk = PEER_HEADS * PEER_NKEYS
    mf = _peer_fold(subkeys, wq)
    inter = jnp.transpose(mf, (0, 2, 1, 3)).reshape(2 * hk, d)
    mcat = jnp.concatenate([inter, mf[1].reshape(hk, d)], axis=0).astype(MXU_DTYPE)
    v3 = jnp.transpose(v.astype(MXU_DTYPE).reshape(-1, PEER_QUAD, d), (0, 2, 1))
    return mcat, u.astype(MXU_DTYPE), v3


def _peer_ffn(x, g, prep):
    mcat, u2, v3 = prep
    s_t, h_t = _peer_scores(x, g, mcat)
    y_t = _peer_experts(h_t, *_peer_topk(s_t), u2, v3)
    return x + y_t.T


HGRN_CHUNK = 64
HGRN_SUB = 16
MASKED_EXPONENT = -1e30


def _cumsum_rows(x):
    rows = x.shape[0]
    row = lax.broadcasted_iota(I32, x.shape, 0)
    d = 1
    while d < rows:
        x = x + jnp.where(row >= d, pltpu.roll(x, d, 0), 0.0)
        d *= 2
    return x


def _hgrn_chunk(q, k, v, g, s_t, sub):
    c = q.shape[0]
    cum = _cumsum_rows(g)
    o = _dot_nt((q * jnp.exp(cum)).astype(MXU_DTYPE), s_t.astype(MXU_DTYPE))
    outs = []
    for blk in range(c // sub):
        r0 = blk * sub
        q_b, cum_b, k_b, v_b = q[r0:r0 + sub], cum[r0:r0 + sub], k[r0:r0 + sub], v[r0:r0 + sub]
        o_b = o[r0:r0 + sub]
        if blk > 0:
            base = cum[r0 - 1:r0]
            qs = q_b * jnp.exp(cum_b - base)
            ks = k[0:r0] * jnp.exp(base - cum[0:r0])
            att = _dot_nt(qs.astype(MXU_DTYPE), ks.astype(MXU_DTYPE))
            o_b = o_b + _dot(att.astype(MXU_DTYPE), v[0:r0].astype(MXU_DTYPE))
        row = lax.broadcasted_iota(I32, (sub, q.shape[1]), 0)
        for s in range(sub):
            dec = jnp.exp(jnp.where(row >= s, cum_b - cum_b[s:s + 1], MASKED_EXPONENT))
            att = jnp.sum(q_b * k_b[s:s + 1] * dec, axis=1, keepdims=True)
            o_b = o_b + att * v_b[s:s + 1]
        outs.append(o_b)
    o = outs[0] if len(outs) == 1 else jnp.concatenate(outs, axis=0)
    last = cum[c - 1:c]
    kd = k * jnp.exp(last - cum)
    upd = lax.dot_general(v.astype(MXU_DTYPE), kd.astype(MXU_DTYPE), (((0,), (0,)), ((), ())),
                          preferred_element_type=F32)
    return o, s_t * jnp.exp(last) + upd


def _hgrn_kernel(q_ref, f_ref, i_ref, g_ref, lb_ref, gain_ref, s0_ref, o_ref, s_ref, *, chunk, sub):
    @pl.when(pl.program_id(2) == 0)
    def _():
        s_ref[...] = s0_ref[...]

    lb = lb_ref[...]
    s_t = s_ref[...]
    for c0 in range(0, q_ref.shape[0], chunk):
        rows = slice(c0, c0 + chunk)
        f = lb + (1.0 - lb) * jax.nn.sigmoid(f_ref[rows, :])
        o, s_t = _hgrn_chunk(jax.nn.silu(q_ref[rows, :]), 1.0 - f, i_ref[rows, :], jnp.log(f), s_t, sub)
        o_ref[rows, :] = _rms(o, gain_ref[...]) * jax.nn.silu(g_ref[rows, :])
    s_ref[...] = s_t


def _hgrn(proj, row_start, batch, t, heads, lb, gain, s0_t):
    chunk = math.gcd(t, HGRN_CHUNK)
    sub = min(HGRN_SUB, chunk)
    tc = min(t, 4 * chunk)
    nt = t // tc
    rb0 = row_start // tc
    dk = LANES

    def slab(k):
        return pl.BlockSpec((tc, dk), lambda b, h, i, k=k: (rb0 + b * nt + i, k * heads + h))

    vec = pl.BlockSpec((1, dk), lambda b, h, i: (0, h))
    st = pl.BlockSpec((None, None, dk, dk), lambda b, h, i: (b, h, 0, 0))
    return pl.pallas_call(
        functools.partial(_hgrn_kernel, chunk=chunk, sub=sub),
        out_shape=(jax.ShapeDtypeStruct((batch * t, heads * dk), F32),
                   jax.ShapeDtypeStruct((batch, heads, dk, dk), F32)),
        grid=(batch, heads, nt),
        in_specs=[slab(0), slab(1), slab(2), slab(3), vec, vec, st],
        out_specs=(pl.BlockSpec((tc, dk), lambda b, h, i: (b * nt + i, h)), st),
        compiler_params=_cparams("parallel", "parallel", "arbitrary"),
        name="hgrn2",
    )(proj, proj, proj, proj, lb.reshape(1, -1), gain.reshape(1, -1), s0_t)


DSA_TOPK_MAX = 256
DSA_QUERY_BLOCK = 128
INT32_MIN = -2 ** 31


def _count(m):
    return jnp.sum(jnp.where(m, 1.0, 0.0), axis=1, keepdims=True)


def _float_key(x):
    u = lax.bitcast_convert_type(x, I32)
    return u ^ ((u >> 31) & I32(0x7FFFFFFF))


KEY_NEG_INF = INT32_MIN + 0x007FFFFF
INT32_MAX = 2 ** 31 - 1


def _topk_thresholds(count, k, idx_bits, rows):
    kf = float(k)
    v = jnp.where(count(lambda key, idx: key >= 0) >= kf, I32(0), I32(INT32_MIN))

    def value_bit(it, v):
        t = v | (I32(1) << (I32(30) - it))
        return jnp.where(count(lambda key, idx: key >= t) >= kf, t, v)

    v = lax.fori_loop(0, 31, value_bit, v)
    at_least = count(lambda key, idx: key >= v)

    def break_ties():
        need = kf - count(lambda key, idx: key > v)

        def index_bit(it, j):
            t = j | (I32(1) << (I32(idx_bits - 1) - it))
            return jnp.where(count(lambda key, idx: (key == v) & (idx < t)) < need, t, j)

        return lax.fori_loop(0, idx_bits, index_bit, jnp.zeros((rows, 1), I32))

    j = lax.cond(jnp.max(at_least) > kf, break_ties, lambda: jnp.full((rows, 1), INT32_MAX, I32))
    return v, j


def _selected(key, idx, v, j):
    return ((key > v) | ((key == v) & (idx <= j))) & (key > KEY_NEG_INF)


def _topk_mask(scores, k, idx_bits):
    key = _float_key(scores)
    idx = lax.broadcasted_iota(I32, scores.shape, 1)
    v, j = _topk_thresholds(lambda pred: _count(pred(key, idx)), k, idx_bits, scores.shape[0])
    return _selected(key, idx, v, j)


def _index_scores(qi, w, ki, heads, keys_transposed=False):
    d = qi.shape[1]
    dots = (_dot(qi, ki) if keys_transposed else _dot_nt(qi, ki)) * (d ** -0.5)
    terms = jnp.maximum(dots, 0.0) * (w * (heads ** -0.5))
    r = qi.shape[0] // heads
    acc = terms[0:r]
    for h in range(1, heads):
        acc = acc + terms[h * r:(h + 1) * r]
    return acc


def _masked_softmax_pv(s, mask, v):
    s = jnp.where(mask, s, NEG_INF)
    p = jnp.exp(s - jnp.max(s, axis=1, keepdims=True))
    return _dot(p.astype(MXU_DTYPE), v) / jnp.sum(p, axis=1, keepdims=True)


PAGES_PER_STEP = 32
ONLINE_SOFTMAX_FLOOR = -1e30
KEY_BLOCK = 512


def _dsa_prompt_kernel(qi_ref, w_ref, ki_ref, q_ref, k_ref, v_ref, o_ref, *, topk, hi, kb_size):
    heads, tq, dh = q_ref.shape
    kv_heads = k_ref.shape[0]
    t = ki_ref.shape[0]
    q0 = pl.program_id(1) * tq
    n_kb = (q0 + tq - 1) // kb_size + 1

    def attend(s_keys):
        qpos = q0 + lax.broadcasted_iota(I32, (tq, s_keys), 0)
        kpos = lax.broadcasted_iota(I32, (tq, s_keys), 1)
        scores = _index_scores(qi_ref[...], w_ref[...], ki_ref[0:s_keys, :], hi)
        mask = _topk_mask(jnp.where(kpos <= qpos, scores, NEG_INF), topk, max(1, (s_keys - 1).bit_length()))
        for h in range(heads):
            g = h // (heads // kv_heads)
            s = _dot_nt(q_ref[h], k_ref[g, 0:s_keys, :]) * (dh ** -0.5)
            o_ref[h] = _masked_softmax_pv(s, mask, v_ref[g, 0:s_keys, :])

    for n in range(1, t // kb_size + 1):
        pl.when(n_kb == n)(functools.partial(attend, n * kb_size))


def _dsa_prompt(qi2, wcol, ki, q4, k4, v4, hi, tq=128):
    b, t, d = ki.shape
    _, h, _, dh = q4.shape
    hkv = k4.shape[1]
    topk = min(DSA_TOPK_MAX, t // 4)
    kb = math.gcd(t, KEY_BLOCK)
    return pl.pallas_call(
        functools.partial(_dsa_prompt_kernel, topk=topk, hi=hi, kb_size=kb),
        out_shape=jax.ShapeDtypeStruct((b, h, t, dh), F32),
        grid=(b, t // tq),
        in_specs=[pl.BlockSpec((None, None, hi * tq, d), lambda n, i: (n, i, 0, 0)),
                  pl.BlockSpec((None, None, hi * tq, 1), lambda n, i: (n, i, 0, 0)),
                  pl.BlockSpec((None, t, d), lambda n, i: (n, 0, 0)),
                  pl.BlockSpec((None, h, tq, dh), lambda n, i: (n, 0, i, 0)),
                  pl.BlockSpec((None, hkv, t, dh), lambda n, i: (n, 0, 0, 0)),
                  pl.BlockSpec((None, hkv, t, dh), lambda n, i: (n, 0, 0, 0))],
        out_specs=pl.BlockSpec((None, h, tq, dh), lambda n, i: (n, 0, i, 0)),
        compiler_params=_cparams("parallel", "arbitrary"),
        name="dsa_prompt",
    )(qi2, wcol, ki, q4, k4, v4)


def _page_specs(block, layer, pages):
    zeros = (0,) * (len(block) - 2)
    return [pl.BlockSpec(block, lambda b, c, pt, r=r: (layer, pt[b, c * pages + r]) + zeros) for r in range(pages)]


def _dsa_select_kernel(pt_ref, qi_ref, w_ref, kn_ref, *rest, pages, topk, t_new, hi):
    page_refs, (mask_ref, sc_ref) = rest[:pages], rest[pages:]
    c = pl.program_id(1)
    nc = pl.num_programs(1)
    kc = jnp.concatenate([r[...] for r in page_refs], axis=1).astype(MXU_DTYPE)
    sc_ref[c] = _index_scores(qi_ref[...], w_ref[...], kc, hi, keys_transposed=True)

    @pl.when(c == nc - 1)
    def _():
        n_chunks, rows, ch = sc_ref.shape
        new = _index_scores(qi_ref[...], w_ref[...], kn_ref[...], hi)
        qpos = lax.broadcasted_iota(I32, new.shape, 0)
        kpos = lax.broadcasted_iota(I32, new.shape, 1)
        new = jnp.where((kpos <= qpos) & (kpos < t_new), new, NEG_INF)
        pieces = [sc_ref[i] for i in range(n_chunks)] + [new]
        if ch > new.shape[1]:
            pieces.append(jnp.full((rows, ch - new.shape[1]), NEG_INF, F32))
        scores = jnp.concatenate(pieces, axis=1)
        mask = _topk_mask(scores, topk, scores.shape[1].bit_length())
        for i in range(n_chunks + 1):
            mask_ref[i] = jnp.where(mask[:, i * ch:(i + 1) * ch], 1.0, 0.0)


def _dsa_select(page_table, qi2, wcol, ki_new_pad, pool_idx, layer, topk, t_new, hi):
    b, n_pages = page_table.shape
    pages = math.gcd(n_pages, PAGES_PER_STEP)
    nc = n_pages // pages
    ch = pages * PAGE_SIZE
    _, rows, d = qi2.shape
    t = rows // hi
    grid_spec = pltpu.PrefetchScalarGridSpec(
        num_scalar_prefetch=1,
        grid=(b, nc),
        in_specs=[pl.BlockSpec((None, rows, d), lambda n, c, pt: (n, 0, 0)),
                  pl.BlockSpec((None, rows, 1), lambda n, c, pt: (n, 0, 0)),
                  pl.BlockSpec((None, LANES, d), lambda n, c, pt: (n, 0, 0))]
        + _page_specs((None, None, d, PAGE_SIZE), layer, pages),
        out_specs=pl.BlockSpec((None, nc + 1, t, ch), lambda n, c, pt: (n, 0, 0, 0)),
        scratch_shapes=[pltpu.VMEM((nc, t, ch), F32)],
    )
    return pl.pallas_call(
        functools.partial(_dsa_select_kernel, pages=pages, topk=topk, t_new=t_new, hi=hi),
        out_shape=jax.ShapeDtypeStruct((b, nc + 1, t, ch), F32),
        grid_spec=grid_spec,
        compiler_params=_cparams("parallel", "arbitrary"),
        name="dsa_sample_select",
    )(page_table, qi2, wcol, ki_new_pad, *([pool_idx] * pages))


def _online_softmax_step(m_ref, l_ref, acc_ref, j, s, mask, v, values_transposed=False):
    m_old = m_ref[j]
    m_new = jnp.maximum(m_old, jnp.max(jnp.where(mask, s, ONLINE_SOFTMAX_FLOOR), axis=1, keepdims=True))
    alpha = jnp.exp(m_old - m_new)
    p = jnp.where(mask, jnp.exp(s - m_new), 0.0)
    l_ref[j] = alpha * l_ref[j] + jnp.sum(p, axis=1, keepdims=True)
    pv = _dot_nt(p.astype(MXU_DTYPE), v) if values_transposed else _dot(p.astype(MXU_DTYPE), v)
    acc_ref[j] = alpha * acc_ref[j] + pv
    m_ref[j] = m_new


def _online_softmax_heads(m_ref, l_ref, acc_ref, scores, mask, pv_fn):
    rows = scores[0].shape[0]
    s = jnp.concatenate(scores, axis=0)
    mask = jnp.concatenate([mask] * (s.shape[0] // mask.shape[0]), axis=0)
    m_old = m_ref[...]
    m_new = jnp.maximum(m_old, jnp.max(jnp.where(mask, s, ONLINE_SOFTMAX_FLOOR), axis=1, keepdims=True))
    alpha = jnp.exp(m_old - m_new)
    p = jnp.where(mask, jnp.exp(s - m_new), 0.0)
    l_ref[...] = alpha * l_ref[...] + jnp.sum(p, axis=1, keepdims=True)
    pb = p.astype(MXU_DTYPE)
    pv = jnp.concatenate([pv_fn(g, pb[g * rows:(g + 1) * rows]) for g in range(len(scores))], axis=0)
    acc_ref[...] = alpha * acc_ref[...] + pv
    m_ref[...] = m_new


def _dsa_attend_kernel(pt_ref, q_ref, mc_ref, mn_ref, kn_ref, vn_ref, *rest, pages):
    k_pages, v_pages = rest[:pages], rest[pages:2 * pages]
    o_ref, m_ref, l_ref, acc_ref = rest[2 * pages:]
    c = pl.program_id(1)
    kv_heads, rows, dh = q_ref.shape
    scale = dh ** -0.5

    @pl.when(c == 0)
    def _():
        m_ref[...] = jnp.full_like(m_ref, ONLINE_SOFTMAX_FLOOR)
        l_ref[...] = jnp.zeros_like(l_ref)
        acc_ref[...] = jnp.zeros_like(acc_ref)

    def keys_t(refs, g):
        return jnp.concatenate([r[g] for r in refs], axis=1).astype(MXU_DTYPE)

    scores = [_dot(q_ref[g], keys_t(k_pages, g)) * scale for g in range(kv_heads)]
    _online_softmax_heads(m_ref, l_ref, acc_ref, scores, mc_ref[...] > 0.0,
                          lambda g, p: _dot_nt(p, keys_t(v_pages, g)))

    @pl.when(c == pl.num_programs(1) - 1)
    def _():
        new = [_dot_nt(q_ref[g], kn_ref[g]) * scale for g in range(kv_heads)]
        _online_softmax_heads(m_ref, l_ref, acc_ref, new, mn_ref[:, 0:LANES] > 0.0, lambda g, p: _dot(p, vn_ref[g]))
        o_ref[...] = (acc_ref[...] / l_ref[...]).reshape(o_ref.shape)


def _dsa_attend(page_table, q4, mask, k_new_pad, v_new_pad, pool_k, pool_v, layer):
    b, n_pages = page_table.shape
    pages = math.gcd(n_pages, PAGES_PER_STEP)
    nc = n_pages // pages
    _, hkv, rows, dh = q4.shape
    _, _, t, ch = mask.shape
    grid_spec = pltpu.PrefetchScalarGridSpec(
        num_scalar_prefetch=1,
        grid=(b, nc),
        in_specs=[pl.BlockSpec((None, hkv, rows, dh), lambda n, c, pt: (n, 0, 0, 0)),
                  pl.BlockSpec((None, None, t, ch), lambda n, c, pt: (n, c, 0, 0)),
                  pl.BlockSpec((None, None, t, ch), lambda n, c, pt: (n, nc, 0, 0)),
                  pl.BlockSpec((None, hkv, LANES, dh), lambda n, c, pt: (n, 0, 0, 0)),
                  pl.BlockSpec((None, hkv, LANES, dh), lambda n, c, pt: (n, 0, 0, 0))]
        + _page_specs((None, None, hkv, dh, PAGE_SIZE), layer, pages)
        + _page_specs((None, None, hkv, dh, PAGE_SIZE), layer, pages),
        out_specs=pl.BlockSpec((None, hkv, rows, dh), lambda n, c, pt: (n, 0, 0, 0)),
        scratch_shapes=[pltpu.VMEM((hkv * rows, 1), F32), pltpu.VMEM((hkv * rows, 1), F32),
                        pltpu.VMEM((hkv * rows, dh), F32)],
    )
    return pl.pallas_call(
        functools.partial(_dsa_attend_kernel, pages=pages),
        out_shape=jax.ShapeDtypeStruct((b, hkv, rows, dh), F32),
        grid_spec=grid_spec,
        compiler_params=_cparams("parallel", "arbitrary"),
        name="dsa_sample_attend",
    )(page_table, q4, mask, mask, k_new_pad, v_new_pad, *([pool_k] * pages), *([pool_v] * pages))


GN_EPS = 64e-5
RWKV_HEAD = 64
RWKV_BATCH_BLOCK = 4


def _dot_f32(a, b):
    return jnp.dot(a, b, preferred_element_type=F32, precision=lax.Precision.HIGHEST)


def _rwkv_prep_kernel(pc_ref, prev_ref, mu_ref, vec_ref, lora_ref, gup_ref, seg_ref,
                      r_ref, w_ref, k_ref, v_ref, kk_ref, b_ref, g_ref, bonus_ref):
    cw = r_ref.shape[1]
    pc = pc_ref[...]
    xm = pc + (prev_ref[...] - pc) * mu_ref[...]
    r, kc, vc = xm[:, 0:cw], xm[:, cw:2 * cw], xm[:, 2 * cw:3 * cw]
    wa = xm[:, 3 * cw:3 * cw + LANES]
    gd = xm[:, 3 * cw + LANES:]
    lane = lax.broadcasted_iota(I32, wa.shape, 1)
    wa = jnp.where(lane < LANES // 2, jnp.tanh(wa), wa)
    lo = _dot(wa.astype(MXU_DTYPE), lora_ref[...])
    w0, a0, k_k, k_a, r_k = (vec_ref[i:i + 1, :] for i in range(5))
    w_log = -jax.nn.softplus(-(w0 + lo[:, 0:cw])) - 0.5
    a = jax.nn.sigmoid(a0 + lo[:, cw:2 * cw])
    kk = kc * k_k
    norm = jnp.sqrt(_dot_f32(kk * kk, seg_ref[...]))
    kk = kk / jnp.maximum(norm, 1e-12)
    kc = kc * (1.0 + (a - 1.0) * k_a)
    r_ref[...] = r
    w_ref[...] = jnp.exp(-jnp.exp(w_log))
    k_ref[...] = kc
    v_ref[...] = vc
    kk_ref[...] = kk
    b_ref[...] = -(kk * a)
    g_ref[...] = _dot(jax.nn.sigmoid(gd).astype(MXU_DTYPE), gup_ref[...])
    bonus_ref[...] = _dot_f32(r * kc * r_k, seg_ref[...]) * vc


def _rwkv_prep(pc, prev, mu, vecs, lora, g_up, seg, tm):
    n, width = prev.shape
    cw = vecs.shape[1]
    row = lambda w: pl.BlockSpec((tm, w), lambda i: (i, 0))
    full = lambda a: pl.BlockSpec(a.shape, lambda i: (0, 0))
    out = jax.ShapeDtypeStruct((n, cw), F32)
    return pl.pallas_call(
        _rwkv_prep_kernel,
        out_shape=(out,) * 8,
        grid=(n // tm,),
        in_specs=[row(width), row(width), full(mu), full(vecs), full(lora), full(g_up), full(seg)],
        out_specs=(row(cw),) * 8,
        compiler_params=_cparams("parallel"),
        name="rwkv_prep",
    )(pc, prev, mu, vecs, lora, g_up, seg)


def _segment_sum(x, seg):
    hi = x.astype(MXU_DTYPE)
    lo = (x - hi.astype(F32)).astype(MXU_DTYPE)
    return _dot(hi, seg) + _dot(lo, seg)


def _rwkv_scan_kernel(*refs, nb):
    ins, (seg_ref, s0_ref, y_ref, s_ref) = refs[:6 * nb], refs[6 * nb:]

    @pl.when(pl.program_id(1) == 0)
    def _():
        s_ref[...] = s0_ref[...]

    _, rows, width = s_ref.shape
    pairs = rows // RWKV_HEAD
    lane = lax.broadcasted_iota(I32, (nb * rows, width), 1)
    sub = lax.broadcasted_iota(I32, (nb * rows, width), 0)
    diag = (sub & (RWKV_HEAD - 1)) == (lane & (RWKV_HEAD - 1))
    own = (lax.broadcasted_iota(I32, (2, width), 0) == 0) == (lax.broadcasted_iota(I32, (2, width), 1) < RWKV_HEAD)
    seg = seg_ref[...]

    def group(g, carry):
        t0 = pl.multiple_of(g * SUBLANES, SUBLANES)
        r8, w8, k8, v8, kk8, nb8 = ([ins[6 * bb + q][pl.ds(t0, SUBLANES), :] for bb in range(nb)] for q in range(6))
        s = s_ref[...].reshape(nb * rows, width)
        for i in range(SUBLANES):
            def per_row(x8):
                return jnp.concatenate([jnp.broadcast_to(x[i:i + 1, p * width:(p + 1) * width], (RWKV_HEAD, width))
                                        for x in x8 for p in range(pairs)], axis=0)
            sa = _segment_sum(s * per_row(kk8), seg)
            vcol = _segment_sum(jnp.where(diag, per_row(v8), 0.0), seg)
            s = s * per_row(w8) + sa * per_row(nb8) + vcol * per_row(k8)
            sb = s.astype(MXU_DTYPE)
            for bb in range(nb):
                for p in range(pairs):
                    r2 = jnp.where(own, r8[bb][i:i + 1, p * width:(p + 1) * width], 0.0)
                    r0 = (bb * pairs + p) * RWKV_HEAD
                    y = _dot_nt(r2.astype(MXU_DTYPE), sb[r0:r0 + RWKV_HEAD])
                    y_ref[bb, pl.ds(t0 + i, 1), 2 * p:2 * p + 2, :] = y[None]
        s_ref[...] = s.reshape(nb, rows, width)
        return carry

    lax.fori_loop(0, ins[0].shape[0] // SUBLANES, group, 0)


def _rwkv_scan(ins, row_start, batch, t, s0_packed, seg2, nb):
    cw = ins[0].shape[1]
    heads = cw // RWKV_HEAD
    tb = min(t, 64)
    nt = t // tb
    rb0 = row_start // tb
    rows = [pl.BlockSpec((tb, cw), lambda b, i, bb=bb: (rb0 + (b * nb + bb) * nt + i, 0)) for bb in range(nb)]
    st = pl.BlockSpec((nb,) + s0_packed.shape[1:], lambda b, i: (b, 0, 0))
    return pl.pallas_call(
        functools.partial(_rwkv_scan_kernel, nb=nb),
        out_shape=(jax.ShapeDtypeStruct((batch, t, heads, RWKV_HEAD), F32),
                   jax.ShapeDtypeStruct(s0_packed.shape, F32)),
        grid=(batch // nb, nt),
        in_specs=[spec for spec in rows for _ in range(6)] + [pl.BlockSpec(seg2.shape, lambda b, i: (0, 0)), st],
        out_specs=(pl.BlockSpec((nb, tb, heads, RWKV_HEAD), lambda b, i: (b, i, 0, 0)), st),
        compiler_params=_cparams("parallel", "arbitrary"),
        name="rwkv_scan",
    )(*(list(ins) * nb), seg2, s0_packed)


def _rwkv_post_kernel(y_ref, bonus_ref, g_ref, ln_ref, seg_ref, o_ref):
    y = y_ref[...]
    avg = seg_ref[...] * (1.0 / RWKV_HEAD)
    d = y - _dot_f32(y, avg)
    var = _dot_f32(d * d, avg)
    yn = d * lax.rsqrt(var + GN_EPS) * ln_ref[0:1, :] + ln_ref[1:2, :]
    o_ref[...] = (yn + bonus_ref[...]) * g_ref[...]


def _rwkv_post(y, bonus, g, ln, seg, row_start, tm):
    n, cw = y.shape
    rb0 = row_start // tm
    row = pl.BlockSpec((tm, cw), lambda i: (i, 0))
    off = pl.BlockSpec((tm, cw), lambda i: (rb0 + i, 0))
    full = lambda a: pl.BlockSpec(a.shape, lambda i: (0, 0))
    return pl.pallas_call(
        _rwkv_post_kernel,
        out_shape=jax.ShapeDtypeStruct((n, cw), F32),
        grid=(n // tm,),
        in_specs=[row, off, off, full(ln), full(seg)],
        out_specs=row,
        compiler_params=_cparams("parallel"),
        name="rwkv_post",
    )(y, bonus, g, ln, seg)


MLA_HEADS = 8
MLA_NOPE = 64
MLA_ROPE = 32
MLA_SCALE = (MLA_NOPE + MLA_ROPE) ** -0.5


def _rope_tile(x, cos, sin):
    lane = lax.broadcasted_iota(I32, x.shape, 1)
    half = MLA_ROPE // 2
    rot = jnp.where(lane < half, pltpu.roll(x, LANES - half, 1), pltpu.roll(x, half, 1))
    return x * cos + rot * sin


def _mla_prep_kernel(qd_ref, ckv_ref, kr_ref, cos_ref, sin_ref, qn_ref, kvn_ref, wuq_ref, wuk_ref,
                     ql_ref, qr_ref, c_ref, krn_ref):
    cq = _rms(qd_ref[...], qn_ref[...])
    qh = _dot(cq.astype(MXU_DTYPE), wuq_ref[...])
    nope = MLA_HEADS * MLA_NOPE
    ql_ref[...] = _dot(qh[:, 0:nope].astype(MXU_DTYPE), wuk_ref[...]).astype(ql_ref.dtype)
    cos, sin = cos_ref[...], sin_ref[...]
    for h in range(MLA_HEADS):
        lanes = slice(nope + h * LANES, nope + (h + 1) * LANES)
        qr_ref[:, h * LANES:(h + 1) * LANES] = _rope_tile(qh[:, lanes], cos, sin).astype(qr_ref.dtype)
    c_ref[...] = _rms(ckv_ref[...], kvn_ref[...])
    krn_ref[...] = _rope_tile(kr_ref[...], cos, sin)


def _mla_prep(proj, col_blocks, row_start, nrows, cos, sin, q_norm, kv_norm, wuq, wuk, tm):
    qd0, ckv0, kr0 = col_blocks
    d_q, d_kv = q_norm.shape[1], kv_norm.shape[1]
    rb0 = row_start // tm
    nper = cos.shape[0] // tm
    full = lambda a: pl.BlockSpec(a.shape, lambda i: (0, 0))
    rows = lambda w: pl.BlockSpec((tm, w), lambda i: (i, 0))
    tab = pl.BlockSpec((tm, LANES), lambda i: (i % nper, 0))
    return pl.pallas_call(
        _mla_prep_kernel,
        out_shape=(jax.ShapeDtypeStruct((nrows, MLA_HEADS * d_kv), MXU_DTYPE),
                   jax.ShapeDtypeStruct((nrows, MLA_HEADS * LANES), MXU_DTYPE),
                   jax.ShapeDtypeStruct((nrows, d_kv), F32),
                   jax.ShapeDtypeStruct((nrows, LANES), F32)),
        grid=(nrows // tm,),
        in_specs=[pl.BlockSpec((tm, d_q), lambda i: (rb0 + i, qd0 * LANES // d_q)),
                  pl.BlockSpec((tm, d_kv), lambda i: (rb0 + i, ckv0 * LANES // d_kv)),
                  pl.BlockSpec((tm, LANES), lambda i: (rb0 + i, kr0)),
                  tab, tab, full(q_norm), full(kv_norm), full(wuq), full(wuk)],
        out_specs=(rows(MLA_HEADS * d_kv), rows(MLA_HEADS * LANES), rows(d_kv), rows(LANES)),
        compiler_params=_cparams("parallel"),
        name="mla_prep",
    )(proj, proj, proj, cos, sin, q_norm, kv_norm, wuq, wuk)


def _mla_prompt_kernel(ql_ref, qr_ref, c_ref, kr_ref, wuv_ref, o_ref, m_ref, l_ref, acc_ref, *, kb_size):
    tq = ql_ref.shape[0]
    d_kv = c_ref.shape[1]
    q0 = pl.program_id(1) * tq
    n_kb = (q0 + tq - 1) // kb_size + 1
    qpos = q0 + lax.broadcasted_iota(I32, (tq, kb_size), 0)
    lane = lax.broadcasted_iota(I32, (tq, kb_size), 1)
    m_ref[...] = jnp.full_like(m_ref, ONLINE_SOFTMAX_FLOOR)
    l_ref[...] = jnp.zeros_like(l_ref)
    acc_ref[...] = jnp.zeros_like(acc_ref)

    def block(kb, carry):
        k0 = pl.multiple_of(kb * kb_size, kb_size)
        c, kr = c_ref[pl.ds(k0, kb_size), :], kr_ref[pl.ds(k0, kb_size), :]
        scores = [(_dot_nt(ql_ref[:, h * d_kv:(h + 1) * d_kv], c)
                   + _dot_nt(qr_ref[:, h * LANES:(h + 1) * LANES], kr)) * MLA_SCALE for h in range(MLA_HEADS)]
        _online_softmax_heads(m_ref, l_ref, acc_ref, scores, k0 + lane <= qpos, lambda g, p: _dot(p, c))
        return carry

    lax.fori_loop(0, n_kb, block, 0)
    o_lat = (acc_ref[...] / l_ref[...]).astype(MXU_DTYPE)
    o_lat = jnp.concatenate([o_lat[h * tq:(h + 1) * tq] for h in range(MLA_HEADS)], axis=1)
    o_ref[...] = _dot(o_lat, wuv_ref[...])


def _mla_prompt(q_lat, q_rope, c, kr, wuv, batch, t, tq=128):
    d_kv = c.shape[1]
    nq = t // tq
    rows = MLA_HEADS * tq
    return pl.pallas_call(
        functools.partial(_mla_prompt_kernel, kb_size=math.gcd(t, KEY_BLOCK)),
        scratch_shapes=[pltpu.VMEM((rows, 1), F32), pltpu.VMEM((rows, 1), F32), pltpu.VMEM((rows, d_kv), F32)],
        out_shape=jax.ShapeDtypeStruct((batch * t, wuv.shape[1]), F32),
        grid=(batch, nq),
        in_specs=[pl.BlockSpec((tq, MLA_HEADS * d_kv), lambda b, i: (b * nq + i, 0)),
                  pl.BlockSpec((tq, MLA_HEADS * LANES), lambda b, i: (b * nq + i, 0)),
                  pl.BlockSpec((t, d_kv), lambda b, i: (b, 0)),
                  pl.BlockSpec((t, LANES), lambda b, i: (b, 0)),
                  pl.BlockSpec(wuv.shape, lambda b, i: (0, 0))],
        out_specs=pl.BlockSpec((tq, wuv.shape[1]), lambda b, i: (b * nq + i, 0)),
        compiler_params=_cparams("parallel", "arbitrary"),
        name="mla_prompt",
    )(q_lat, q_rope, c, kr, wuv)


def _mla_sample_kernel(pt_ref, ql_ref, qr_ref, cn_ref, krn_ref, *rest, pages, t_new):
    c_pages, kr_pages = rest[:pages], rest[pages:2 * pages]
    o_ref, m_ref, l_ref, acc_ref = rest[2 * pages:]
    step = pl.program_id(1)

    @pl.when(step == 0)
    def _():
        m_ref[...] = jnp.full_like(m_ref, ONLINE_SOFTMAX_FLOOR)
        l_ref[...] = jnp.zeros_like(l_ref)
        acc_ref[...] = jnp.zeros_like(acc_ref)

    ql, qr = ql_ref[...], qr_ref[...]
    cc = jnp.concatenate([r[...] for r in c_pages], axis=0).astype(MXU_DTYPE)
    kc = jnp.concatenate([r[...] for r in kr_pages], axis=1).astype(MXU_DTYPE)
    s = (_dot_nt(ql, cc) + _dot(qr[:, 0:MLA_ROPE], kc)) * MLA_SCALE
    _online_softmax_step(m_ref, l_ref, acc_ref, 0, s, jnp.full(s.shape, True), cc)

    @pl.when(step == pl.num_programs(1) - 1)
    def _():
        cn = cn_ref[...]
        s_new = (_dot_nt(ql, cn) + _dot_nt(qr, krn_ref[...])) * MLA_SCALE
        qpos = lax.broadcasted_iota(I32, s_new.shape, 0) % t_new
        kpos = lax.broadcasted_iota(I32, s_new.shape, 1)
        _online_softmax_step(m_ref, l_ref, acc_ref, 0, s_new, (kpos <= qpos) & (kpos < t_new), cn)
        o_ref[...] = acc_ref[0] / l_ref[0]


def _mla_sample(page_table, q_lat, q_rope, c_new_pad, kr_new_pad, pool_c, pool_kr, layer, t_new):
    b, n_pages = page_table.shape
    pages = math.gcd(n_pages, PAGES_PER_STEP)
    _, rows, d_kv = q_lat.shape
    one = lambda a: pl.BlockSpec((None,) + a.shape[1:], lambda n, c, pt: (n, 0, 0))
    grid_spec = pltpu.PrefetchScalarGridSpec(
        num_scalar_prefetch=1,
        grid=(b, n_pages // pages),
        in_specs=[one(q_lat), one(q_rope), one(c_new_pad), one(kr_new_pad)]
        + _page_specs((None, None, PAGE_SIZE, d_kv), layer, pages)
        + _page_specs((None, None, MLA_ROPE, PAGE_SIZE), layer, pages),
        out_specs=pl.BlockSpec((None, rows, d_kv), lambda n, c, pt: (n, 0, 0)),
        scratch_shapes=[pltpu.VMEM((1, rows, 1), F32), pltpu.VMEM((1, rows, 1), F32),
                        pltpu.VMEM((1, rows, d_kv), F32)],
    )
    return pl.pallas_call(
        functools.partial(_mla_sample_kernel, pages=pages, t_new=t_new),
        out_shape=jax.ShapeDtypeStruct((b, rows, d_kv), F32),
        grid_spec=grid_spec,
        compiler_params=_cparams("parallel", "arbitrary"),
        name="mla_sample",
    )(page_table, q_lat, q_rope, c_new_pad, kr_new_pad, *([pool_c] * pages), *([pool_kr] * pages))


ROW_BLOCK = 256
A_HEADS = 4
B_HEADS, B_KV_HEADS, B_DH = 8, 4, 64
IDX_HEADS, IDX_DIM = 8, 64
C_WIDTH = 512
RWKV_IN = 3 * C_WIDTH + 64 + 64 + 128
D_Q_RANK, D_KV_RANK = 384, 256
ODD_COLS = RWKV_IN + D_KV_RANK + 2 * LANES + D_Q_RANK
ODD_SLABS = ((RWKV_IN + D_KV_RANK + 2 * LANES) // LANES, RWKV_IN // LANES, (RWKV_IN + D_KV_RANK) // LANES)


def _pad_rows(a, rows=LANES):
    return jnp.pad(a, ((0, 0),) * (a.ndim - 2) + ((0, rows - a.shape[-2]), (0, 0)))


def _head_major(a, b, t, h):
    return jnp.transpose(a.reshape(b, t, h, -1), (0, 2, 1, 3))


def _token_major(a):
    b, h, t, w = a.shape
    return jnp.transpose(a, (0, 2, 1, 3)).reshape(b * t, h * w)


def _rope_tables(pos):
    half = MLA_ROPE // 2
    inv = ROPE_THETA ** (-jnp.arange(0, MLA_ROPE, 2, dtype=F32) / MLA_ROPE)
    ang = pos.astype(F32)[:, None] * inv[None, :]
    zeros = jnp.zeros((pos.shape[0], LANES - 2 * half), F32)
    cos, sin = jnp.cos(ang), jnp.sin(ang)
    return jnp.concatenate([cos, cos, zeros], axis=1), jnp.concatenate([-sin, sin, zeros], axis=1)


def kernel(x_prompt, x_sample, cache_dsa_k, cache_dsa_v, cache_dsa_idx, cache_mla_ckv, cache_mla_krope, state_hgrn, state_rwkv, state_shift, page_table, norm_mix, norm_ffn, norm_final, w_in_even, w_out_even, hgrn_lb, hgrn_norm, w_in_odd, w_out_odd, rwkv_mu, rwkv_w0, rwkv_w_up, rwkv_a0, rwkv_a_up, rwkv_g_up, rwkv_k_k, rwkv_k_a, rwkv_r_k, rwkv_ln_w, rwkv_ln_b, mla_q_norm, mla_w_uq, mla_kv_norm, mla_w_uk, mla_w_uv, peer_wq, peer_subkeys, peer_u, peer_v):
    bp, tp, d_model = x_prompt.shape
    bs, ts, _ = x_sample.shape
    n_p, n_s = bp * tp, bs * ts
    n_all = n_p + n_s
    n_pad = _round_up(n_all, PEER_TOKEN_BLOCK)
    tm_s = min(ROW_BLOCK, n_s)
    n_past = page_table.shape[1] * PAGE_SIZE
    depth = norm_mix.shape[0]
    md = MXU_DTYPE

    def all_rows(p, s):
        return jnp.concatenate([p, s, jnp.zeros((n_pad - n_all, p.shape[1]), p.dtype)], axis=0)

    x = all_rows(x_prompt.reshape(n_p, d_model), x_sample.reshape(n_s, d_model))

    lb_cum = jnp.cumsum(jax.nn.softmax(hgrn_lb.astype(F32), axis=0), axis=0)
    lower_bounds = lb_cum - lb_cum[:1]
    seg = jnp.kron(jnp.eye(C_WIDTH // RWKV_HEAD, dtype=F32), jnp.ones((RWKV_HEAD, RWKV_HEAD), F32))
    pool_dsa_k = jnp.transpose(cache_dsa_k, (0, 1, 3, 4, 2))
    pool_dsa_v = jnp.transpose(cache_dsa_v, (0, 1, 3, 4, 2))
    pool_dsa_idx = jnp.transpose(cache_dsa_idx, (0, 1, 3, 2))
    pool_mla_kr = jnp.transpose(cache_mla_krope, (0, 1, 3, 2))
    eye_h = jnp.eye(MLA_HEADS, dtype=F32)
    cos_p, sin_p = _rope_tables(jnp.arange(tp))
    cos_s, sin_s = (jnp.tile(a, (tm_s // ts, 1)) for a in _rope_tables(n_past + jnp.arange(ts)))

    outs = {k: [] for k in ("pk", "pv", "pi", "pc", "pr", "ph", "ps", "psh", "sk", "sv", "si", "sc", "sr", "sh", "ss", "ssh")}

    for l in range(depth):
        j = l // 2
        if l % 2 == 0:
            w_in = jnp.pad(w_in_even[j], ((0, 0), (0, _round_up(w_in_even.shape[2], LANES) - w_in_even.shape[2]))).astype(md)
            proj = _matmul(x, w_in, g=norm_mix[l], tm=ROW_BLOCK)
            aw = A_HEADS * LANES
            c_q, c_k, c_v = 4 * aw, 4 * aw + B_HEADS * B_DH, 4 * aw + (B_HEADS + B_KV_HEADS) * B_DH
            c_qi = c_v + B_KV_HEADS * B_DH
            c_ki = c_qi + IDX_HEADS * IDX_DIM
            c_wi = c_ki + IDX_DIM
            mixes = []
            for rows, b, t, s0_t, kk, kv, ki_key, kh in ((slice(0, n_p), bp, tp, None, "pk", "pv", "pi", "ph"),
                                                    (slice(n_p, n_all), bs, ts, state_hgrn[j], "sk", "sv", "si", "sh")):
                sample = s0_t is not None
                s0_t = jnp.swapaxes(s0_t, -1, -2).astype(F32) if sample else jnp.zeros((b, A_HEADS, LANES, LANES), F32)
                oa, s_a = _hgrn(proj, rows.start, b, t, A_HEADS, lower_bounds[j], hgrn_norm[j], s0_t)
                pr = proj[rows]
                qb, kb, vb = pr[:, c_q:c_k], pr[:, c_k:c_v], pr[:, c_v:c_qi]
                qi, ki, wi = pr[:, c_qi:c_ki], pr[:, c_ki:c_wi], pr[:, c_wi:c_wi + IDX_HEADS]
                q4 = _head_major(qb, b, t, B_HEADS).astype(md)
                k4 = _head_major(kb, b, t, B_KV_HEADS).astype(md)
                v4 = _head_major(vb, b, t, B_KV_HEADS).astype(md)
                ki3 = ki.reshape(b, t, IDX_DIM).astype(md)
                tq = t if sample else DSA_QUERY_BLOCK
                qi2 = jnp.transpose(qi.reshape(b, t // tq, tq, IDX_HEADS, IDX_DIM), (0, 1, 3, 2, 4))
                qi2 = qi2.reshape(b, t // tq, IDX_HEADS * tq, IDX_DIM).astype(md)
                wcol = jnp.transpose(wi.reshape(b, t // tq, tq, IDX_HEADS), (0, 1, 3, 2)).reshape(b, t // tq, IDX_HEADS * tq, 1)
                if sample:
                    topk = min(DSA_TOPK_MAX, (n_past + t) // 4)
                    mask = _dsa_select(page_table, qi2[:, 0], wcol[:, 0], _pad_rows(ki3), pool_dsa_idx, j, topk, t, IDX_HEADS)
                    group = B_HEADS // B_KV_HEADS
                    o4 = _dsa_attend(page_table, q4.reshape(b, B_KV_HEADS, group * t, B_DH), mask,
                                     _pad_rows(k4), _pad_rows(v4), pool_dsa_k, pool_dsa_v, j)
                    o4 = o4.reshape(b, B_HEADS, t, B_DH)
                else:
                    o4 = _dsa_prompt(qi2, wcol, ki3, q4, k4, v4, IDX_HEADS, tq)
                mixes.append(jnp.concatenate([oa, _token_major(o4)], axis=1))
                outs[kk].append(kb.reshape(b, t, B_KV_HEADS, B_DH))
                outs[kv].append(vb.reshape(b, t, B_KV_HEADS, B_DH))
                outs[ki_key].append(ki.reshape(b, t, IDX_DIM))
                outs[kh].append(jnp.swapaxes(s_a, -1, -2))
            x = _matmul(all_rows(*mixes), w_out_even[j].astype(md), res=x, tm=ROW_BLOCK)
        else:
            w = w_in_odd[j]
            c_qd, c_ckv = RWKV_IN, RWKV_IN + D_Q_RANK
            c_kr = c_ckv + D_KV_RANK
            w_in = jnp.concatenate([w[:, :RWKV_IN], w[:, c_ckv:c_kr], w[:, c_kr:c_kr + MLA_ROPE],
                                    jnp.zeros((d_model, 2 * LANES - MLA_ROPE), w.dtype), w[:, c_qd:c_ckv]], axis=1).astype(md)
            proj = _matmul(x, w_in, g=norm_mix[l], tm=ROW_BLOCK)
            pc_p = proj[:n_p, :RWKV_IN].reshape(bp, tp, RWKV_IN)
            pc_s = proj[n_p:n_all, :RWKV_IN].reshape(bs, ts, RWKV_IN)
            prev_p = jnp.concatenate([jnp.zeros((bp, 1, RWKV_IN), F32), pc_p[:, :-1]], axis=1)
            prev_s = jnp.concatenate([state_shift[j].astype(F32)[:, None], pc_s[:, :-1]], axis=1)
            prev = all_rows(prev_p.reshape(n_p, RWKV_IN), prev_s.reshape(n_s, RWKV_IN))
            zeros_l = jnp.zeros((rwkv_w_up.shape[1], C_WIDTH), F32)
            lora = jnp.concatenate([jnp.concatenate([rwkv_w_up[j], zeros_l], axis=1),
                                    jnp.concatenate([zeros_l, rwkv_a_up[j]], axis=1)], axis=0).astype(md)
            vecs = jnp.stack([rwkv_w0[j], rwkv_a0[j], rwkv_k_k[j], rwkv_k_a[j], rwkv_r_k[j].reshape(-1)]).astype(F32)
            prep = _rwkv_prep(proj, prev, rwkv_mu[j].reshape(1, -1).astype(F32), vecs, lora, rwkv_g_up[j].astype(md), seg, ROW_BLOCK)
            scan_in, g_all, bonus_all = prep[:6], prep[6], prep[7]
            ln = jnp.stack([rwkv_ln_w[j], rwkv_ln_b[j]]).astype(F32)

            wuq = mla_w_uq[j].reshape(D_Q_RANK, MLA_HEADS, MLA_NOPE + MLA_ROPE)
            wuq_rope = jnp.pad(wuq[:, :, MLA_NOPE:], ((0, 0), (0, 0), (0, LANES - MLA_ROPE))).reshape(D_Q_RANK, -1)
            wuq_p = jnp.concatenate([wuq[:, :, :MLA_NOPE].reshape(D_Q_RANK, -1), wuq_rope], axis=1).astype(md)
            wuk = jnp.einsum("chn,hg->hngc", mla_w_uk[j], eye_h).reshape(MLA_HEADS * MLA_NOPE, -1).astype(md)
            wuv = jnp.einsum("chv,hg->hcgv", mla_w_uv[j], eye_h).reshape(MLA_HEADS * D_KV_RANK, -1).astype(md)
            q_norm = mla_q_norm[j].reshape(1, -1).astype(F32)
            kv_norm = mla_kv_norm[j].reshape(1, -1).astype(F32)

            mixes = []
            for start, b, t, tm, cos, sin, sample in ((0, bp, tp, ROW_BLOCK, cos_p, sin_p, False),
                                                      (n_p, bs, ts, tm_s, cos_s, sin_s, True)):
                n = b * t
                if sample:
                    heads = C_WIDTH // RWKV_HEAD
                    s0 = state_rwkv[j].astype(F32).reshape(b, heads // 2, 2, RWKV_HEAD, RWKV_HEAD)
                    s0 = jnp.transpose(s0, (0, 1, 3, 2, 4)).reshape(b, heads // 2 * RWKV_HEAD, 2 * RWKV_HEAD)
                else:
                    s0 = jnp.zeros((b, C_WIDTH // LANES * RWKV_HEAD, LANES), F32)
                nb = math.gcd(b, RWKV_BATCH_BLOCK)
                y3, s_c = _rwkv_scan(scan_in, start, b, t, s0, seg[:LANES, :LANES].astype(md), nb)
                yc = _rwkv_post(y3.reshape(n, C_WIDTH), bonus_all, g_all, ln, seg, start, tm)
                s_c = jnp.transpose(s_c.reshape(b, -1, RWKV_HEAD, 2, RWKV_HEAD), (0, 1, 3, 2, 4))
                s_c = s_c.reshape(b, -1, RWKV_HEAD, RWKV_HEAD)

                q_lat, q_rope, c_new, kr_new = _mla_prep(proj, ODD_SLABS, start, n, cos, sin, q_norm, kv_norm, wuq_p, wuk, tm)
                if sample:
                    hq = lambda a: _head_major(a, b, t, MLA_HEADS).reshape(b, MLA_HEADS * t, -1)
                    o_lat = _mla_sample(page_table, hq(q_lat), hq(q_rope),
                                        _pad_rows(c_new.reshape(b, t, -1)).astype(md),
                                        _pad_rows(kr_new.reshape(b, t, -1)).astype(md),
                                        cache_mla_ckv, pool_mla_kr, j, t)
                    o_lat = _token_major(o_lat.reshape(b, MLA_HEADS, t, -1))
                    od = _matmul(o_lat, wuv, tm=tm)
                else:
                    od = _mla_prompt(q_lat, q_rope, c_new.astype(md), kr_new.astype(md), wuv, b, t)
                mixes.append(jnp.concatenate([yc, od], axis=1))
                pre = "s" if sample else "p"
                outs[pre + "c"].append(c_new.reshape(b, t, D_KV_RANK))
                outs[pre + "r"].append(kr_new[:, :MLA_ROPE].reshape(b, t, MLA_ROPE))
                outs[pre + "s"].append(s_c)
                outs[pre + "sh"].append((pc_s if sample else pc_p)[:, -1])
            x = _matmul(all_rows(*mixes), w_out_odd[j].astype(md), res=x, tm=ROW_BLOCK)
        x = _peer_ffn(x, norm_ffn[l], _peer_prepare(peer_wq[l], peer_subkeys[l], peer_u[l], peer_v[l]))

    y = _rmsnorm(x, norm_final)
    y_prompt = y[:n_p].reshape(bp, tp, d_model)
    y_sample = y[n_p:n_all].reshape(bs, ts, d_model)
    st = lambda k: jnp.stack(outs[k])
    return (y_prompt, y_sample,
            st("pk"), st("pv"), st("pi"), st("pc"), st("pr"), st("ph"), st("ps"), st("psh"),
            st("sk"), st("sv"), st("si"), st("sc"), st("sr"), st("sh"), st("ss"), st("ssh"))
```

```python
import functools
import math

import jax
import jax.numpy as jnp
from jax import lax
from jax.experimental import pallas as pl
from jax.experimental.pallas import tpu as pltpu

F32 = jnp.float32
I32 = jnp.int32
MXU_DTYPE = jnp.bfloat16
GATE_DTYPE = jnp.bfloat16
LANES = 128
SUBLANES = 8
VMEM_LIMIT_BYTES = 56 * 1024 * 1024
NEG_INF = float("-inf")

RMS_EPS = 1e-6
PAGE_SIZE = 128
ROPE_THETA = 10000.0
PEER_HEADS = 8
PEER_NKEYS = 128
PEER_TOPK = 16
PEER_TOKEN_BLOCK = 512
PEER_GROUP = 16


def _cparams(*sem):
    return pltpu.CompilerParams(dimension_semantics=sem, vmem_limit_bytes=VMEM_LIMIT_BYTES)


def _round_up(n, m):
    return -(-n // m) * m


def _rms(x, g):
    return x * lax.rsqrt(jnp.mean(x * x, axis=-1, keepdims=True) + RMS_EPS) * g


def _dot(a, b):
    return jnp.dot(a, b, preferred_element_type=F32)


def _dot_nt(a, b):
    return lax.dot_general(a, b, (((1,), (1,)), ((), ())), preferred_element_type=F32)


def _matmul_kernel(*refs, norm, residual):
    it = iter(refs)
    a_ref = next(it)
    g_ref = next(it) if norm else None
    w_ref = next(it)
    r_ref = next(it) if residual else None
    o_ref = next(it)
    a = a_ref[...]
    if norm:
        a = _rms(a, g_ref[...])
    acc = _dot(a.astype(w_ref.dtype), w_ref[...])
    if residual:
        acc = acc + r_ref[...]
    o_ref[...] = acc


def _matmul(a, w, g=None, res=None, tm=256):
    n, k = a.shape
    m = w.shape[1]
    ins, specs = [a], [pl.BlockSpec((tm, k), lambda i: (i, 0))]
    if g is not None:
        ins.append(g.reshape(1, k).astype(F32))
        specs.append(pl.BlockSpec((1, k), lambda i: (0, 0)))
    ins.append(w)
    specs.append(pl.BlockSpec((k, m), lambda i: (0, 0)))
    if res is not None:
        ins.append(res)
        specs.append(pl.BlockSpec((tm, m), lambda i: (i, 0)))
    return pl.pallas_call(
        functools.partial(_matmul_kernel, norm=g is not None, residual=res is not None),
        out_shape=jax.ShapeDtypeStruct((n, m), F32),
        grid=(n // tm,),
        in_specs=specs,
        out_specs=pl.BlockSpec((tm, m), lambda i: (i, 0)),
        compiler_params=_cparams("parallel"),
        name="proj_matmul",
    )(*ins)


def _rmsnorm_kernel(x_ref, g_ref, o_ref):
    o_ref[...] = _rms(x_ref[...], g_ref[...])


def _rmsnorm(x, g, tm=512):
    n, d = x.shape
    return pl.pallas_call(
        _rmsnorm_kernel,
        out_shape=jax.ShapeDtypeStruct((n, d), F32),
        grid=(n // tm,),
        in_specs=[pl.BlockSpec((tm, d), lambda i: (i, 0)), pl.BlockSpec((1, d), lambda i: (0, 0))],
        out_specs=pl.BlockSpec((tm, d), lambda i: (i, 0)),
        compiler_params=_cparams("parallel"),
        name="final_rmsnorm",
    )(x, g.reshape(1, d).astype(F32))


def _peer_fold_kernel(sub_ref, wq_ref, o_ref):
    o_ref[...] = _dot_nt(sub_ref[...], wq_ref[...])


def _peer_fold(subkeys, wq):
    d_model = wq.shape[0]
    half = subkeys.shape[2]
    return pl.pallas_call(
        _peer_fold_kernel,
        out_shape=jax.ShapeDtypeStruct((2, PEER_HEADS, PEER_NKEYS, d_model), F32),
        grid=(2, PEER_HEADS),
        in_specs=[pl.BlockSpec((None, PEER_NKEYS, half), lambda p, h: (p, 0, 0)),
                  pl.BlockSpec((d_model, half), lambda p, h: (0, h * 2 + p))],
        out_specs=pl.BlockSpec((None, None, PEER_NKEYS, d_model), lambda p, h: (p, h, 0, 0)),
        compiler_params=_cparams("parallel", "parallel"),
        name="peer_fold",
    )(subkeys.astype(MXU_DTYPE), wq.astype(MXU_DTYPE))


def _peer_scores_kernel(x_ref, g_ref, m_ref, s_ref, h_ref):
    hb = _rms(x_ref[...], g_ref[...]).astype(MXU_DTYPE)
    s_ref[...] = _dot_nt(m_ref[...], hb)
    h_ref[...] = hb.T


def _peer_scores(x, g, mcat, tm=PEER_TOKEN_BLOCK):
    n, d = x.shape
    rows = mcat.shape[0]
    return pl.pallas_call(
        _peer_scores_kernel,
        out_shape=(jax.ShapeDtypeStruct((rows, n), F32), jax.ShapeDtypeStruct((d, n), MXU_DTYPE)),
        grid=(n // tm,),
        in_specs=[pl.BlockSpec((tm, d), lambda i: (i, 0)),
                  pl.BlockSpec((1, d), lambda i: (0, 0)),
                  pl.BlockSpec((rows, d), lambda i: (0, 0))],
        out_specs=(pl.BlockSpec((rows, tm), lambda i: (0, i)), pl.BlockSpec((d, tm), lambda i: (0, i))),
        compiler_params=_cparams("parallel"),
        name="peer_scores",
    )(x, g.reshape(1, d).astype(F32), mcat)


def _bitonic_merge_desc(v):
    n = len(v)
    if n == 1:
        return v
    half = n // 2
    hi = [jnp.maximum(v[i], v[i + half]) for i in range(half)]
    lo = [jnp.minimum(v[i], v[i + half]) for i in range(half)]
    return _bitonic_merge_desc(hi) + _bitonic_merge_desc(lo)


def _sort_desc(v):
    n = len(v)
    if n == 1:
        return v
    return _bitonic_merge_desc(_sort_desc(v[:n // 2]) + _sort_desc(v[n // 2:])[::-1])


def _merge_top(x, y):
    n = len(x)
    return _bitonic_merge_desc([jnp.maximum(x[i], y[n - 1 - i]) for i in range(n)])


def _peer_topk_kernel(s_ref, c1_ref, e1_ref, r2_ref, e2_ref):
    hk = PEER_HEADS * PEER_NKEYS
    tn = s_ref.shape[1]

    def top_values(base):
        runs = []
        for r0 in range(0, PEER_NKEYS, PEER_TOPK):
            rows = [s_ref[base + (r0 + i) * PEER_HEADS:base + (r0 + i + 1) * PEER_HEADS, :] for i in range(PEER_TOPK)]
            runs.append(_sort_desc(rows))
        while len(runs) > 1:
            runs = [_merge_top(runs[i], runs[i + 1]) for i in range(0, len(runs), 2)]
        return runs[0]

    a = top_values(0)
    b = top_values(hk)
    neg = jnp.full((PEER_HEADS, tn), NEG_INF, F32)
    lists = [[a[k] + b[l] if (k + 1) * (l + 1) <= PEER_TOPK else neg for l in range(PEER_TOPK)] for k in range(PEER_TOPK)]
    while len(lists) > 1:
        lists = [_merge_top(lists[i], lists[i + 1]) for i in range(0, len(lists), 2)]
    best = lists[0]
    tau = best[-1]
    z = jnp.zeros_like(tau)
    for c in best:
        z = z + jnp.exp(c - best[0])
    inv_z = 1.0 / z

    twice = lambda x: jnp.concatenate([x, x], axis=0)
    tau2, a0_2, inv_z2, b2 = twice(tau), twice(a[0]), twice(inv_z), [twice(x) for x in b]

    def first_half(i, carry):
        rows = pl.ds(pl.multiple_of(i * 2 * PEER_HEADS, 2 * PEER_HEADS), 2 * PEER_HEADS)
        x = s_ref[rows, :]
        cnt = jnp.zeros_like(x)
        for bl in b2:
            cnt = cnt + jnp.where(x + bl >= tau2, 1.0, 0.0)
        c1_ref[rows, :] = cnt.astype(c1_ref.dtype)
        e1_ref[rows, :] = (jnp.exp(x - a0_2) * inv_z2).astype(e1_ref.dtype)
        return carry

    lax.fori_loop(0, PEER_NKEYS // 2, first_half, 0)
    for h in range(PEER_HEADS):
        x = s_ref[pl.ds(2 * hk + h * PEER_NKEYS, PEER_NKEYS), :]
        rank = jnp.zeros_like(x)
        for bl in b:
            rank = rank + jnp.where(bl[h:h + 1, :] > x, 1.0, 0.0)
        rows = slice(h * PEER_NKEYS, (h + 1) * PEER_NKEYS)
        r2_ref[rows, :] = rank.astype(r2_ref.dtype)
        e2_ref[rows, :] = jnp.exp(x - b[0][h:h + 1, :]).astype(e2_ref.dtype)


def _peer_topk(s_t, tn=PEER_TOKEN_BLOCK):
    rows, n = s_t.shape
    hk = PEER_HEADS * PEER_NKEYS
    out = jax.ShapeDtypeStruct((hk, n), GATE_DTYPE)
    spec = pl.BlockSpec((hk, tn), lambda i: (0, i))
    return pl.pallas_call(
        _peer_topk_kernel,
        out_shape=(out,) * 4,
        grid=(n // tn,),
        in_specs=[pl.BlockSpec((rows, tn), lambda i: (0, i))],
        out_specs=(spec,) * 4,
        compiler_params=_cparams("parallel"),
        name="peer_topk",
    )(s_t)


def _gelu(x):
    return 0.5 * x * (1.0 + lax.erf(x * (1.0 / math.sqrt(2.0))))


PEER_QUAD = 4 * PEER_NKEYS
GATE_TILE = (64, 256)


def _peer_experts_kernel(h_ref, c1_ref, e1_ref, r2_ref, e2_ref, u_ref, v_ref, o_ref, *a_refs):
    @pl.when(pl.program_id(1) == 0)
    def _():
        o_ref[...] = jnp.zeros_like(o_ref)

    hb = h_ref[...]
    tn = hb.shape[1]
    te, tt = GATE_TILE
    n_quads = u_ref.shape[0] // PEER_QUAD
    scores = lambda quad: _dot(u_ref[quad * PEER_QUAD:(quad + 1) * PEER_QUAD, :], hb)
    st_next = scores(0)
    for quad in range(n_quads):
        st, st_next = st_next, (scores(quad + 1) if quad + 1 < n_quads else None)
        a_ref = a_refs[quad % 2]
        for sub in range(PEER_QUAD // PEER_NKEYS):
            i1 = quad * (PEER_QUAD // PEER_NKEYS) + sub
            c1 = c1_ref[i1 * PEER_HEADS:(i1 + 1) * PEER_HEADS, :]
            e1 = e1_ref[i1 * PEER_HEADS:(i1 + 1) * PEER_HEADS, :]
            for r0 in range(0, PEER_NKEYS, te):
                for l0 in range(0, tn, tt):
                    lanes = slice(l0, l0 + tt)
                    gate = None
                    for h in range(PEER_HEADS):
                        rows = slice(h * PEER_NKEYS + r0, h * PEER_NKEYS + r0 + te)
                        sel = r2_ref[rows, lanes] < c1[h:h + 1, lanes]
                        term = jnp.where(sel, e2_ref[rows, lanes], 0.0) * e1[h:h + 1, lanes]
                        gate = term if gate is None else gate + term
                    rows = slice(sub * PEER_NKEYS + r0, sub * PEER_NKEYS + r0 + te)
                    a_ref[rows, lanes] = (_gelu(st[rows, lanes]).astype(gate.dtype) * gate).astype(a_ref.dtype)
        o_ref[...] += _dot(v_ref[quad], a_ref[...])


def _peer_experts(h_t, c1, e1, r2, e2, u2, v3, tn=PEER_TOKEN_BLOCK):
    d, n = h_t.shape
    hk = PEER_HEADS * PEER_NKEYS
    ge = PEER_GROUP * PEER_NKEYS
    tok = lambda rows: pl.BlockSpec((rows, tn), lambda i, g: (0, i))
    grp = pl.BlockSpec((PEER_GROUP * PEER_HEADS, tn), lambda i, g: (g, i))
    return pl.pallas_call(
        _peer_experts_kernel,
        out_shape=jax.ShapeDtypeStruct((d, n), F32),
        grid=(n // tn, u2.shape[0] // ge),
        in_specs=[tok(d), grp, grp, tok(hk), tok(hk),
                  pl.BlockSpec((ge, d), lambda i, g: (g, 0)),
                  pl.BlockSpec((ge // PEER_QUAD, d, PEER_QUAD), lambda i, g: (g, 0, 0))],
        out_specs=tok(d),
        scratch_shapes=[pltpu.VMEM((PEER_QUAD, tn), MXU_DTYPE)] * 2,
        compiler_params=_cparams("parallel", "arbitrary"),
        name="peer_experts",
    )(h_t, c1, e1, r2, e2, u2, v3)


def _peer_prepare(wq, subkeys, u, v):
    d = wq.shape[0]
    hk = PEER_HEADS * PEER_NKEYS
    mf = _peer_fold(subkeys, wq)
    inter = jnp.transpose(mf, (0, 2, 1, 3)).reshape(2 * hk, d)
    mcat = jnp.concatenate([inter, mf[1].reshape(hk, d)], axis=0).astype(MXU_DTYPE)
    v3 = jnp.transpose(v.astype(MXU_DTYPE).reshape(-1, PEER_QUAD, d), (0, 2, 1))
    return mcat, u.astype(MXU_DTYPE), v3


def _peer_ffn(x, g, prep):
    mcat, u2, v3 = prep
    s_t, h_t = _peer_scores(x, g, mcat)
    y_t = _peer_experts(h_t, *_peer_topk(s_t), u2, v3)
    return x + y_t.T


HGRN_CHUNK = 64
HGRN_SUB = 16
MASKED_EXPONENT = -1e30


def _cumsum_rows(x):
    rows = x.shape[0]
    row = lax.broadcasted_iota(I32, x.shape, 0)
    d = 1
    while d < rows:
        x = x + jnp.where(row >= d, pltpu.roll(x, d, 0), 0.0)
        d *= 2
    return x


def _hgrn_chunk(q, k, v, g, s_t, sub):
    c = q.shape[0]
    cum = _cumsum_rows(g)
    o = _dot_nt((q * jnp.exp(cum)).astype(MXU_DTYPE), s_t.astype(MXU_DTYPE))
    outs = []
    for blk in range(c // sub):
        r0 = blk * sub
        q_b, cum_b, k_b, v_b = q[r0:r0 + sub], cum[r0:r0 + sub], k[r0:r0 + sub], v[r0:r0 + sub]
        o_b = o[r0:r0 + sub]
        if blk > 0:
            base = cum[r0 - 1:r0]
            qs = q_b * jnp.exp(cum_b - base)
            ks = k[0:r0] * jnp.exp(base - cum[0:r0])
            att = _dot_nt(qs.astype(MXU_DTYPE), ks.astype(MXU_DTYPE))
            o_b = o_b + _dot(att.astype(MXU_DTYPE), v[0:r0].astype(MXU_DTYPE))
        row = lax.broadcasted_iota(I32, (sub, q.shape[1]), 0)
        for s in range(sub):
            dec = jnp.exp(jnp.where(row >= s, cum_b - cum_b[s:s + 1], MASKED_EXPONENT))
            att = jnp.sum(q_b * k_b[s:s + 1] * dec, axis=1, keepdims=True)
            o_b = o_b + att * v_b[s:s + 1]
        outs.append(o_b)
    o = outs[0] if len(outs) == 1 else jnp.concatenate(outs, axis=0)
    last = cum[c - 1:c]
    kd = k * jnp.exp(last - cum)
    upd = lax.dot_general(v.astype(MXU_DTYPE), kd.astype(MXU_DTYPE), (((0,), (0,)), ((), ())),
                          preferred_element_type=F32)
    return o, s_t * jnp.exp(last) + upd


def _hgrn_kernel(q_ref, f_ref, i_ref, g_ref, lb_ref, gain_ref, s0_ref, o_ref, s_ref, *, chunk, sub):
    @pl.when(pl.program_id(2) == 0)
    def _():
        s_ref[...] = s0_ref[...]

    lb = lb_ref[...]
    s_t = s_ref[...]
    for c0 in range(0, q_ref.shape[0], chunk):
        rows = slice(c0, c0 + chunk)
        f = lb + (1.0 - lb) * jax.nn.sigmoid(f_ref[rows, :])
        o, s_t = _hgrn_chunk(jax.nn.silu(q_ref[rows, :]), 1.0 - f, i_ref[rows, :], jnp.log(f), s_t, sub)
        o_ref[rows, :] = _rms(o, gain_ref[...]) * jax.nn.silu(g_ref[rows, :])
    s_ref[...] = s_t


def _hgrn(proj, row_start, batch, t, heads, lb, gain, s0_t):
    chunk = math.gcd(t, HGRN_CHUNK)
    sub = min(HGRN_SUB, chunk)
    tc = min(t, 4 * chunk)
    nt = t // tc
    rb0 = row_start // tc
    dk = LANES

    def slab(k):
        return pl.BlockSpec((tc, dk), lambda b, h, i, k=k: (rb0 + b * nt + i, k * heads + h))

    vec = pl.BlockSpec((1, dk), lambda b, h, i: (0, h))
    st = pl.BlockSpec((None, None, dk, dk), lambda b, h, i: (b, h, 0, 0))
    return pl.pallas_call(
        functools.partial(_hgrn_kernel, chunk=chunk, sub=sub),
        out_shape=(jax.ShapeDtypeStruct((batch * t, heads * dk), F32),
                   jax.ShapeDtypeStruct((batch, heads, dk, dk), F32)),
        grid=(batch, heads, nt),
        in_specs=[slab(0), slab(1), slab(2), slab(3), vec, vec, st],
        out_specs=(pl.BlockSpec((tc, dk), lambda b, h, i: (b * nt + i, h)), st),
        compiler_params=_cparams("parallel", "parallel", "arbitrary"),
        name="hgrn2",
    )(proj, proj, proj, proj, lb.reshape(1, -1), gain.reshape(1, -1), s0_t)


DSA_TOPK_MAX = 256
DSA_QUERY_BLOCK = 128
INT32_MIN = -2 ** 31


def _count(m):
    return jnp.sum(jnp.where(m, 1.0, 0.0), axis=1, keepdims=True)


def _float_key(x):
    u = lax.bitcast_convert_type(x, I32)
    return u ^ ((u >> 31) & I32(0x7FFFFFFF))


KEY_NEG_INF = INT32_MIN + 0x007FFFFF
INT32_MAX = 2 ** 31 - 1


def _topk_thresholds(count, k, idx_bits, rows):
    kf = float(k)
    v = jnp.where(count(lambda key, idx: key >= 0) >= kf, I32(0), I32(INT32_MIN))

    def value_bit(it, v):
        t = v | (I32(1) << (I32(30) - it))
        return jnp.where(count(lambda key, idx: key >= t) >= kf, t, v)

    v = lax.fori_loop(0, 31, value_bit, v)
    at_least = count(lambda key, idx: key >= v)

    def break_ties():
        need = kf - count(lambda key, idx: key > v)

        def index_bit(it, j):
            t = j | (I32(1) << (I32(idx_bits - 1) - it))
            return jnp.where(count(lambda key, idx: (key == v) & (idx < t)) < need, t, j)

        return lax.fori_loop(0, idx_bits, index_bit, jnp.zeros((rows, 1), I32))

    j = lax.cond(jnp.max(at_least) > kf, break_ties, lambda: jnp.full((rows, 1), INT32_MAX, I32))
    return v, j


def _selected(key, idx, v, j):
    return ((key > v) | ((key == v) & (idx <= j))) & (key > KEY_NEG_INF)


def _topk_mask(scores, k, idx_bits):
    key = _float_key(scores)
    idx = lax.broadcasted_iota(I32, scores.shape, 1)
    v, j = _topk_thresholds(lambda pred: _count(pred(key, idx)), k, idx_bits, scores.shape[0])
    return _selected(key, idx, v, j)


def _index_scores(qi, w, ki, heads, keys_transposed=False):
    d = qi.shape[1]
    dots = (_dot(qi, ki) if keys_transposed else _dot_nt(qi, ki)) * (d ** -0.5)
    terms = jnp.maximum(dots, 0.0) * (w * (heads ** -0.5))
    r = qi.shape[0] // heads
    acc = terms[0:r]
    for h in range(1, heads):
        acc = acc + terms[h * r:(h + 1) * r]
    return acc


def _masked_softmax_pv(s, mask, v):
    s = jnp.where(mask, s, NEG_INF)
    p = jnp.exp(s - jnp.max(s, axis=1, keepdims=True))
    return _dot(p.astype(MXU_DTYPE), v) / jnp.sum(p, axis=1, keepdims=True)


PAGES_PER_STEP = 32
ONLINE_SOFTMAX_FLOOR = -1e30
KEY_BLOCK = 512


def _dsa_prompt_kernel(qi_ref, w_ref, ki_ref, q_ref, k_ref, v_ref, o_ref, *, topk, hi, kb_size):
    heads, tq, dh = q_ref.shape
    kv_heads = k_ref.shape[0]
    t = ki_ref.shape[0]
    q0 = pl.program_id(1) * tq
    n_kb = (q0 + tq - 1) // kb_size + 1

    def attend(s_keys):
        qpos = q0 + lax.broadcasted_iota(I32, (tq, s_keys), 0)
        kpos = lax.broadcasted_iota(I32, (tq, s_keys), 1)
        scores = _index_scores(qi_ref[...], w_ref[...], ki_ref[0:s_keys, :], hi)
        mask = _topk_mask(jnp.where(kpos <= qpos, scores, NEG_INF), topk, max(1, (s_keys - 1).bit_length()))
        for h in range(heads):
            g = h // (heads // kv_heads)
            s = _dot_nt(q_ref[h], k_ref[g, 0:s_keys, :]) * (dh ** -0.5)
            o_ref[h] = _masked_softmax_pv(s, mask, v_ref[g, 0:s_keys, :])

    for n in range(1, t // kb_size + 1):
        pl.when(n_kb == n)(functools.partial(attend, n * kb_size))


def _dsa_prompt(qi2, wcol, ki, q4, k4, v4, hi, tq=128):
    b, t, d = ki.shape
    _, h, _, dh = q4.shape
    hkv = k4.shape[1]
    topk = min(DSA_TOPK_MAX, t // 4)
    kb = math.gcd(t, KEY_BLOCK)
    return pl.pallas_call(
        functools.partial(_dsa_prompt_kernel, topk=topk, hi=hi, kb_size=kb),
        out_shape=jax.ShapeDtypeStruct((b, h, t, dh), F32),
        grid=(b, t // tq),
        in_specs=[pl.BlockSpec((None, None, hi * tq, d), lambda n, i: (n, i, 0, 0)),
                  pl.BlockSpec((None, None, hi * tq, 1), lambda n, i: (n, i, 0, 0)),
                  pl.BlockSpec((None, t, d), lambda n, i: (n, 0, 0)),
                  pl.BlockSpec((None, h, tq, dh), lambda n, i: (n, 0, i, 0)),
                  pl.BlockSpec((None, hkv, t, dh), lambda n, i: (n, 0, 0, 0)),
                  pl.BlockSpec((None, hkv, t, dh), lambda n, i: (n, 0, 0, 0))],
        out_specs=pl.BlockSpec((None, h, tq, dh), lambda n, i: (n, 0, i, 0)),
        compiler_params=_cparams("parallel", "arbitrary"),
        name="dsa_prompt",
    )(qi2, wcol, ki, q4, k4, v4)


def _page_specs(block, layer, pages):
    zeros = (0,) * (len(block) - 2)
    return [pl.BlockSpec(block, lambda b, c, pt, r=r: (layer, pt[b, c * pages + r]) + zeros) for r in range(pages)]


def _dsa_select_kernel(pt_ref, qi_ref, w_ref, kn_ref, *rest, pages, topk, t_new, hi):
    page_refs, (mask_ref, sc_ref) = rest[:pages], rest[pages:]
    c = pl.program_id(1)
    nc = pl.num_programs(1)
    kc = jnp.concatenate([r[...] for r in page_refs], axis=1).astype(MXU_DTYPE)
    sc_ref[c] = _index_scores(qi_ref[...], w_ref[...], kc, hi, keys_transposed=True)

    @pl.when(c == nc - 1)
    def _():
        n_chunks, rows, ch = sc_ref.shape
        new = _index_scores(qi_ref[...], w_ref[...], kn_ref[...], hi)
        qpos = lax.broadcasted_iota(I32, new.shape, 0)
        kpos = lax.broadcasted_iota(I32, new.shape, 1)
        new = jnp.where((kpos <= qpos) & (kpos < t_new), new, NEG_INF)
        pieces = [sc_ref[i] for i in range(n_chunks)] + [new]
        if ch > new.shape[1]:
            pieces.append(jnp.full((rows, ch - new.shape[1]), NEG_INF, F32))
        scores = jnp.concatenate(pieces, axis=1)
        mask = _topk_mask(scores, topk, scores.shape[1].bit_length())
        for i in range(n_chunks + 1):
            mask_ref[i] = jnp.where(mask[:, i * ch:(i + 1) * ch], 1.0, 0.0)


def _dsa_select(page_table, qi2, wcol, ki_new_pad, pool_idx, layer, topk, t_new, hi):
    b, n_pages = page_table.shape
    pages = math.gcd(n_pages, PAGES_PER_STEP)
    nc = n_pages // pages
    ch = pages * PAGE_SIZE
    _, rows, d = qi2.shape
    t = rows // hi
    grid_spec = pltpu.PrefetchScalarGridSpec(
        num_scalar_prefetch=1,
        grid=(b, nc),
        in_specs=[pl.BlockSpec((None, rows, d), lambda n, c, pt: (n, 0, 0)),
                  pl.BlockSpec((None, rows, 1), lambda n, c, pt: (n, 0, 0)),
                  pl.BlockSpec((None, LANES, d), lambda n, c, pt: (n, 0, 0))]
        + _page_specs((None, None, d, PAGE_SIZE), layer, pages),
        out_specs=pl.BlockSpec((None, nc + 1, t, ch), lambda n, c, pt: (n, 0, 0, 0)),
        scratch_shapes=[pltpu.VMEM((nc, t, ch), F32)],
    )
    return pl.pallas_call(
        functools.partial(_dsa_select_kernel, pages=pages, topk=topk, t_new=t_new, hi=hi),
        out_shape=jax.ShapeDtypeStruct((b, nc + 1, t, ch), F32),
        grid_spec=grid_spec,
        compiler_params=_cparams("parallel", "arbitrary"),
        name="dsa_sample_select",
    )(page_table, qi2, wcol, ki_new_pad, *([pool_idx] * pages))


def _online_softmax_step(m_ref, l_ref, acc_ref, j, s, mask, v, values_transposed=False):
    m_old = m_ref[j]
    m_new = jnp.maximum(m_old, jnp.max(jnp.where(mask, s, ONLINE_SOFTMAX_FLOOR), axis=1, keepdims=True))
    alpha = jnp.exp(m_old - m_new)
    p = jnp.where(mask, jnp.exp(s - m_new), 0.0)
    l_ref[j] = alpha * l_ref[j] + jnp.sum(p, axis=1, keepdims=True)
    pv = _dot_nt(p.astype(MXU_DTYPE), v) if values_transposed else _dot(p.astype(MXU_DTYPE), v)
    acc_ref[j] = alpha * acc_ref[j] + pv
    m_ref[j] = m_new


def _online_softmax_heads(m_ref, l_ref, acc_ref, scores, mask, pv_fn):
    rows = scores[0].shape[0]
    s = jnp.concatenate(scores, axis=0)
    mask = jnp.concatenate([mask] * (s.shape[0] // mask.shape[0]), axis=0)
    m_old = m_ref[...]
    m_new = jnp.maximum(m_old, jnp.max(jnp.where(mask, s, ONLINE_SOFTMAX_FLOOR), axis=1, keepdims=True))
    alpha = jnp.exp(m_old - m_new)
    p = jnp.where(mask, jnp.exp(s - m_new), 0.0)
    l_ref[...] = alpha * l_ref[...] + jnp.sum(p, axis=1, keepdims=True)
    pb = p.astype(MXU_DTYPE)
    pv = jnp.concatenate([pv_fn(g, pb[g * rows:(g + 1) * rows]) for g in range(len(scores))], axis=0)
    acc_ref[...] = alpha * acc_ref[...] + pv
    m_ref[...] = m_new


def _dsa_attend_kernel(pt_ref, q_ref, mc_ref, mn_ref, kn_ref, vn_ref, *rest, pages):
    k_pages, v_pages = rest[:pages], rest[pages:2 * pages]
    o_ref, m_ref, l_ref, acc_ref = rest[2 * pages:]
    c = pl.program_id(1)
    kv_heads, rows, dh = q_ref.shape
    scale = dh ** -0.5

    @pl.when(c == 0)
    def _():
        m_ref[...] = jnp.full_like(m_ref, ONLINE_SOFTMAX_FLOOR)
        l_ref[...] = jnp.zeros_like(l_ref)
        acc_ref[...] = jnp.zeros_like(acc_ref)

    def keys_t(refs, g):
        return jnp.concatenate([r[g] for r in refs], axis=1).astype(MXU_DTYPE)

    scores = [_dot(q_ref[g], keys_t(k_pages, g)) * scale for g in range(kv_heads)]
    _online_softmax_heads(m_ref, l_ref, acc_ref, scores, mc_ref[...] > 0.0,
                          lambda g, p: _dot_nt(p, keys_t(v_pages, g)))

    @pl.when(c == pl.num_programs(1) - 1)
    def _():
        new = [_dot_nt(q_ref[g], kn_ref[g]) * scale for g in range(kv_heads)]
        _online_softmax_heads(m_ref, l_ref, acc_ref, new, mn_ref[:, 0:LANES] > 0.0, lambda g, p: _dot(p, vn_ref[g]))
        o_ref[...] = (acc_ref[...] / l_ref[...]).reshape(o_ref.shape)


def _dsa_attend(page_table, q4, mask, k_new_pad, v_new_pad, pool_k, pool_v, layer):
    b, n_pages = page_table.shape
    pages = math.gcd(n_pages, PAGES_PER_STEP)
    nc = n_pages // pages
    _, hkv, rows, dh = q4.shape
    _, _, t, ch = mask.shape
    grid_spec = pltpu.PrefetchScalarGridSpec(
        num_scalar_prefetch=1,
        grid=(b, nc),
        in_specs=[pl.BlockSpec((None, hkv, rows, dh), lambda n, c, pt: (n, 0, 0, 0)),
                  pl.BlockSpec((None, None, t, ch), lambda n, c, pt: (n, c, 0, 0)),
                  pl.BlockSpec((None, None, t, ch), lambda n, c, pt: (n, nc, 0, 0)),
                  pl.BlockSpec((None, hkv, LANES, dh), lambda n, c, pt: (n, 0, 0, 0)),
                  pl.BlockSpec((None, hkv, LANES, dh), lambda n, c, pt: (n, 0, 0, 0))]
        + _page_specs((None, None, hkv, dh, PAGE_SIZE), layer, pages)
        + _page_specs((None, None, hkv, dh, PAGE_SIZE), layer, pages),
        out_specs=pl.BlockSpec((None, hkv, rows, dh), lambda n, c, pt: (n, 0, 0, 0)),
        scratch_shapes=[pltpu.VMEM((hkv * rows, 1), F32), pltpu.VMEM((hkv * rows, 1), F32),
                        pltpu.VMEM((hkv * rows, dh), F32)],
    )
    return pl.pallas_call(
        functools.partial(_dsa_attend_kernel, pages=pages),
        out_shape=jax.ShapeDtypeStruct((b, hkv, rows, dh), F32),
        grid_spec=grid_spec,
        compiler_params=_cparams("parallel", "arbitrary"),
        name="dsa_sample_attend",
    )(page_table, q4, mask, mask, k_new_pad, v_new_pad, *([pool_k] * pages), *([pool_v] * pages))


GN_EPS = 64e-5
RWKV_HEAD = 64
RWKV_BATCH_BLOCK = 4


def _dot_f32(a, b):
    return jnp.dot(a, b, preferred_element_type=F32, precision=lax.Precision.HIGHEST)


def _rwkv_prep_kernel(pc_ref, prev_ref, mu_ref, vec_ref, lora_ref, gup_ref, seg_ref,
                      r_ref, w_ref, k_ref, v_ref, kk_ref, b_ref, g_ref, bonus_ref):
    cw = r_ref.shape[1]
    pc = pc_ref[...]
    xm = pc + (prev_ref[...] - pc) * mu_ref[...]
    r, kc, vc = xm[:, 0:cw], xm[:, cw:2 * cw], xm[:, 2 * cw:3 * cw]
    wa = xm[:, 3 * cw:3 * cw + LANES]
    gd = xm[:, 3 * cw + LANES:]
    lane = lax.broadcasted_iota(I32, wa.shape, 1)
    wa = jnp.where(lane < LANES // 2, jnp.tanh(wa), wa)
    lo = _dot(wa.astype(MXU_DTYPE), lora_ref[...])
    w0, a0, k_k, k_a, r_k = (vec_ref[i:i + 1, :] for i in range(5))
    w_log = -jax.nn.softplus(-(w0 + lo[:, 0:cw])) - 0.5
    a = jax.nn.sigmoid(a0 + lo[:, cw:2 * cw])
    kk = kc * k_k
    norm = jnp.sqrt(_dot_f32(kk * kk, seg_ref[...]))
    kk = kk / jnp.maximum(norm, 1e-12)
    kc = kc * (1.0 + (a - 1.0) * k_a)
    r_ref[...] = r
    w_ref[...] = jnp.exp(-jnp.exp(w_log))
    k_ref[...] = kc
    v_ref[...] = vc
    kk_ref[...] = kk
    b_ref[...] = -(kk * a)
    g_ref[...] = _dot(jax.nn.sigmoid(gd).astype(MXU_DTYPE), gup_ref[...])
    bonus_ref[...] = _dot_f32(r * kc * r_k, seg_ref[...]) * vc


def _rwkv_prep(pc, prev, mu, vecs, lora, g_up, seg, tm):
    n, width = prev.shape
    cw = vecs.shape[1]
    row = lambda w: pl.BlockSpec((tm, w), lambda i: (i, 0))
    full = lambda a: pl.BlockSpec(a.shape, lambda i: (0, 0))
    out = jax.ShapeDtypeStruct((n, cw), F32)
    return pl.pallas_call(
        _rwkv_prep_kernel,
        out_shape=(out,) * 8,
        grid=(n // tm,),
        in_specs=[row(width), row(width), full(mu), full(vecs), full(lora), full(g_up), full(seg)],
        out_specs=(row(cw),) * 8,
        compiler_params=_cparams("parallel"),
        name="rwkv_prep",
    )(pc, prev, mu, vecs, lora, g_up, seg)


def _segment_sum(x, seg):
    hi = x.astype(MXU_DTYPE)
    lo = (x - hi.astype(F32)).astype(MXU_DTYPE)
    return _dot(hi, seg) + _dot(lo, seg)


def _rwkv_scan_kernel(*refs, nb):
    ins, (seg_ref, s0_ref, y_ref, s_ref) = refs[:6 * nb], refs[6 * nb:]

    @pl.when(pl.program_id(1) == 0)
    def _():
        s_ref[...] = s0_ref[...]

    _, rows, width = s_ref.shape
    pairs = rows // RWKV_HEAD
    lane = lax.broadcasted_iota(I32, (nb * rows, width), 1)
    sub = lax.broadcasted_iota(I32, (nb * rows, width), 0)
    diag = (sub & (RWKV_HEAD - 1)) == (lane & (RWKV_HEAD - 1))
    own = (lax.broadcasted_iota(I32, (2, width), 0) == 0) == (lax.broadcasted_iota(I32, (2, width), 1) < RWKV_HEAD)
    seg = seg_ref[...]

    def group(g, carry):
        t0 = pl.multiple_of(g * SUBLANES, SUBLANES)
        r8, w8, k8, v8, kk8, nb8 = ([ins[6 * bb + q][pl.ds(t0, SUBLANES), :] for bb in range(nb)] for q in range(6))
        s = s_ref[...].reshape(nb * rows, width)
        for i in range(SUBLANES):
            def per_row(x8):
                return jnp.concatenate([jnp.broadcast_to(x[i:i + 1, p * width:(p + 1) * width], (RWKV_HEAD, width))
                                        for x in x8 for p in range(pairs)], axis=0)
            sa = _segment_sum(s * per_row(kk8), seg)
            vcol = _segment_sum(jnp.where(diag, per_row(v8), 0.0), seg)
            s = s * per_row(w8) + sa * per_row(nb8) + vcol * per_row(k8)
            sb = s.astype(MXU_DTYPE)
            for bb in range(nb):
                for p in range(pairs):
                    r2 = jnp.where(own, r8[bb][i:i + 1, p * width:(p + 1) * width], 0.0)
                    r0 = (bb * pairs + p) * RWKV_HEAD
                    y = _dot_nt(r2.astype(MXU_DTYPE), sb[r0:r0 + RWKV_HEAD])
                    y_ref[bb, pl.ds(t0 + i, 1), 2 * p:2 * p + 2, :] = y[None]
        s_ref[...] = s.reshape(nb, rows, width)
        return carry

    lax.fori_loop(0, ins[0].shape[0] // SUBLANES, group, 0)


def _rwkv_scan(ins, row_start, batch, t, s0_packed, seg2, nb):
    cw = ins[0].shape[1]
    heads = cw // RWKV_HEAD
    tb = min(t, 64)
    nt = t // tb
    rb0 = row_start // tb
    rows = [pl.BlockSpec((tb, cw), lambda b, i, bb=bb: (rb0 + (b * nb + bb) * nt + i, 0)) for bb in range(nb)]
    st = pl.BlockSpec((nb,) + s0_packed.shape[1:], lambda b, i: (b, 0, 0))
    return pl.pallas_call(
        functools.partial(_rwkv_scan_kernel, nb=nb),
        out_shape=(jax.ShapeDtypeStruct((batch, t, heads, RWKV_HEAD), F32),
                   jax.ShapeDtypeStruct(s0_packed.shape, F32)),
        grid=(batch // nb, nt),
        in_specs=[spec for spec in rows for _ in range(6)] + [pl.BlockSpec(seg2.shape, lambda b, i: (0, 0)), st],
        out_specs=(pl.BlockSpec((nb, tb, heads, RWKV_HEAD), lambda b, i: (b, i, 0, 0)), st),
        compiler_params=_cparams("parallel", "arbitrary"),
        name="rwkv_scan",
    )(*(list(ins) * nb), seg2, s0_packed)


def _rwkv_post_kernel(y_ref, bonus_ref, g_ref, ln_ref, seg_ref, o_ref):
    y = y_ref[...]
    avg = seg_ref[...] * (1.0 / RWKV_HEAD)
    d = y - _dot_f32(y, avg)
    var = _dot_f32(d * d, avg)
    yn = d * lax.rsqrt(var + GN_EPS) * ln_ref[0:1, :] + ln_ref[1:2, :]
    o_ref[...] = (yn + bonus_ref[...]) * g_ref[...]


def _rwkv_post(y, bonus, g, ln, seg, row_start, tm):
    n, cw = y.shape
    rb0 = row_start // tm
    row = pl.BlockSpec((tm, cw), lambda i: (i, 0))
    off = pl.BlockSpec((tm, cw), lambda i: (rb0 + i, 0))
    full = lambda a: pl.BlockSpec(a.shape, lambda i: (0, 0))
    return pl.pallas_call(
        _rwkv_post_kernel,
        out_shape=jax.ShapeDtypeStruct((n, cw), F32),
        grid=(n // tm,),
        in_specs=[row, off, off, full(ln), full(seg)],
        out_specs=row,
        compiler_params=_cparams("parallel"),
        name="rwkv_post",
    )(y, bonus, g, ln, seg)


MLA_HEADS = 8
MLA_NOPE = 64
MLA_ROPE = 32
MLA_SCALE = (MLA_NOPE + MLA_ROPE) ** -0.5


def _rope_tile(x, cos, sin):
    lane = lax.broadcasted_iota(I32, x.shape, 1)
    half = MLA_ROPE // 2
    rot = jnp.where(lane < half, pltpu.roll(x, LANES - half, 1), pltpu.roll(x, half, 1))
    return x * cos + rot * sin


def _mla_prep_kernel(qd_ref, ckv_ref, kr_ref, cos_ref, sin_ref, qn_ref, kvn_ref, wuq_ref, wuk_ref,
                     ql_ref, qr_ref, c_ref, krn_ref):
    cq = _rms(qd_ref[...], qn_ref[...])
    qh = _dot(cq.astype(MXU_DTYPE), wuq_ref[...])
    nope = MLA_HEADS * MLA_NOPE
    ql_ref[...] = _dot(qh[:, 0:nope].astype(MXU_DTYPE), wuk_ref[...]).astype(ql_ref.dtype)
    cos, sin = cos_ref[...], sin_ref[...]
    for h in range(MLA_HEADS):
        lanes = slice(nope + h * LANES, nope + (h + 1) * LANES)
        qr_ref[:, h * LANES:(h + 1) * LANES] = _rope_tile(qh[:, lanes], cos, sin).astype(qr_ref.dtype)
    c_ref[...] = _rms(ckv_ref[...], kvn_ref[...])
    krn_ref[...] = _rope_tile(kr_ref[...], cos, sin)


def _mla_prep(proj, col_blocks, row_start, nrows, cos, sin, q_norm, kv_norm, wuq, wuk, tm):
    qd0, ckv0, kr0 = col_blocks
    d_q, d_kv = q_norm.shape[1], kv_norm.shape[1]
    rb0 = row_start // tm
    nper = cos.shape[0] // tm
    full = lambda a: pl.BlockSpec(a.shape, lambda i: (0, 0))
    rows = lambda w: pl.BlockSpec((tm, w), lambda i: (i, 0))
    tab = pl.BlockSpec((tm, LANES), lambda i: (i % nper, 0))
    return pl.pallas_call(
        _mla_prep_kernel,
        out_shape=(jax.ShapeDtypeStruct((nrows, MLA_HEADS * d_kv), MXU_DTYPE),
                   jax.ShapeDtypeStruct((nrows, MLA_HEADS * LANES), MXU_DTYPE),
                   jax.ShapeDtypeStruct((nrows, d_kv), F32),
                   jax.ShapeDtypeStruct((nrows, LANES), F32)),
        grid=(nrows // tm,),
        in_specs=[pl.BlockSpec((tm, d_q), lambda i: (rb0 + i, qd0 * LANES // d_q)),
                  pl.BlockSpec((tm, d_kv), lambda i: (rb0 + i, ckv0 * LANES // d_kv)),
                  pl.BlockSpec((tm, LANES), lambda i: (rb0 + i, kr0)),
                  tab, tab, full(q_norm), full(kv_norm), full(wuq), full(wuk)],
        out_specs=(rows(MLA_HEADS * d_kv), rows(MLA_HEADS * LANES), rows(d_kv), rows(LANES)),
        compiler_params=_cparams("parallel"),
        name="mla_prep",
    )(proj, proj, proj, cos, sin, q_norm, kv_norm, wuq, wuk)


def _mla_prompt_kernel(ql_ref, qr_ref, c_ref, kr_ref, wuv_ref, o_ref, m_ref, l_ref, acc_ref, *, kb_size):
    tq = ql_ref.shape[0]
    d_kv = c_ref.shape[1]
    q0 = pl.program_id(1) * tq
    n_kb = (q0 + tq - 1) // kb_size + 1
    qpos = q0 + lax.broadcasted_iota(I32, (tq, kb_size), 0)
    lane = lax.broadcasted_iota(I32, (tq, kb_size), 1)
    m_ref[...] = jnp.full_like(m_ref, ONLINE_SOFTMAX_FLOOR)
    l_ref[...] = jnp.zeros_like(l_ref)
    acc_ref[...] = jnp.zeros_like(acc_ref)

    def block(kb, carry):
        k0 = pl.multiple_of(kb * kb_size, kb_size)
        c, kr = c_ref[pl.ds(k0, kb_size), :], kr_ref[pl.ds(k0, kb_size), :]
        scores = [(_dot_nt(ql_ref[:, h * d_kv:(h + 1) * d_kv], c)
                   + _dot_nt(qr_ref[:, h * LANES:(h + 1) * LANES], kr)) * MLA_SCALE for h in range(MLA_HEADS)]
        _online_softmax_heads(m_ref, l_ref, acc_ref, scores, k0 + lane <= qpos, lambda g, p: _dot(p, c))
        return carry

    lax.fori_loop(0, n_kb, block, 0)
    o_lat = (acc_ref[...] / l_ref[...]).astype(MXU_DTYPE)
    o_lat = jnp.concatenate([o_lat[h * tq:(h + 1) * tq] for h in range(MLA_HEADS)], axis=1)
    o_ref[...] = _dot(o_lat, wuv_ref[...])


def _mla_prompt(q_lat, q_rope, c, kr, wuv, batch, t, tq=128):
    d_kv = c.shape[1]
    nq = t // tq
    rows = MLA_HEADS * tq
    return pl.pallas_call(
        functools.partial(_mla_prompt_kernel, kb_size=math.gcd(t, KEY_BLOCK)),
        scratch_shapes=[pltpu.VMEM((rows, 1), F32), pltpu.VMEM((rows, 1), F32), pltpu.VMEM((rows, d_kv), F32)],
        out_shape=jax.ShapeDtypeStruct((batch * t, wuv.shape[1]), F32),
        grid=(batch, nq),
        in_specs=[pl.BlockSpec((tq, MLA_HEADS * d_kv), lambda b, i: (b * nq + i, 0)),
                  pl.BlockSpec((tq, MLA_HEADS * LANES), lambda b, i: (b * nq + i, 0)),
                  pl.BlockSpec((t, d_kv), lambda b, i: (b, 0)),
                  pl.BlockSpec((t, LANES), lambda b, i: (b, 0)),
                  pl.BlockSpec(wuv.shape, lambda b, i: (0, 0))],
        out_specs=pl.BlockSpec((tq, wuv.shape[1]), lambda b, i: (b * nq + i, 0)),
        compiler_params=_cparams("parallel", "arbitrary"),
        name="mla_prompt",
    )(q_lat, q_rope, c, kr, wuv)


def _mla_sample_kernel(pt_ref, ql_ref, qr_ref, cn_ref, krn_ref, *rest, pages, t_new):
    c_pages, kr_pages = rest[:pages], rest[pages:2 * pages]
    o_ref, m_ref, l_ref, acc_ref = rest[2 * pages:]
    step = pl.program_id(1)

    @pl.when(step == 0)
    def _():
        m_ref[...] = jnp.full_like(m_ref, ONLINE_SOFTMAX_FLOOR)
        l_ref[...] = jnp.zeros_like(l_ref)
        acc_ref[...] = jnp.zeros_like(acc_ref)

    ql, qr = ql_ref[...], qr_ref[...]
    cc = jnp.concatenate([r[...] for r in c_pages], axis=0).astype(MXU_DTYPE)
    kc = jnp.concatenate([r[...] for r in kr_pages], axis=1).astype(MXU_DTYPE)
    s = (_dot_nt(ql, cc) + _dot(qr[:, 0:MLA_ROPE], kc)) * MLA_SCALE
    _online_softmax_step(m_ref, l_ref, acc_ref, 0, s, jnp.full(s.shape, True), cc)

    @pl.when(step == pl.num_programs(1) - 1)
    def _():
        cn = cn_ref[...]
        s_new = (_dot_nt(ql, cn) + _dot_nt(qr, krn_ref[...])) * MLA_SCALE
        qpos = lax.broadcasted_iota(I32, s_new.shape, 0) % t_new
        kpos = lax.broadcasted_iota(I32, s_new.shape, 1)
        _online_softmax_step(m_ref, l_ref, acc_ref, 0, s_new, (kpos <= qpos) & (kpos < t_new), cn)
        o_ref[...] = acc_ref[0] / l_ref[0]


def _mla_sample(page_table, q_lat, q_rope, c_new_pad, kr_new_pad, pool_c, pool_kr, layer, t_new):
    b, n_pages = page_table.shape
    pages = math.gcd(n_pages, PAGES_PER_STEP)
    _, rows, d_kv = q_lat.shape
    one = lambda a: pl.BlockSpec((None,) + a.shape[1:], lambda n, c, pt: (n, 0, 0))
    grid_spec = pltpu.PrefetchScalarGridSpec(
        num_scalar_prefetch=1,
        grid=(b, n_pages // pages),
        in_specs=[one(q_lat), one(q_rope), one(c_new_pad), one(kr_new_pad)]
        + _page_specs((None, None, PAGE_SIZE, d_kv), layer, pages)
        + _page_specs((None, None, MLA_ROPE, PAGE_SIZE), layer, pages),
        out_specs=pl.BlockSpec((None, rows, d_kv), lambda n, c, pt: (n, 0, 0)),
        scratch_shapes=[pltpu.VMEM((1, rows, 1), F32), pltpu.VMEM((1, rows, 1), F32),
                        pltpu.VMEM((1, rows, d_kv), F32)],
    )
    return pl.pallas_call(
        functools.partial(_mla_sample_kernel, pages=pages, t_new=t_new),
        out_shape=jax.ShapeDtypeStruct((b, rows, d_kv), F32),
        grid_spec=grid_spec,
        compiler_params=_cparams("parallel", "arbitrary"),
        name="mla_sample",
    )(page_table, q_lat, q_rope, c_new_pad, kr_new_pad, *([pool_c] * pages), *([pool_kr] * pages))


ROW_BLOCK = 256
A_HEADS = 4
B_HEADS, B_KV_HEADS, B_DH = 8, 4, 64
IDX_HEADS, IDX_DIM = 8, 64
C_WIDTH = 512
RWKV_IN = 3 * C_WIDTH + 64 + 64 + 128
D_Q_RANK, D_KV_RANK = 384, 256
ODD_COLS = RWKV_IN + D_KV_RANK + 2 * LANES + D_Q_RANK
ODD_SLABS = ((RWKV_IN + D_KV_RANK + 2 * LANES) // LANES, RWKV_IN // LANES, (RWKV_IN + D_KV_RANK) // LANES)


def _pad_rows(a, rows=LANES):
    return jnp.pad(a, ((0, 0),) * (a.ndim - 2) + ((0, rows - a.shape[-2]), (0, 0)))


def _head_major(a, b, t, h):
    return jnp.transpose(a.reshape(b, t, h, -1), (0, 2, 1, 3))


def _token_major(a):
    b, h, t, w = a.shape
    return jnp.transpose(a, (0, 2, 1, 3)).reshape(b * t, h * w)


def _rope_tables(pos):
    half = MLA_ROPE // 2
    inv = ROPE_THETA ** (-jnp.arange(0, MLA_ROPE, 2, dtype=F32) / MLA_ROPE)
    ang = pos.astype(F32)[:, None] * inv[None, :]
    zeros = jnp.zeros((pos.shape[0], LANES - 2 * half), F32)
    cos, sin = jnp.cos(ang), jnp.sin(ang)
    return jnp.concatenate([cos, cos, zeros], axis=1), jnp.concatenate([-sin, sin, zeros], axis=1)


def kernel(x_prompt, x_sample, cache_dsa_k, cache_dsa_v, cache_dsa_idx, cache_mla_ckv, cache_mla_krope, state_hgrn, state_rwkv, state_shift, page_table, norm_mix, norm_ffn, norm_final, w_in_even, w_out_even, hgrn_lb, hgrn_norm, w_in_odd, w_out_odd, rwkv_mu, rwkv_w0, rwkv_w_up, rwkv_a0, rwkv_a_up, rwkv_g_up, rwkv_k_k, rwkv_k_a, rwkv_r_k, rwkv_ln_w, rwkv_ln_b, mla_q_norm, mla_w_uq, mla_kv_norm, mla_w_uk, mla_w_uv, peer_wq, peer_subkeys, peer_u, peer_v):
    bp, tp, d_model = x_prompt.shape
    bs, ts, _ = x_sample.shape
    n_p, n_s = bp * tp, bs * ts
    n_all = n_p + n_s
    n_pad = _round_up(n_all, PEER_TOKEN_BLOCK)
    tm_s = min(ROW_BLOCK, n_s)
    n_past = page_table.shape[1] * PAGE_SIZE
    depth = norm_mix.shape[0]
    md = MXU_DTYPE

    def all_rows(p, s):
        return jnp.concatenate([p, s, jnp.zeros((n_pad - n_all, p.shape[1]), p.dtype)], axis=0)

    x = all_rows(x_prompt.reshape(n_p, d_model), x_sample.reshape(n_s, d_model))

    lb_cum = jnp.cumsum(jax.nn.softmax(hgrn_lb.astype(F32), axis=0), axis=0)
    lower_bounds = lb_cum - lb_cum[:1]
    seg = jnp.kron(jnp.eye(C_WIDTH // RWKV_HEAD, dtype=F32), jnp.ones((RWKV_HEAD, RWKV_HEAD), F32))
    pool_dsa_k = jnp.transpose(cache_dsa_k, (0, 1, 3, 4, 2))
    pool_dsa_v = jnp.transpose(cache_dsa_v, (0, 1, 3, 4, 2))
    pool_dsa_idx = jnp.transpose(cache_dsa_idx, (0, 1, 3, 2))
    pool_mla_kr = jnp.transpose(cache_mla_krope, (0, 1, 3, 2))
    eye_h = jnp.eye(MLA_HEADS, dtype=F32)
    cos_p, sin_p = _rope_tables(jnp.arange(tp))
    cos_s, sin_s = (jnp.tile(a, (tm_s // ts, 1)) for a in _rope_tables(n_past + jnp.arange(ts)))

    outs = {k: [] for k in ("pk", "pv", "pi", "pc", "pr", "ph", "ps", "psh", "sk", "sv", "si", "sc", "sr", "sh", "ss", "ssh")}

    for l in range(depth):
        j = l // 2
        if l % 2 == 0:
            w_in = jnp.pad(w_in_even[j], ((0, 0), (0, _round_up(w_in_even.shape[2], LANES) - w_in_even.shape[2]))).astype(md)
            proj = _matmul(x, w_in, g=norm_mix[l], tm=ROW_BLOCK)
            aw = A_HEADS * LANES
            c_q, c_k, c_v = 4 * aw, 4 * aw + B_HEADS * B_DH, 4 * aw + (B_HEADS + B_KV_HEADS) * B_DH
            c_qi = c_v + B_KV_HEADS * B_DH
            c_ki = c_qi + IDX_HEADS * IDX_DIM
            c_wi = c_ki + IDX_DIM
            mixes = []
            for rows, b, t, s0_t, kk, kv, ki_key, kh in ((slice(0, n_p), bp, tp, None, "pk", "pv", "pi", "ph"),
                                                    (slice(n_p, n_all), bs, ts, state_hgrn[j], "sk", "sv", "si", "sh")):
                sample = s0_t is not None
                s0_t = jnp.swapaxes(s0_t, -1, -2).astype(F32) if sample else jnp.zeros((b, A_HEADS, LANES, LANES), F32)
                oa, s_a = _hgrn(proj, rows.start, b, t, A_HEADS, lower_bounds[j], hgrn_norm[j], s0_t)
                pr = proj[rows]
                qb, kb, vb = pr[:, c_q:c_k], pr[:, c_k:c_v], pr[:, c_v:c_qi]
                qi, ki, wi = pr[:, c_qi:c_ki], pr[:, c_ki:c_wi], pr[:, c_wi:c_wi + IDX_HEADS]
                q4 = _head_major(qb, b, t, B_HEADS).astype(md)
                k4 = _head_major(kb, b, t, B_KV_HEADS).astype(md)
                v4 = _head_major(vb, b, t, B_KV_HEADS).astype(md)
                ki3 = ki.reshape(b, t, IDX_DIM).astype(md)
                tq = t if sample else DSA_QUERY_BLOCK
                qi2 = jnp.transpose(qi.reshape(b, t // tq, tq, IDX_HEADS, IDX_DIM), (0, 1, 3, 2, 4))
                qi2 = qi2.reshape(b, t // tq, IDX_HEADS * tq, IDX_DIM).astype(md)
                wcol = jnp.transpose(wi.reshape(b, t // tq, tq, IDX_HEADS), (0, 1, 3, 2)).reshape(b, t // tq, IDX_HEADS * tq, 1)
                if sample:
                    topk = min(DSA_TOPK_MAX, (n_past + t) // 4)
                    mask = _dsa_select(page_table, qi2[:, 0], wcol[:, 0], _pad_rows(ki3), pool_dsa_idx, j, topk, t, IDX_HEADS)
                    group = B_HEADS // B_KV_HEADS
                    o4 = _dsa_attend(page_table, q4.reshape(b, B_KV_HEADS, group * t, B_DH), mask,
                                     _pad_rows(k4), _pad_rows(v4), pool_dsa_k, pool_dsa_v, j)
                    o4 = o4.reshape(b, B_HEADS, t, B_DH)
                else:
                    o4 = _dsa_prompt(qi2, wcol, ki3, q4, k4, v4, IDX_HEADS, tq)
                mixes.append(jnp.concatenate([oa, _token_major(o4)], axis=1))
                outs[kk].append(kb.reshape(b, t, B_KV_HEADS, B_DH))
                outs[kv].append(vb.reshape(b, t, B_KV_HEADS, B_DH))
                outs[ki_key].append(ki.reshape(b, t, IDX_DIM))
                outs[kh].append(jnp.swapaxes(s_a, -1, -2))
            x = _matmul(all_rows(*mixes), w_out_even[j].astype(md), res=x, tm=ROW_BLOCK)
        else:
            w = w_in_odd[j]
            c_qd, c_ckv = RWKV_IN, RWKV_IN + D_Q_RANK
            c_kr = c_ckv + D_KV_RANK
            w_in = jnp.concatenate([w[:, :RWKV_IN], w[:, c_ckv:c_kr], w[:, c_kr:c_kr + MLA_ROPE],
                                    jnp.zeros((d_model, 2 * LANES - MLA_ROPE), w.dtype), w[:, c_qd:c_ckv]], axis=1).astype(md)
            proj = _matmul(x, w_in, g=norm_mix[l], tm=ROW_BLOCK)
            pc_p = proj[:n_p, :RWKV_IN].reshape(bp, tp, RWKV_IN)
            pc_s = proj[n_p:n_all, :RWKV_IN].reshape(bs, ts, RWKV_IN)
            prev_p = jnp.concatenate([jnp.zeros((bp, 1, RWKV_IN), F32), pc_p[:, :-1]], axis=1)
            prev_s = jnp.concatenate([state_shift[j].astype(F32)[:, None], pc_s[:, :-1]], axis=1)
            prev = all_rows(prev_p.reshape(n_p, RWKV_IN), prev_s.reshape(n_s, RWKV_IN))
            zeros_l = jnp.zeros((rwkv_w_up.shape[1], C_WIDTH), F32)
            lora = jnp.concatenate([jnp.concatenate([rwkv_w_up[j], zeros_l], axis=1),
                                    jnp.concatenate([zeros_l, rwkv_a_up[j]], axis=1)], axis=0).astype(md)
            vecs = jnp.stack([rwkv_w0[j], rwkv_a0[j], rwkv_k_k[j], rwkv_k_a[j], rwkv_r_k[j].reshape(-1)]).astype(F32)
            prep = _rwkv_prep(proj, prev, rwkv_mu[j].reshape(1, -1).astype(F32), vecs, lora, rwkv_g_up[j].astype(md), seg, ROW_BLOCK)
            scan_in, g_all, bonus_all = prep[:6], prep[6], prep[7]
            ln = jnp.stack([rwkv_ln_w[j], rwkv_ln_b[j]]).astype(F32)

            wuq = mla_w_uq[j].reshape(D_Q_RANK, MLA_HEADS, MLA_NOPE + MLA_ROPE)
            wuq_rope = jnp.pad(wuq[:, :, MLA_NOPE:], ((0, 0), (0, 0), (0, LANES - MLA_ROPE))).reshape(D_Q_RANK, -1)
            wuq_p = jnp.concatenate([wuq[:, :, :MLA_NOPE].reshape(D_Q_RANK, -1), wuq_rope], axis=1).astype(md)
            wuk = jnp.einsum("chn,hg->hngc", mla_w_uk[j], eye_h).reshape(MLA_HEADS * MLA_NOPE, -1).astype(md)
            wuv = jnp.einsum("chv,hg->hcgv", mla_w_uv[j], eye_h).reshape(MLA_HEADS * D_KV_RANK, -1).astype(md)
            q_norm = mla_q_norm[j].reshape(1, -1).astype(F32)
            kv_norm = mla_kv_norm[j].reshape(1, -1).astype(F32)

            mixes = []
            for start, b, t, tm, cos, sin, sample in ((0, bp, tp, ROW_BLOCK, cos_p, sin_p, False),
                                                      (n_p, bs, ts, tm_s, cos_s, sin_s, True)):
                n = b * t
                if sample:
                    heads = C_WIDTH // RWKV_HEAD
                    s0 = state_rwkv[j].astype(F32).reshape(b, heads // 2, 2, RWKV_HEAD, RWKV_HEAD)
                    s0 = jnp.transpose(s0, (0, 1, 3, 2, 4)).reshape(b, heads // 2 * RWKV_HEAD, 2 * RWKV_HEAD)
                else:
                    s0 = jnp.zeros((b, C_WIDTH // LANES * RWKV_HEAD, LANES), F32)
                nb = math.gcd(b, RWKV_BATCH_BLOCK)
                y3, s_c = _rwkv_scan(scan_in, start, b, t, s0, seg[:LANES, :LANES].astype(md), nb)
                yc = _rwkv_post(y3.reshape(n, C_WIDTH), bonus_all, g_all, ln, seg, start, tm)
                s_c = jnp.transpose(s_c.reshape(b, -1, RWKV_HEAD, 2, RWKV_HEAD), (0, 1, 3, 2, 4))
                s_c = s_c.reshape(b, -1, RWKV_HEAD, RWKV_HEAD)

                q_lat, q_rope, c_new, kr_new = _mla_prep(proj, ODD_SLABS, start, n, cos, sin, q_norm, kv_norm, wuq_p, wuk, tm)
                if sample:
                    hq = lambda a: _head_major(a, b, t, MLA_HEADS).reshape(b, MLA_HEADS * t, -1)
                    o_lat = _mla_sample(page_table, hq(q_lat), hq(q_rope),
                                        _pad_rows(c_new.reshape(b, t, -1)).astype(md),
                                        _pad_rows(kr_new.reshape(b, t, -1)).astype(md),
                                        cache_mla_ckv, pool_mla_kr, j, t)
                    o_lat = _token_major(o_lat.reshape(b, MLA_HEADS, t, -1))
                    od = _matmul(o_lat, wuv, tm=tm)
                else:
                    od = _mla_prompt(q_lat, q_rope, c_new.astype(md), kr_new.astype(md), wuv, b, t)
                mixes.append(jnp.concatenate([yc, od], axis=1))
                pre = "s" if sample else "p"
                outs[pre + "c"].append(c_new.reshape(b, t, D_KV_RANK))
                outs[pre + "r"].append(kr_new[:, :MLA_ROPE].reshape(b, t, MLA_ROPE))
                outs[pre + "s"].append(s_c)
                outs[pre + "sh"].append((pc_s if sample else pc_p)[:, -1])
            x = _matmul(all_rows(*mixes), w_out_odd[j].astype(md), res=x, tm=ROW_BLOCK)
        x = _peer_ffn(x, norm_ffn[l], _peer_prepare(peer_wq[l], peer_subkeys[l], peer_u[l], peer_v[l]))

    y = _rmsnorm(x, norm_final)
    y_prompt = y[:n_p].reshape(bp, tp, d_model)
    y_sample = y[n_p:n_all].reshape(bs, ts, d_model)
    st = lambda k: jnp.stack(outs[k])
    return (y_prompt, y_sample,
            st("pk"), st("pv"), st("pi"), st("pc"), st("pr"), st("ph"), st("ps"), st("psh"),
            st("sk"), st("sv"), st("si"), st("sc"), st("sr"), st("sh"), st("ss"), st("ssh"))
```

```python
import functools
import math

import jax
import jax.numpy as jnp
from jax import lax
from jax.experimental import pallas as pl
from jax.experimental.pallas import tpu as pltpu

F32 = jnp.float32
I32 = jnp.int32
MXU_DTYPE = jnp.bfloat16
GATE_DTYPE = jnp.bfloat16
LANES = 128
SUBLANES = 8
VMEM_LIMIT_BYTES = 56 * 1024 * 1024
NEG_INF = float("-inf")

RMS_EPS = 1e-6
PAGE_SIZE = 128
ROPE_THETA = 10000.0
PEER_HEADS = 8
PEER_NKEYS = 128
PEER_TOPK = 16
PEER_TOKEN_BLOCK = 512
PEER_GROUP = 16


def _cparams(*sem):
    return pltpu.CompilerParams(dimension_semantics=sem, vmem_limit_bytes=VMEM_LIMIT_BYTES)


def _round_up(n, m):
    return -(-n // m) * m


def _rms(x, g):
    return x * lax.rsqrt(jnp.mean(x * x, axis=-1, keepdims=True) + RMS_EPS) * g


def _dot(a, b):
    return jnp.dot(a, b, preferred_element_type=F32)


def _dot_nt(a, b):
    return lax.dot_general(a, b, (((1,), (1,)), ((), ())), preferred_element_type=F32)


def _matmul_kernel(*refs, norm, residual):
    it = iter(refs)
    a_ref = next(it)
    g_ref = next(it) if norm else None
    w_ref = next(it)
    r_ref = next(it) if residual else None
    o_ref = next(it)
    a = a_ref[...]
    if norm:
        a = _rms(a, g_ref[...])
    acc = _dot(a.astype(w_ref.dtype), w_ref[...])
    if residual:
        acc = acc + r_ref[...]
    o_ref[...] = acc


def _matmul(a, w, g=None, res=None, tm=256):
    n, k = a.shape
    m = w.shape[1]
    ins, specs = [a], [pl.BlockSpec((tm, k), lambda i: (i, 0))]
    if g is not None:
        ins.append(g.reshape(1, k).astype(F32))
        specs.append(pl.BlockSpec((1, k), lambda i: (0, 0)))
    ins.append(w)
    specs.append(pl.BlockSpec((k, m), lambda i: (0, 0)))
    if res is not None:
        ins.append(res)
        specs.append(pl.BlockSpec((tm, m), lambda i: (i, 0)))
    return pl.pallas_call(
        functools.partial(_matmul_kernel, norm=g is not None, residual=res is not None),
        out_shape=jax.ShapeDtypeStruct((n, m), F32),
        grid=(n // tm,),
        in_specs=specs,
        out_specs=pl.BlockSpec((tm, m), lambda i: (i, 0)),
        compiler_params=_cparams("parallel"),
        name="proj_matmul",
    )(*ins)


def _rmsnorm_kernel(x_ref, g_ref, o_ref):
    o_ref[...] = _rms(x_ref[...], g_ref[...])


def _rmsnorm(x, g, tm=512):
    n, d = x.shape
    return pl.pallas_call(
        _rmsnorm_kernel,
        out_shape=jax.ShapeDtypeStruct((n, d), F32),
        grid=(n // tm,),
        in_specs=[pl.BlockSpec((tm, d), lambda i: (i, 0)), pl.BlockSpec((1, d), lambda i: (0, 0))],
        out_specs=pl.BlockSpec((tm, d), lambda i: (i, 0)),
        compiler_params=_cparams("parallel"),
        name="final_rmsnorm",
    )(x, g.reshape(1, d).astype(F32))


def _peer_fold_kernel(sub_ref, wq_ref, o_ref):
    o_ref[...] = _dot_nt(sub_ref[...], wq_ref[...])


def _peer_fold(subkeys, wq):
    d_model = wq.shape[0]
    half = subkeys.shape[2]
    return pl.pallas_call(
        _peer_fold_kernel,
        out_shape=jax.ShapeDtypeStruct((2, PEER_HEADS, PEER_NKEYS, d_model), F32),
        grid=(2, PEER_HEADS),
        in_specs=[pl.BlockSpec((None, PEER_NKEYS, half), lambda p, h: (p, 0, 0)),
                  pl.BlockSpec((d_model, half), lambda p, h: (0, h * 2 + p))],
        out_specs=pl.BlockSpec((None, None, PEER_NKEYS, d_model), lambda p, h: (p, h, 0, 0)),
        compiler_params=_cparams("parallel", "parallel"),
        name="peer_fold",
    )(subkeys.astype(MXU_DTYPE), wq.astype(MXU_DTYPE))


def _peer_scores_kernel(x_ref, g_ref, m_ref, s_ref, h_ref):
    hb = _rms(x_ref[...], g_ref[...]).astype(MXU_DTYPE)
    s_ref[...] = _dot_nt(m_ref[...], hb)
    h_ref[...] = hb.T


def _peer_scores(x, g, mcat, tm=PEER_TOKEN_BLOCK):
    n, d = x.shape
    rows = mcat.shape[0]
    return pl.pallas_call(
        _peer_scores_kernel,
        out_shape=(jax.ShapeDtypeStruct((rows, n), F32), jax.ShapeDtypeStruct((d, n), MXU_DTYPE)),
        grid=(n // tm,),
        in_specs=[pl.BlockSpec((tm, d), lambda i: (i, 0)),
                  pl.BlockSpec((1, d), lambda i: (0, 0)),
                  pl.BlockSpec((rows, d), lambda i: (0, 0))],
        out_specs=(pl.BlockSpec((rows, tm), lambda i: (0, i)), pl.BlockSpec((d, tm), lambda i: (0, i))),
        compiler_params=_cparams("parallel"),
        name="peer_scores",
    )(x, g.reshape(1, d).astype(F32), mcat)


def _bitonic_merge_desc(v):
    n = len(v)
    if n == 1:
        return v
    half = n // 2
    hi = [jnp.maximum(v[i], v[i + half]) for i in range(half)]
    lo = [jnp.minimum(v[i], v[i + half]) for i in range(half)]
    return _bitonic_merge_desc(hi) + _bitonic_merge_desc(lo)


def _sort_desc(v):
    n = len(v)
    if n == 1:
        return v
    return _bitonic_merge_desc(_sort_desc(v[:n // 2]) + _sort_desc(v[n // 2:])[::-1])


def _merge_top(x, y):
    n = len(x)
    return _bitonic_merge_desc([jnp.maximum(x[i], y[n - 1 - i]) for i in range(n)])


def _peer_topk_kernel(s_ref, c1_ref, e1_ref, r2_ref, e2_ref):
    hk = PEER_HEADS * PEER_NKEYS
    tn = s_ref.shape[1]

    def top_values(base):
        runs = []
        for r0 in range(0, PEER_NKEYS, PEER_TOPK):
            rows = [s_ref[base + (r0 + i) * PEER_HEADS:base + (r0 + i + 1) * PEER_HEADS, :] for i in range(PEER_TOPK)]
            runs.append(_sort_desc(rows))
        while len(runs) > 1:
            runs = [_merge_top(runs[i], runs[i + 1]) for i in range(0, len(runs), 2)]
        return runs[0]

    a = top_values(0)
    b = top_values(hk)
    neg = jnp.full((PEER_HEADS, tn), NEG_INF, F32)
    lists = [[a[k] + b[l] if (k + 1) * (l + 1) <= PEER_TOPK else neg for l in range(PEER_TOPK)] for k in range(PEER_TOPK)]
    while len(lists) > 1:
        lists = [_merge_top(lists[i], lists[i + 1]) for i in range(0, len(lists), 2)]
    best = lists[0]
    tau = best[-1]
    z = jnp.zeros_like(tau)
    for c in best:
        z = z + jnp.exp(c - best[0])
    inv_z = 1.0 / z

    twice = lambda x: jnp.concatenate([x, x], axis=0)
    tau2, a0_2, inv_z2, b2 = twice(tau), twice(a[0]), twice(inv_z), [twice(x) for x in b]

    def first_half(i, carry):
        rows = pl.ds(pl.multiple_of(i * 2 * PEER_HEADS, 2 * PEER_HEADS), 2 * PEER_HEADS)
        x = s_ref[rows, :]
        cnt = jnp.zeros_like(x)
        for bl in b2:
            cnt = cnt + jnp.where(x + bl >= tau2, 1.0, 0.0)
        c1_ref[rows, :] = cnt.astype(c1_ref.dtype)
        e1_ref[rows, :] = (jnp.exp(x - a0_2) * inv_z2).astype(e1_ref.dtype)
        return carry

    lax.fori_loop(0, PEER_NKEYS // 2, first_half, 0)
    for h in range(PEER_HEADS):
        x = s_ref[pl.ds(2 * hk + h * PEER_NKEYS, PEER_NKEYS), :]
        rank = jnp.zeros_like(x)
        for bl in b:
            rank = rank + jnp.where(bl[h:h + 1, :] > x, 1.0, 0.0)
        rows = slice(h * PEER_NKEYS, (h + 1) * PEER_NKEYS)
        r2_ref[rows, :] = rank.astype(r2_ref.dtype)
        e2_ref[rows, :] = jnp.exp(x - b[0][h:h + 1, :]).astype(e2_ref.dtype)


def _peer_topk(s_t, tn=PEER_TOKEN_BLOCK):
    rows, n = s_t.shape
    hk = PEER_HEADS * PEER_NKEYS
    out = jax.ShapeDtypeStruct((hk, n), GATE_DTYPE)
    spec = pl.BlockSpec((hk, tn), lambda i: (0, i))
    return pl.pallas_call(
        _peer_topk_kernel,
        out_shape=(out,) * 4,
        grid=(n // tn,),
        in_specs=[pl.BlockSpec((rows, tn), lambda i: (0, i))],
        out_specs=(spec,) * 4,
        compiler_params=_cparams("parallel"),
        name="peer_topk",
    )(s_t)


def _gelu(x):
    return 0.5 * x * (1.0 + lax.erf(x * (1.0 / math.sqrt(2.0))))


PEER_QUAD = 4 * PEER_NKEYS
GATE_TILE = (64, 256)


def _peer_experts_kernel(x_ref, h_ref, c1_ref, e1_ref, r2_ref, e2_ref, u_ref, v_ref, o_ref, acc_ref, *a_refs):
    @pl.when(pl.program_id(1) == 0)
    def _():
        acc_ref[...] = jnp.zeros_like(acc_ref)

    hb = h_ref[...]
    tn = hb.shape[1]
    te, tt = GATE_TILE
    n_quads = u_ref.shape[0] // PEER_QUAD
    scores = lambda quad: _dot(u_ref[quad * PEER_QUAD:(quad + 1) * PEER_QUAD, :], hb)
    st_next = scores(0)
    for quad in range(n_quads):
        st, st_next = st_next, (scores(quad + 1) if quad + 1 < n_quads else None)
        a_ref = a_refs[quad % 2]
        for sub in range(PEER_QUAD // PEER_NKEYS):
            i1 = quad * (PEER_QUAD // PEER_NKEYS) + sub
            c1 = c1_ref[i1 * PEER_HEADS:(i1 + 1) * PEER_HEADS, :]
            e1 = e1_ref[i1 * PEER_HEADS:(i1 + 1) * PEER_HEADS, :]
            for r0 in range(0, PEER_NKEYS, te):
                for l0 in range(0, tn, tt):
                    lanes = slice(l0, l0 + tt)
                    gate = None
                    for h in range(PEER_HEADS):
                        rows = slice(h * PEER_NKEYS + r0, h * PEER_NKEYS + r0 + te)
                        sel = r2_ref[rows, lanes] < c1[h:h + 1, lanes]
                        term = jnp.where(sel, e2_ref[rows, lanes], 0.0) * e1[h:h + 1, lanes]
                        gate = term if gate is None else gate + term
                    rows = slice(sub * PEER_NKEYS + r0, sub * PEER_NKEYS + r0 + te)
                    a_ref[rows, lanes] = (_gelu(st[rows, lanes]).astype(gate.dtype) * gate).astype(a_ref.dtype)
        acc_ref[...] += _dot(v_ref[quad], a_ref[...])

    @pl.when(pl.program_id(1) == pl.num_programs(1) - 1)
    def _():
        o_ref[...] = x_ref[...] + acc_ref[...].T


def _peer_experts(x, h_t, c1, e1, r2, e2, u2, v3, tn=PEER_TOKEN_BLOCK):
    d, n = h_t.shape
    hk = PEER_HEADS * PEER_NKEYS
    ge = PEER_GROUP * PEER_NKEYS
    tok = lambda rows: pl.BlockSpec((rows, tn), lambda i, g: (0, i))
    grp = pl.BlockSpec((PEER_GROUP * PEER_HEADS, tn), lambda i, g: (g, i))
    return pl.pallas_call(
        _peer_experts_kernel,
        out_shape=jax.ShapeDtypeStruct((n, d), F32),
        grid=(n // tn, u2.shape[0] // ge),
        in_specs=[pl.BlockSpec((tn, d), lambda i, g: (i, 0)), tok(d), grp, grp, tok(hk), tok(hk),
                  pl.BlockSpec((ge, d), lambda i, g: (g, 0)),
                  pl.BlockSpec((ge // PEER_QUAD, d, PEER_QUAD), lambda i, g: (g, 0, 0))],
        out_specs=pl.BlockSpec((tn, d), lambda i, g: (i, 0)),
        scratch_shapes=[pltpu.VMEM((d, tn), F32)] + [pltpu.VMEM((PEER_QUAD, tn), MXU_DTYPE)] * 2,
        compiler_params=_cparams("parallel", "arbitrary"),
        name="peer_experts",
    )(x, h_t, c1, e1, r2, e2, u2, v3)


def _peer_prepare(wq, subkeys, u, v):
    d = wq.shape[0]
    hk = PEER_HEADS * PEER_NKEYS
    mf = _peer_fold(subkeys, wq)
    inter = jnp.transpose(mf, (0, 2, 1, 3)).reshape(2 * hk, d)
    mcat = jnp.concatenate([inter, mf[1].reshape(hk, d)], axis=0).astype(MXU_DTYPE)
    v3 = jnp.transpose(v.astype(MXU_DTYPE).reshape(-1, PEER_QUAD, d), (0, 2, 1))
    return mcat, u.astype(MXU_DTYPE), v3


def _peer_ffn(x, g, prep):
    mcat, u2, v3 = prep
    s_t, h_t = _peer_scores(x, g, mcat)
    return _peer_experts(x, h_t, *_peer_topk(s_t), u2, v3)


HGRN_CHUNK = 64
HGRN_SUB = 16
MASKED_EXPONENT = -1e30


def _cumsum_rows(x):
    rows = x.shape[0]
    row = lax.broadcasted_iota(I32, x.shape, 0)
    d = 1
    while d < rows:
        x = x + jnp.where(row >= d, pltpu.roll(x, d, 0), 0.0)
        d *= 2
    return x


def _hgrn_chunk(q, k, v, g, s_t, sub):
    c = q.shape[0]
    cum = _cumsum_rows(g)
    o = _dot_nt((q * jnp.exp(cum)).astype(MXU_DTYPE), s_t.astype(MXU_DTYPE))
    outs = []
    for blk in range(c // sub):
        r0 = blk * sub
        q_b, cum_b, k_b, v_b = q[r0:r0 + sub], cum[r0:r0 + sub], k[r0:r0 + sub], v[r0:r0 + sub]
        o_b = o[r0:r0 + sub]
        if blk > 0:
            base = cum[r0 - 1:r0]
            qs = q_b * jnp.exp(cum_b - base)
            ks = k[0:r0] * jnp.exp(base - cum[0:r0])
            att = _dot_nt(qs.astype(MXU_DTYPE), ks.astype(MXU_DTYPE))
            o_b = o_b + _dot(att.astype(MXU_DTYPE), v[0:r0].astype(MXU_DTYPE))
        row = lax.broadcasted_iota(I32, (sub, q.shape[1]), 0)
        for s in range(sub):
            dec = jnp.exp(jnp.where(row >= s, cum_b - cum_b[s:s + 1], MASKED_EXPONENT))
            att = jnp.sum(q_b * k_b[s:s + 1] * dec, axis=1, keepdims=True)
            o_b = o_b + att * v_b[s:s + 1]
        outs.append(o_b)
    o = outs[0] if len(outs) == 1 else jnp.concatenate(outs, axis=0)
    last = cum[c - 1:c]
    kd = k * jnp.exp(last - cum)
    upd = lax.dot_general(v.astype(MXU_DTYPE), kd.astype(MXU_DTYPE), (((0,), (0,)), ((), ())),
                          preferred_element_type=F32)
    return o, s_t * jnp.exp(last) + upd


def _hgrn_kernel(q_ref, f_ref, i_ref, g_ref, lb_ref, gain_ref, s0_ref, o_ref, s_ref, *, chunk, sub):
    @pl.when(pl.program_id(2) == 0)
    def _():
        s_ref[...] = s0_ref[...]

    lb = lb_ref[...]
    s_t = s_ref[...]
    for c0 in range(0, q_ref.shape[0], chunk):
        rows = slice(c0, c0 + chunk)
        f = lb + (1.0 - lb) * jax.nn.sigmoid(f_ref[rows, :])
        o, s_t = _hgrn_chunk(jax.nn.silu(q_ref[rows, :]), 1.0 - f, i_ref[rows, :], jnp.log(f), s_t, sub)
        o_ref[rows, :] = _rms(o, gain_ref[...]) * jax.nn.silu(g_ref[rows, :])
    s_ref[...] = s_t


def _hgrn(proj, row_start, batch, t, heads, lb, gain, s0_t):
    chunk = math.gcd(t, HGRN_CHUNK)
    sub = min(HGRN_SUB, chunk)
    tc = min(t, 4 * chunk)
    nt = t // tc
    rb0 = row_start // tc
    dk = LANES

    def slab(k):
        return pl.BlockSpec((tc, dk), lambda b, h, i, k=k: (rb0 + b * nt + i, k * heads + h))

    vec = pl.BlockSpec((1, dk), lambda b, h, i: (0, h))
    st = pl.BlockSpec((None, None, dk, dk), lambda b, h, i: (b, h, 0, 0))
    return pl.pallas_call(
        functools.partial(_hgrn_kernel, chunk=chunk, sub=sub),
        out_shape=(jax.ShapeDtypeStruct((batch * t, heads * dk), F32),
                   jax.ShapeDtypeStruct((batch, heads, dk, dk), F32)),
        grid=(batch, heads, nt),
        in_specs=[slab(0), slab(1), slab(2), slab(3), vec, vec, st],
        out_specs=(pl.BlockSpec((tc, dk), lambda b, h, i: (b * nt + i, h)), st),
        compiler_params=_cparams("parallel", "parallel", "arbitrary"),
        name="hgrn2",
    )(proj, proj, proj, proj, lb.reshape(1, -1), gain.reshape(1, -1), s0_t)


DSA_TOPK_MAX = 256
DSA_QUERY_BLOCK = 128
INT32_MIN = -2 ** 31


def _count(m):
    return jnp.sum(jnp.where(m, 1.0, 0.0), axis=1, keepdims=True)


def _float_key(x):
    u = lax.bitcast_convert_type(x, I32)
    return u ^ ((u >> 31) & I32(0x7FFFFFFF))


KEY_NEG_INF = INT32_MIN + 0x007FFFFF
INT32_MAX = 2 ** 31 - 1


def _topk_thresholds(count, k, idx_bits, rows):
    kf = float(k)
    v = jnp.where(count(lambda key, idx: key >= 0) >= kf, I32(0), I32(INT32_MIN))

    def value_bit(it, v):
        t = v | (I32(1) << (I32(30) - it))
        return jnp.where(count(lambda key, idx: key >= t) >= kf, t, v)

    v = lax.fori_loop(0, 31, value_bit, v)
    at_least = count(lambda key, idx: key >= v)

    def break_ties():
        need = kf - count(lambda key, idx: key > v)

        def index_bit(it, j):
            t = j | (I32(1) << (I32(idx_bits - 1) - it))
            return jnp.where(count(lambda key, idx: (key == v) & (idx < t)) < need, t, j)

        return lax.fori_loop(0, idx_bits, index_bit, jnp.zeros((rows, 1), I32))

    j = lax.cond(jnp.max(at_least) > kf, break_ties, lambda: jnp.full((rows, 1), INT32_MAX, I32))
    return v, j


def _selected(key, idx, v, j):
    return ((key > v) | ((key == v) & (idx <= j))) & (key > KEY_NEG_INF)


def _topk_mask(scores, k, idx_bits):
    key = _float_key(scores)
    idx = lax.broadcasted_iota(I32, scores.shape, 1)
    v, j = _topk_thresholds(lambda pred: _count(pred(key, idx)), k, idx_bits, scores.shape[0])
    return _selected(key, idx, v, j)


def _index_scores(qi, w, ki, heads, keys_transposed=False):
    d = qi.shape[1]
    dots = (_dot(qi, ki) if keys_transposed else _dot_nt(qi, ki)) * (d ** -0.5)
    terms = jnp.maximum(dots, 0.0) * (w * (heads ** -0.5))
    r = qi.shape[0] // heads
    acc = terms[0:r]
    for h in range(1, heads):
        acc = acc + terms[h * r:(h + 1) * r]
    return acc


def _masked_softmax_pv(s, mask, v):
    s = jnp.where(mask, s, NEG_INF)
    p = jnp.exp(s - jnp.max(s, axis=1, keepdims=True))
    return _dot(p.astype(MXU_DTYPE), v) / jnp.sum(p, axis=1, keepdims=True)


PAGES_PER_STEP = 32
ONLINE_SOFTMAX_FLOOR = -1e30
KEY_BLOCK = 512


def _dsa_prompt_kernel(qi_ref, w_ref, ki_ref, q_ref, k_ref, v_ref, o_ref, *, topk, hi, kb_size):
    heads, tq, dh = q_ref.shape
    kv_heads = k_ref.shape[0]
    t = ki_ref.shape[0]
    q0 = pl.program_id(1) * tq
    n_kb = (q0 + tq - 1) // kb_size + 1

    def attend(s_keys):
        qpos = q0 + lax.broadcasted_iota(I32, (tq, s_keys), 0)
        kpos = lax.broadcasted_iota(I32, (tq, s_keys), 1)
        scores = _index_scores(qi_ref[...], w_ref[...], ki_ref[0:s_keys, :], hi)
        mask = _topk_mask(jnp.where(kpos <= qpos, scores, NEG_INF), topk, max(1, (s_keys - 1).bit_length()))
        for h in range(heads):
            g = h // (heads // kv_heads)
            s = _dot_nt(q_ref[h], k_ref[g, 0:s_keys, :]) * (dh ** -0.5)
            o_ref[h] = _masked_softmax_pv(s, mask, v_ref[g, 0:s_keys, :])

    for n in range(1, t // kb_size + 1):
        pl.when(n_kb == n)(functools.partial(attend, n * kb_size))


def _dsa_prompt(qi2, wcol, ki, q4, k4, v4, hi, tq=128):
    b, t, d = ki.shape
    _, h, _, dh = q4.shape
    hkv = k4.shape[1]
    topk = min(DSA_TOPK_MAX, t // 4)
    kb = math.gcd(t, KEY_BLOCK)
    return pl.pallas_call(
        functools.partial(_dsa_prompt_kernel, topk=topk, hi=hi, kb_size=kb),
        out_shape=jax.ShapeDtypeStruct((b, h, t, dh), F32),
        grid=(b, t // tq),
        in_specs=[pl.BlockSpec((None, None, hi * tq, d), lambda n, i: (n, i, 0, 0)),
                  pl.BlockSpec((None, None, hi * tq, 1), lambda n, i: (n, i, 0, 0)),
                  pl.BlockSpec((None, t, d), lambda n, i: (n, 0, 0)),
                  pl.BlockSpec((None, h, tq, dh), lambda n, i: (n, 0, i, 0)),
                  pl.BlockSpec((None, hkv, t, dh), lambda n, i: (n, 0, 0, 0)),
                  pl.BlockSpec((None, hkv, t, dh), lambda n, i: (n, 0, 0, 0))],
        out_specs=pl.BlockSpec((None, h, tq, dh), lambda n, i: (n, 0, i, 0)),
        compiler_params=_cparams("parallel", "arbitrary"),
        name="dsa_prompt",
    )(qi2, wcol, ki, q4, k4, v4)


def _page_specs(block, layer, pages):
    zeros = (0,) * (len(block) - 2)
    return [pl.BlockSpec(block, lambda b, c, pt, r=r: (layer, pt[b, c * pages + r]) + zeros) for r in range(pages)]


DSA_PICK_BATCH = 8


def _dsa_scores_kernel(pt_ref, qi_ref, w_ref, kn_ref, *rest, pages, t_new, hi):
    page_refs, (sc_ref, new_ref) = rest[:pages], rest[pages:]
    kc = jnp.concatenate([r[...] for r in page_refs], axis=1).astype(MXU_DTYPE)
    sc_ref[...] = _index_scores(qi_ref[...], w_ref[...], kc, hi, keys_transposed=True)

    @pl.when(pl.program_id(1) == pl.num_programs(1) - 1)
    def _():
        new = _index_scores(qi_ref[...], w_ref[...], kn_ref[...], hi)
        qpos = lax.broadcasted_iota(I32, new.shape, 0)
        kpos = lax.broadcasted_iota(I32, new.shape, 1)
        new_ref[...] = jnp.where((kpos <= qpos) & (kpos < t_new), new, NEG_INF)


def _dsa_pick_kernel(sc_ref, new_ref, mask_ref, *, topk):
    nbat, n_chunks, rows, ch = sc_ref.shape
    scores = jnp.concatenate([jnp.concatenate([sc_ref[bb, i] for i in range(n_chunks)] + [new_ref[bb]], axis=1)
                              for bb in range(nbat)], axis=0)
    mask = jnp.where(_topk_mask(scores, topk, scores.shape[1].bit_length()), 1.0, 0.0)
    pad = jnp.zeros((rows, ch - new_ref.shape[2]), F32)
    for bb in range(nbat):
        for i in range(n_chunks):
            mask_ref[bb, i] = mask[bb * rows:(bb + 1) * rows, i * ch:(i + 1) * ch]
        mask_ref[bb, n_chunks] = jnp.concatenate([mask[bb * rows:(bb + 1) * rows, n_chunks * ch:], pad], axis=1)


def _dsa_select(page_table, qi2, wcol, ki_new_pad, pool_idx, layer, topk, t_new, hi):
    b, n_pages = page_table.shape
    pages = math.gcd(n_pages, PAGES_PER_STEP)
    nc = n_pages // pages
    ch = pages * PAGE_SIZE
    _, rows, d = qi2.shape
    t = rows // hi
    grid_spec = pltpu.PrefetchScalarGridSpec(
        num_scalar_prefetch=1,
        grid=(b, nc),
        in_specs=[pl.BlockSpec((None, rows, d), lambda n, c, pt: (n, 0, 0)),
                  pl.BlockSpec((None, rows, 1), lambda n, c, pt: (n, 0, 0)),
                  pl.BlockSpec((None, LANES, d), lambda n, c, pt: (n, 0, 0))]
        + _page_specs((None, None, d, PAGE_SIZE), layer, pages),
        out_specs=(pl.BlockSpec((None, None, t, ch), lambda n, c, pt: (n, c, 0, 0)),
                   pl.BlockSpec((None, t, LANES), lambda n, c, pt: (n, 0, 0))),
    )
    scores, new = pl.pallas_call(
        functools.partial(_dsa_scores_kernel, pages=pages, t_new=t_new, hi=hi),
        out_shape=(jax.ShapeDtypeStruct((b, nc, t, ch), F32), jax.ShapeDtypeStruct((b, t, LANES), F32)),
        grid_spec=grid_spec,
        compiler_params=_cparams("parallel", "arbitrary"),
        name="dsa_sample_scores",
    )(page_table, qi2, wcol, ki_new_pad, *([pool_idx] * pages))
    nbat = math.gcd(b, DSA_PICK_BATCH)
    return pl.pallas_call(
        functools.partial(_dsa_pick_kernel, topk=topk),
        out_shape=jax.ShapeDtypeStruct((b, nc + 1, t, ch), F32),
        grid=(b // nbat,),
        in_specs=[pl.BlockSpec((nbat, nc, t, ch), lambda n: (n, 0, 0, 0)),
                  pl.BlockSpec((nbat, t, LANES), lambda n: (n, 0, 0))],
        out_specs=pl.BlockSpec((nbat, nc + 1, t, ch), lambda n: (n, 0, 0, 0)),
        compiler_params=_cparams("parallel"),
        name="dsa_sample_select",
    )(scores, new)


def _online_softmax_step(m_ref, l_ref, acc_ref, j, s, mask, v, values_transposed=False):
    m_old = m_ref[j]
    m_new = jnp.maximum(m_old, jnp.max(jnp.where(mask, s, ONLINE_SOFTMAX_FLOOR), axis=1, keepdims=True))
    alpha = jnp.exp(m_old - m_new)
    p = jnp.where(mask, jnp.exp(s - m_new), 0.0)
    l_ref[j] = alpha * l_ref[j] + jnp.sum(p, axis=1, keepdims=True)
    pv = _dot_nt(p.astype(MXU_DTYPE), v) if values_transposed else _dot(p.astype(MXU_DTYPE), v)
    acc_ref[j] = alpha * acc_ref[j] + pv
    m_ref[j] = m_new


def _online_softmax_heads(m_ref, l_ref, acc_ref, scores, mask, pv_fn):
    rows = scores[0].shape[0]
    s = jnp.concatenate(scores, axis=0)
    mask = jnp.concatenate([mask] * (s.shape[0] // mask.shape[0]), axis=0)
    m_old = m_ref[...]
    m_new = jnp.maximum(m_old, jnp.max(jnp.where(mask, s, ONLINE_SOFTMAX_FLOOR), axis=1, keepdims=True))
    alpha = jnp.exp(m_old - m_new)
    p = jnp.where(mask, jnp.exp(s - m_new), 0.0)
    l_ref[...] = alpha * l_ref[...] + jnp.sum(p, axis=1, keepdims=True)
    pb = p.astype(MXU_DTYPE)
    pv = jnp.concatenate([pv_fn(g, pb[g * rows:(g + 1) * rows]) for g in range(len(scores))], axis=0)
    acc_ref[...] = alpha * acc_ref[...] + pv
    m_ref[...] = m_new


def _dsa_attend_kernel(pt_ref, q_ref, mc_ref, mn_ref, kn_ref, vn_ref, *rest, pages):
    k_pages, v_pages = rest[:pages], rest[pages:2 * pages]
    o_ref, m_ref, l_ref, acc_ref = rest[2 * pages:]
    c = pl.program_id(1)
    kv_heads, rows, dh = q_ref.shape
    scale = dh ** -0.5

    @pl.when(c == 0)
    def _():
        m_ref[...] = jnp.full_like(m_ref, ONLINE_SOFTMAX_FLOOR)
        l_ref[...] = jnp.zeros_like(l_ref)
        acc_ref[...] = jnp.zeros_like(acc_ref)

    def keys_t(refs, g):
        return jnp.concatenate([r[g] for r in refs], axis=1).astype(MXU_DTYPE)

    scores = [_dot(q_ref[g], keys_t(k_pages, g)) * scale for g in range(kv_heads)]
    _online_softmax_heads(m_ref, l_ref, acc_ref, scores, mc_ref[...] > 0.0,
                          lambda g, p: _dot_nt(p, keys_t(v_pages, g)))

    @pl.when(c == pl.num_programs(1) - 1)
    def _():
        new = [_dot_nt(q_ref[g], kn_ref[g]) * scale for g in range(kv_heads)]
        _online_softmax_heads(m_ref, l_ref, acc_ref, new, mn_ref[:, 0:LANES] > 0.0, lambda g, p: _dot(p, vn_ref[g]))
        o_ref[...] = (acc_ref[...] / l_ref[...]).reshape(o_ref.shape)


def _dsa_attend(page_table, q4, mask, k_new_pad, v_new_pad, pool_k, pool_v, layer):
    b, n_pages = page_table.shape
    pages = math.gcd(n_pages, PAGES_PER_STEP)
    nc = n_pages // pages
    _, hkv, rows, dh = q4.shape
    _, _, t, ch = mask.shape
    grid_spec = pltpu.PrefetchScalarGridSpec(
        num_scalar_prefetch=1,
        grid=(b, nc),
        in_specs=[pl.BlockSpec((None, hkv, rows, dh), lambda n, c, pt: (n, 0, 0, 0)),
                  pl.BlockSpec((None, None, t, ch), lambda n, c, pt: (n, c, 0, 0)),
                  pl.BlockSpec((None, None, t, ch), lambda n, c, pt: (n, nc, 0, 0)),
                  pl.BlockSpec((None, hkv, LANES, dh), lambda n, c, pt: (n, 0, 0, 0)),
                  pl.BlockSpec((None, hkv, LANES, dh), lambda n, c, pt: (n, 0, 0, 0))]
        + _page_specs((None, None, hkv, dh, PAGE_SIZE), layer, pages)
        + _page_specs((None, None, hkv, dh, PAGE_SIZE), layer, pages),
        out_specs=pl.BlockSpec((None, hkv, rows, dh), lambda n, c, pt: (n, 0, 0, 0)),
        scratch_shapes=[pltpu.VMEM((hkv * rows, 1), F32), pltpu.VMEM((hkv * rows, 1), F32),
                        pltpu.VMEM((hkv * rows, dh), F32)],
    )
    return pl.pallas_call(
        functools.partial(_dsa_attend_kernel, pages=pages),
        out_shape=jax.ShapeDtypeStruct((b, hkv, rows, dh), F32),
        grid_spec=grid_spec,
        compiler_params=_cparams("parallel", "arbitrary"),
        name="dsa_sample_attend",
    )(page_table, q4, mask, mask, k_new_pad, v_new_pad, *([pool_k] * pages), *([pool_v] * pages))


GN_EPS = 64e-5
RWKV_HEAD = 64
RWKV_BATCH_BLOCK = 4


def _dot_f32(a, b):
    return jnp.dot(a, b, preferred_element_type=F32, precision=lax.Precision.HIGHEST)


def _rwkv_prep_kernel(pc_ref, prev_ref, mu_ref, vec_ref, lora_ref, gup_ref, seg_ref,
                      r_ref, w_ref, k_ref, v_ref, kk_ref, b_ref, g_ref, bonus_ref):
    cw = r_ref.shape[1]
    pc = pc_ref[...]
    xm = pc + (prev_ref[...] - pc) * mu_ref[...]
    r, kc, vc = xm[:, 0:cw], xm[:, cw:2 * cw], xm[:, 2 * cw:3 * cw]
    wa = xm[:, 3 * cw:3 * cw + LANES]
    gd = xm[:, 3 * cw + LANES:]
    lane = lax.broadcasted_iota(I32, wa.shape, 1)
    wa = jnp.where(lane < LANES // 2, jnp.tanh(wa), wa)
    lo = _dot(wa.astype(MXU_DTYPE), lora_ref[...])
    w0, a0, k_k, k_a, r_k = (vec_ref[i:i + 1, :] for i in range(5))
    w_log = -jax.nn.softplus(-(w0 + lo[:, 0:cw])) - 0.5
    a = jax.nn.sigmoid(a0 + lo[:, cw:2 * cw])
    kk = kc * k_k
    norm = jnp.sqrt(_dot_f32(kk * kk, seg_ref[...]))
    kk = kk / jnp.maximum(norm, 1e-12)
    kc = kc * (1.0 + (a - 1.0) * k_a)
    r_ref[...] = r
    w_ref[...] = jnp.exp(-jnp.exp(w_log))
    k_ref[...] = kc
    v_ref[...] = vc
    kk_ref[...] = kk
    b_ref[...] = -(kk * a)
    g_ref[...] = _dot(jax.nn.sigmoid(gd).astype(MXU_DTYPE), gup_ref[...])
    bonus_ref[...] = _dot_f32(r * kc * r_k, seg_ref[...]) * vc


def _rwkv_prep(pc, prev, mu, vecs, lora, g_up, seg, tm):
    n, width = prev.shape
    cw = vecs.shape[1]
    row = lambda w: pl.BlockSpec((tm, w), lambda i: (i, 0))
    full = lambda a: pl.BlockSpec(a.shape, lambda i: (0, 0))
    out = jax.ShapeDtypeStruct((n, cw), F32)
    return pl.pallas_call(
        _rwkv_prep_kernel,
        out_shape=(out,) * 8,
        grid=(n // tm,),
        in_specs=[row(width), row(width), full(mu), full(vecs), full(lora), full(g_up), full(seg)],
        out_specs=(row(cw),) * 8,
        compiler_params=_cparams("parallel"),
        name="rwkv_prep",
    )(pc, prev, mu, vecs, lora, g_up, seg)


def _segment_sum(x, seg):
    hi = x.astype(MXU_DTYPE)
    lo = (x - hi.astype(F32)).astype(MXU_DTYPE)
    return _dot(hi, seg) + _dot(lo, seg)


def _rwkv_scan_kernel(*refs, nb):
    ins, (seg_ref, s0_ref, y_ref, s_ref) = refs[:6 * nb], refs[6 * nb:]

    @pl.when(pl.program_id(1) == 0)
    def _():
        s_ref[...] = s0_ref[...]

    _, rows, width = s_ref.shape
    pairs = rows // RWKV_HEAD
    stacks = 1
    per = nb // stacks
    lane = lax.broadcasted_iota(I32, (per * rows, width), 1)
    sub = lax.broadcasted_iota(I32, (per * rows, width), 0)
    diag = (sub & (RWKV_HEAD - 1)) == (lane & (RWKV_HEAD - 1))
    own = (lax.broadcasted_iota(I32, (2, width), 0) == 0) == (lax.broadcasted_iota(I32, (2, width), 1) < RWKV_HEAD)
    seg = seg_ref[...]

    def group(g, carry):
        t0 = pl.multiple_of(g * SUBLANES, SUBLANES)
        r8, w8, k8, v8, kk8, nb8 = ([ins[6 * bb + q][pl.ds(t0, SUBLANES), :] for bb in range(nb)] for q in range(6))
        states = [s_ref[c * per:(c + 1) * per].reshape(per * rows, width) for c in range(stacks)]
        for i in range(SUBLANES):
            for c in range(stacks):
                members = range(c * per, (c + 1) * per)

                def per_row(x8):
                    return jnp.concatenate([jnp.broadcast_to(x8[bb][i:i + 1, p * width:(p + 1) * width], (RWKV_HEAD, width))
                                            for bb in members for p in range(pairs)], axis=0)
                s = states[c]
                sa = _segment_sum(s * per_row(kk8), seg)
                vcol = _segment_sum(jnp.where(diag, per_row(v8), 0.0), seg)
                s = s * per_row(w8) + sa * per_row(nb8) + vcol * per_row(k8)
                states[c] = s
                sb = s.astype(MXU_DTYPE)
                for j, bb in enumerate(members):
                    for p in range(pairs):
                        r2 = jnp.where(own, r8[bb][i:i + 1, p * width:(p + 1) * width], 0.0)
                        r0 = (j * pairs + p) * RWKV_HEAD
                        y = _dot_nt(r2.astype(MXU_DTYPE), sb[r0:r0 + RWKV_HEAD])
                        y_ref[bb, pl.ds(t0 + i, 1), 2 * p:2 * p + 2, :] = y[None]
        for c in range(stacks):
            s_ref[c * per:(c + 1) * per] = states[c].reshape(per, rows, width)
        return carry

    lax.fori_loop(0, ins[0].shape[0] // SUBLANES, group, 0)


def _rwkv_scan(ins, row_start, batch, t, s0_packed, seg2, nb):
    cw = ins[0].shape[1]
    heads = cw // RWKV_HEAD
    tb = min(t, 64)
    nt = t // tb
    rb0 = row_start // tb
    rows = [pl.BlockSpec((tb, cw), lambda b, i, bb=bb: (rb0 + (b * nb + bb) * nt + i, 0)) for bb in range(nb)]
    st = pl.BlockSpec((nb,) + s0_packed.shape[1:], lambda b, i: (b, 0, 0))
    return pl.pallas_call(
        functools.partial(_rwkv_scan_kernel, nb=nb),
        out_shape=(jax.ShapeDtypeStruct((batch, t, heads, RWKV_HEAD), F32),
                   jax.ShapeDtypeStruct(s0_packed.shape, F32)),
        grid=(batch // nb, nt),
        in_specs=[spec for spec in rows for _ in range(6)] + [pl.BlockSpec(seg2.shape, lambda b, i: (0, 0)), st],
        out_specs=(pl.BlockSpec((nb, tb, heads, RWKV_HEAD), lambda b, i: (b, i, 0, 0)), st),
        compiler_params=_cparams("parallel", "arbitrary"),
        name="rwkv_scan",
    )(*(list(ins) * nb), seg2, s0_packed)


def _rwkv_post_kernel(y_ref, bonus_ref, g_ref, ln_ref, seg_ref, o_ref):
    y = y_ref[...]
    avg = seg_ref[...] * (1.0 / RWKV_HEAD)
    d = y - _dot_f32(y, avg)
    var = _dot_f32(d * d, avg)
    yn = d * lax.rsqrt(var + GN_EPS) * ln_ref[0:1, :] + ln_ref[1:2, :]
    o_ref[...] = (yn + bonus_ref[...]) * g_ref[...]


def _rwkv_post(y, bonus, g, ln, seg, row_start, tm):
    n, cw = y.shape
    rb0 = row_start // tm
    row = pl.BlockSpec((tm, cw), lambda i: (i, 0))
    off = pl.BlockSpec((tm, cw), lambda i: (rb0 + i, 0))
    full = lambda a: pl.BlockSpec(a.shape, lambda i: (0, 0))
    return pl.pallas_call(
        _rwkv_post_kernel,
        out_shape=jax.ShapeDtypeStruct((n, cw), F32),
        grid=(n // tm,),
        in_specs=[row, off, off, full(ln), full(seg)],
        out_specs=row,
        compiler_params=_cparams("parallel"),
        name="rwkv_post",
    )(y, bonus, g, ln, seg)


MLA_HEADS = 8
MLA_NOPE = 64
MLA_ROPE = 32
MLA_SCALE = (MLA_NOPE + MLA_ROPE) ** -0.5


def _rope_tile(x, cos, sin):
    lane = lax.broadcasted_iota(I32, x.shape, 1)
    half = MLA_ROPE // 2
    rot = jnp.where(lane < half, pltpu.roll(x, LANES - half, 1), pltpu.roll(x, half, 1))
    return x * cos + rot * sin


def _mla_prep_kernel(qd_ref, ckv_ref, kr_ref, cos_ref, sin_ref, qn_ref, kvn_ref, wuq_ref, wuk_ref,
                     ql_ref, qr_ref, c_ref, krn_ref):
    cq = _rms(qd_ref[...], qn_ref[...])
    qh = _dot(cq.astype(MXU_DTYPE), wuq_ref[...])
    nope = MLA_HEADS * MLA_NOPE
    ql_ref[...] = _dot(qh[:, 0:nope].astype(MXU_DTYPE), wuk_ref[...]).astype(ql_ref.dtype)
    cos, sin = cos_ref[...], sin_ref[...]
    for h in range(MLA_HEADS):
        lanes = slice(nope + h * LANES, nope + (h + 1) * LANES)
        qr_ref[:, h * LANES:(h + 1) * LANES] = _rope_tile(qh[:, lanes], cos, sin).astype(qr_ref.dtype)
    c_ref[...] = _rms(ckv_ref[...], kvn_ref[...])
    krn_ref[...] = _rope_tile(kr_ref[...], cos, sin)


def _mla_prep(proj, col_blocks, row_start, nrows, cos, sin, q_norm, kv_norm, wuq, wuk, tm):
    qd0, ckv0, kr0 = col_blocks
    d_q, d_kv = q_norm.shape[1], kv_norm.shape[1]
    rb0 = row_start // tm
    nper = cos.shape[0] // tm
    full = lambda a: pl.BlockSpec(a.shape, lambda i: (0, 0))
    rows = lambda w: pl.BlockSpec((tm, w), lambda i: (i, 0))
    tab = pl.BlockSpec((tm, LANES), lambda i: (i % nper, 0))
    return pl.pallas_call(
        _mla_prep_kernel,
        out_shape=(jax.ShapeDtypeStruct((nrows, MLA_HEADS * d_kv), MXU_DTYPE),
                   jax.ShapeDtypeStruct((nrows, MLA_HEADS * LANES), MXU_DTYPE),
                   jax.ShapeDtypeStruct((nrows, d_kv), F32),
                   jax.ShapeDtypeStruct((nrows, LANES), F32)),
        grid=(nrows // tm,),
        in_specs=[pl.BlockSpec((tm, d_q), lambda i: (rb0 + i, qd0 * LANES // d_q)),
                  pl.BlockSpec((tm, d_kv), lambda i: (rb0 + i, ckv0 * LANES // d_kv)),
                  pl.BlockSpec((tm, LANES), lambda i: (rb0 + i, kr0)),
                  tab, tab, full(q_norm), full(kv_norm), full(wuq), full(wuk)],
        out_specs=(rows(MLA_HEADS * d_kv), rows(MLA_HEADS * LANES), rows(d_kv), rows(LANES)),
        compiler_params=_cparams("parallel"),
        name="mla_prep",
    )(proj, proj, proj, cos, sin, q_norm, kv_norm, wuq, wuk)


def _mla_prompt_kernel(ql_ref, qr_ref, c_ref, kr_ref, wuv_ref, o_ref, m_ref, l_ref, acc_ref, *, kb_size):
    tq = ql_ref.shape[0]
    d_kv = c_ref.shape[1]
    q0 = pl.program_id(1) * tq
    n_kb = (q0 + tq - 1) // kb_size + 1
    qpos = q0 + lax.broadcasted_iota(I32, (tq, kb_size), 0)
    lane = lax.broadcasted_iota(I32, (tq, kb_size), 1)
    m_ref[...] = jnp.full_like(m_ref, ONLINE_SOFTMAX_FLOOR)
    l_ref[...] = jnp.zeros_like(l_ref)
    acc_ref[...] = jnp.zeros_like(acc_ref)

    def block(kb, carry):
        k0 = pl.multiple_of(kb * kb_size, kb_size)
        c, kr = c_ref[pl.ds(k0, kb_size), :], kr_ref[pl.ds(k0, kb_size), :]
        scores = [(_dot_nt(ql_ref[:, h * d_kv:(h + 1) * d_kv], c)
                   + _dot_nt(qr_ref[:, h * LANES:(h + 1) * LANES], kr)) * MLA_SCALE for h in range(MLA_HEADS)]
        _online_softmax_heads(m_ref, l_ref, acc_ref, scores, k0 + lane <= qpos, lambda g, p: _dot(p, c))
        return carry

    lax.fori_loop(0, n_kb, block, 0)
    o_lat = (acc_ref[...] / l_ref[...]).astype(MXU_DTYPE)
    o_lat = jnp.concatenate([o_lat[h * tq:(h + 1) * tq] for h in range(MLA_HEADS)], axis=1)
    o_ref[...] = _dot(o_lat, wuv_ref[...])


def _mla_prompt(q_lat, q_rope, c, kr, wuv, batch, t, tq=128):
    d_kv = c.shape[1]
    nq = t // tq
    rows = MLA_HEADS * tq
    return pl.pallas_call(
        functools.partial(_mla_prompt_kernel, kb_size=math.gcd(t, KEY_BLOCK)),
        scratch_shapes=[pltpu.VMEM((rows, 1), F32), pltpu.VMEM((rows, 1), F32), pltpu.VMEM((rows, d_kv), F32)],
        out_shape=jax.ShapeDtypeStruct((batch * t, wuv.shape[1]), F32),
        grid=(batch, nq),
        in_specs=[pl.BlockSpec((tq, MLA_HEADS * d_kv), lambda b, i: (b * nq + i, 0)),
                  pl.BlockSpec((tq, MLA_HEADS * LANES), lambda b, i: (b * nq + i, 0)),
                  pl.BlockSpec((t, d_kv), lambda b, i: (b, 0)),
                  pl.BlockSpec((t, LANES), lambda b, i: (b, 0)),
                  pl.BlockSpec(wuv.shape, lambda b, i: (0, 0))],
        out_specs=pl.BlockSpec((tq, wuv.shape[1]), lambda b, i: (b * nq + i, 0)),
        compiler_params=_cparams("parallel", "arbitrary"),
        name="mla_prompt",
    )(q_lat, q_rope, c, kr, wuv)


def _mla_sample_kernel(pt_ref, ql_ref, qr_ref, cn_ref, krn_ref, *rest, pages, t_new):
    c_pages, kr_pages = rest[:pages], rest[pages:2 * pages]
    o_ref, m_ref, l_ref, acc_ref = rest[2 * pages:]
    step = pl.program_id(1)

    @pl.when(step == 0)
    def _():
        m_ref[...] = jnp.full_like(m_ref, ONLINE_SOFTMAX_FLOOR)
        l_ref[...] = jnp.zeros_like(l_ref)
        acc_ref[...] = jnp.zeros_like(acc_ref)

    ql, qr = ql_ref[...], qr_ref[...]
    cc = jnp.concatenate([r[...] for r in c_pages], axis=0).astype(MXU_DTYPE)
    kc = jnp.concatenate([r[...] for r in kr_pages], axis=1).astype(MXU_DTYPE)
    s = (_dot_nt(ql, cc) + _dot(qr[:, 0:MLA_ROPE], kc)) * MLA_SCALE
    _online_softmax_step(m_ref, l_ref, acc_ref, 0, s, jnp.full(s.shape, True), cc)

    @pl.when(step == pl.num_programs(1) - 1)
    def _():
        cn = cn_ref[...]
        s_new = (_dot_nt(ql, cn) + _dot_nt(qr, krn_ref[...])) * MLA_SCALE
        qpos = lax.broadcasted_iota(I32, s_new.shape, 0) % t_new
        kpos = lax.broadcasted_iota(I32, s_new.shape, 1)
        _online_softmax_step(m_ref, l_ref, acc_ref, 0, s_new, (kpos <= qpos) & (kpos < t_new), cn)
        o_ref[...] = acc_ref[0] / l_ref[0]


def _mla_sample(page_table, q_lat, q_rope, c_new_pad, kr_new_pad, pool_c, pool_kr, layer, t_new):
    b, n_pages = page_table.shape
    pages = math.gcd(n_pages, PAGES_PER_STEP)
    _, rows, d_kv = q_lat.shape
    one = lambda a: pl.BlockSpec((None,) + a.shape[1:], lambda n, c, pt: (n, 0, 0))
    grid_spec = pltpu.PrefetchScalarGridSpec(
        num_scalar_prefetch=1,
        grid=(b, n_pages // pages),
        in_specs=[one(q_lat), one(q_rope), one(c_new_pad), one(kr_new_pad)]
        + _page_specs((None, None, PAGE_SIZE, d_kv), layer, pages)
        + _page_specs((None, None, MLA_ROPE, PAGE_SIZE), layer, pages),
        out_specs=pl.BlockSpec((None, rows, d_kv), lambda n, c, pt: (n, 0, 0)),
        scratch_shapes=[pltpu.VMEM((1, rows, 1), F32), pltpu.VMEM((1, rows, 1), F32),
                        pltpu.VMEM((1, rows, d_kv), F32)],
    )
    return pl.pallas_call(
        functools.partial(_mla_sample_kernel, pages=pages, t_new=t_new),
        out_shape=jax.ShapeDtypeStruct((b, rows, d_kv), F32),
        grid_spec=grid_spec,
        compiler_params=_cparams("parallel", "arbitrary"),
        name="mla_sample",
    )(page_table, q_lat, q_rope, c_new_pad, kr_new_pad, *([pool_c] * pages), *([pool_kr] * pages))


ROW_BLOCK = 256
A_HEADS = 4
B_HEADS, B_KV_HEADS, B_DH = 8, 4, 64
IDX_HEADS, IDX_DIM = 8, 64
C_WIDTH = 512
RWKV_IN = 3 * C_WIDTH + 64 + 64 + 128
D_Q_RANK, D_KV_RANK = 384, 256
ODD_COLS = RWKV_IN + D_KV_RANK + 2 * LANES + D_Q_RANK
ODD_SLABS = ((RWKV_IN + D_KV_RANK + 2 * LANES) // LANES, RWKV_IN // LANES, (RWKV_IN + D_KV_RANK) // LANES)


def _pad_rows(a, rows=LANES):
    return jnp.pad(a, ((0, 0),) * (a.ndim - 2) + ((0, rows - a.shape[-2]), (0, 0)))


def _head_major(a, b, t, h):
    return jnp.transpose(a.reshape(b, t, h, -1), (0, 2, 1, 3))


def _token_major(a):
    b, h, t, w = a.shape
    return jnp.transpose(a, (0, 2, 1, 3)).reshape(b * t, h * w)


def _rope_tables(pos):
    half = MLA_ROPE // 2
    inv = ROPE_THETA ** (-jnp.arange(0, MLA_ROPE, 2, dtype=F32) / MLA_ROPE)
    ang = pos.astype(F32)[:, None] * inv[None, :]
    zeros = jnp.zeros((pos.shape[0], LANES - 2 * half), F32)
    cos, sin = jnp.cos(ang), jnp.sin(ang)
    return jnp.concatenate([cos, cos, zeros], axis=1), jnp.concatenate([-sin, sin, zeros], axis=1)


def kernel(x_prompt, x_sample, cache_dsa_k, cache_dsa_v, cache_dsa_idx, cache_mla_ckv, cache_mla_krope, state_hgrn, state_rwkv, state_shift, page_table, norm_mix, norm_ffn, norm_final, w_in_even, w_out_even, hgrn_lb, hgrn_norm, w_in_odd, w_out_odd, rwkv_mu, rwkv_w0, rwkv_w_up, rwkv_a0, rwkv_a_up, rwkv_g_up, rwkv_k_k, rwkv_k_a, rwkv_r_k, rwkv_ln_w, rwkv_ln_b, mla_q_norm, mla_w_uq, mla_kv_norm, mla_w_uk, mla_w_uv, peer_wq, peer_subkeys, peer_u, peer_v):
    bp, tp, d_model = x_prompt.shape
    bs, ts, _ = x_sample.shape
    n_p, n_s = bp * tp, bs * ts
    n_all = n_p + n_s
    n_pad = _round_up(n_all, PEER_TOKEN_BLOCK)
    tm_s = min(ROW_BLOCK, n_s)
    n_past = page_table.shape[1] * PAGE_SIZE
    depth = norm_mix.shape[0]
    md = MXU_DTYPE

    def all_rows(p, s):
        return jnp.concatenate([p, s, jnp.zeros((n_pad - n_all, p.shape[1]), p.dtype)], axis=0)

    x = all_rows(x_prompt.reshape(n_p, d_model), x_sample.reshape(n_s, d_model))

    lb_cum = jnp.cumsum(jax.nn.softmax(hgrn_lb.astype(F32), axis=0), axis=0)
    lower_bounds = lb_cum - lb_cum[:1]
    seg = jnp.kron(jnp.eye(C_WIDTH // RWKV_HEAD, dtype=F32), jnp.ones((RWKV_HEAD, RWKV_HEAD), F32))
    pool_dsa_k = jnp.transpose(cache_dsa_k, (0, 1, 3, 4, 2))
    pool_dsa_v = jnp.transpose(cache_dsa_v, (0, 1, 3, 4, 2))
    pool_dsa_idx = jnp.transpose(cache_dsa_idx, (0, 1, 3, 2))
    pool_mla_kr = jnp.transpose(cache_mla_krope, (0, 1, 3, 2))
    eye_h = jnp.eye(MLA_HEADS, dtype=F32)
    cos_p, sin_p = _rope_tables(jnp.arange(tp))
    cos_s, sin_s = (jnp.tile(a, (tm_s // ts, 1)) for a in _rope_tables(n_past + jnp.arange(ts)))

    outs = {k: [] for k in ("pk", "pv", "pi", "pc", "pr", "ph", "ps", "psh", "sk", "sv", "si", "sc", "sr", "sh", "ss", "ssh")}

    for l in range(depth):
        j = l // 2
        if l % 2 == 0:
            w_in = jnp.pad(w_in_even[j], ((0, 0), (0, _round_up(w_in_even.shape[2], LANES) - w_in_even.shape[2]))).astype(md)
            proj = _matmul(x, w_in, g=norm_mix[l], tm=ROW_BLOCK)
            aw = A_HEADS * LANES
            c_q, c_k, c_v = 4 * aw, 4 * aw + B_HEADS * B_DH, 4 * aw + (B_HEADS + B_KV_HEADS) * B_DH
            c_qi = c_v + B_KV_HEADS * B_DH
            c_ki = c_qi + IDX_HEADS * IDX_DIM
            c_wi = c_ki + IDX_DIM
            mixes = []
            for rows, b, t, s0_t, kk, kv, ki_key, kh in ((slice(0, n_p), bp, tp, None, "pk", "pv", "pi", "ph"),
                                                    (slice(n_p, n_all), bs, ts, state_hgrn[j], "sk", "sv", "si", "sh")):
                sample = s0_t is not None
                s0_t = jnp.swapaxes(s0_t, -1, -2).astype(F32) if sample else jnp.zeros((b, A_HEADS, LANES, LANES), F32)
                oa, s_a = _hgrn(proj, rows.start, b, t, A_HEADS, lower_bounds[j], hgrn_norm[j], s0_t)
                pr = proj[rows]
                qb, kb, vb = pr[:, c_q:c_k], pr[:, c_k:c_v], pr[:, c_v:c_qi]
                qi, ki, wi = pr[:, c_qi:c_ki], pr[:, c_ki:c_wi], pr[:, c_wi:c_wi + IDX_HEADS]
                q4 = _head_major(qb, b, t, B_HEADS).astype(md)
                k4 = _head_major(kb, b, t, B_KV_HEADS).astype(md)
                v4 = _head_major(vb, b, t, B_KV_HEADS).astype(md)
                ki3 = ki.reshape(b, t, IDX_DIM).astype(md)
                tq = t if sample else DSA_QUERY_BLOCK
                qi2 = jnp.transpose(qi.reshape(b, t // tq, tq, IDX_HEADS, IDX_DIM), (0, 1, 3, 2, 4))
                qi2 = qi2.reshape(b, t // tq, IDX_HEADS * tq, IDX_DIM).astype(md)
                wcol = jnp.transpose(wi.reshape(b, t // tq, tq, IDX_HEADS), (0, 1, 3, 2)).reshape(b, t // tq, IDX_HEADS * tq, 1)
                if sample:
                    topk = min(DSA_TOPK_MAX, (n_past + t) // 4)
                    mask = _dsa_select(page_table, qi2[:, 0], wcol[:, 0], _pad_rows(ki3), pool_dsa_idx, j, topk, t, IDX_HEADS)
                    group = B_HEADS // B_KV_HEADS
                    o4 = _dsa_attend(page_table, q4.reshape(b, B_KV_HEADS, group * t, B_DH), mask,
                                     _pad_rows(k4), _pad_rows(v4), pool_dsa_k, pool_dsa_v, j)
                    o4 = o4.reshape(b, B_HEADS, t, B_DH)
                else:
                    o4 = _dsa_prompt(qi2, wcol, ki3, q4, k4, v4, IDX_HEADS, tq)
                mixes.append(jnp.concatenate([oa, _token_major(o4)], axis=1))
                outs[kk].append(kb.reshape(b, t, B_KV_HEADS, B_DH))
                outs[kv].append(vb.reshape(b, t, B_KV_HEADS, B_DH))
                outs[ki_key].append(ki.reshape(b, t, IDX_DIM))
                outs[kh].append(jnp.swapaxes(s_a, -1, -2))
            x = _matmul(all_rows(*mixes), w_out_even[j].astype(md), res=x, tm=ROW_BLOCK)
        else:
            w = w_in_odd[j]
            c_qd, c_ckv = RWKV_IN, RWKV_IN + D_Q_RANK
            c_kr = c_ckv + D_KV_RANK
            w_in = jnp.concatenate([w[:, :RWKV_IN], w[:, c_ckv:c_kr], w[:, c_kr:c_kr + MLA_ROPE],
                                    jnp.zeros((d_model, 2 * LANES - MLA_ROPE), w.dtype), w[:, c_qd:c_ckv]], axis=1).astype(md)
            proj = _matmul(x, w_in, g=norm_mix[l], tm=ROW_BLOCK)
            pc_p = proj[:n_p, :RWKV_IN].reshape(bp, tp, RWKV_IN)
            pc_s = proj[n_p:n_all, :RWKV_IN].reshape(bs, ts, RWKV_IN)
            prev_p = jnp.concatenate([jnp.zeros((bp, 1, RWKV_IN), F32), pc_p[:, :-1]], axis=1)
            prev_s = jnp.concatenate([state_shift[j].astype(F32)[:, None], pc_s[:, :-1]], axis=1)
            prev = all_rows(prev_p.reshape(n_p, RWKV_IN), prev_s.reshape(n_s, RWKV_IN))
            zeros_l = jnp.zeros((rwkv_w_up.shape[1], C_WIDTH), F32)
            lora = jnp.concatenate([jnp.concatenate([rwkv_w_up[j], zeros_l], axis=1),
                                    jnp.concatenate([zeros_l, rwkv_a_up[j]], axis=1)], axis=0).astype(md)
            vecs = jnp.stack([rwkv_w0[j], rwkv_a0[j], rwkv_k_k[j], rwkv_k_a[j], rwkv_r_k[j].reshape(-1)]).astype(F32)
            prep = _rwkv_prep(proj, prev, rwkv_mu[j].reshape(1, -1).astype(F32), vecs, lora, rwkv_g_up[j].astype(md), seg, ROW_BLOCK)
            scan_in, g_all, bonus_all = prep[:6], prep[6], prep[7]
            ln = jnp.stack([rwkv_ln_w[j], rwkv_ln_b[j]]).astype(F32)

            wuq = mla_w_uq[j].reshape(D_Q_RANK, MLA_HEADS, MLA_NOPE + MLA_ROPE)
            wuq_rope = jnp.pad(wuq[:, :, MLA_NOPE:], ((0, 0), (0, 0), (0, LANES - MLA_ROPE))).reshape(D_Q_RANK, -1)
            wuq_p = jnp.concatenate([wuq[:, :, :MLA_NOPE].reshape(D_Q_RANK, -1), wuq_rope], axis=1).astype(md)
            wuk = jnp.einsum("chn,hg->hngc", mla_w_uk[j], eye_h).reshape(MLA_HEADS * MLA_NOPE, -1).astype(md)
            wuv = jnp.einsum("chv,hg->hcgv", mla_w_uv[j], eye_h).reshape(MLA_HEADS * D_KV_RANK, -1).astype(md)
            q_norm = mla_q_norm[j].reshape(1, -1).astype(F32)
            kv_norm = mla_kv_norm[j].reshape(1, -1).astype(F32)

            mixes = []
            for start, b, t, tm, cos, sin, sample in ((0, bp, tp, ROW_BLOCK, cos_p, sin_p, False),
                                                      (n_p, bs, ts, tm_s, cos_s, sin_s, True)):
                n = b * t
                if sample:
                    heads = C_WIDTH // RWKV_HEAD
                    s0 = state_rwkv[j].astype(F32).reshape(b, heads // 2, 2, RWKV_HEAD, RWKV_HEAD)
                    s0 = jnp.transpose(s0, (0, 1, 3, 2, 4)).reshape(b, heads // 2 * RWKV_HEAD, 2 * RWKV_HEAD)
                else:
                    s0 = jnp.zeros((b, C_WIDTH // LANES * RWKV_HEAD, LANES), F32)
                nb = math.gcd(b, RWKV_BATCH_BLOCK)
                y3, s_c = _rwkv_scan(scan_in, start, b, t, s0, seg[:LANES, :LANES].astype(md), nb)
                yc = _rwkv_post(y3.reshape(n, C_WIDTH), bonus_all, g_all, ln, seg, start, tm)
                s_c = jnp.transpose(s_c.reshape(b, -1, RWKV_HEAD, 2, RWKV_HEAD), (0, 1, 3, 2, 4))
                s_c = s_c.reshape(b, -1, RWKV_HEAD, RWKV_HEAD)

                q_lat, q_rope, c_new, kr_new = _mla_prep(proj, ODD_SLABS, start, n, cos, sin, q_norm, kv_norm, wuq_p, wuk, tm)
                if sample:
                    hq = lambda a: _head_major(a, b, t, MLA_HEADS).reshape(b, MLA_HEADS * t, -1)
                    o_lat = _mla_sample(page_table, hq(q_lat), hq(q_rope),
                                        _pad_rows(c_new.reshape(b, t, -1)).astype(md),
                                        _pad_rows(kr_new.reshape(b, t, -1)).astype(md),
                                        cache_mla_ckv, pool_mla_kr, j, t)
                    o_lat = _token_major(o_lat.reshape(b, MLA_HEADS, t, -1))
                    od = _matmul(o_lat, wuv, tm=tm)
                else:
                    od = _mla_prompt(q_lat, q_rope, c_new.astype(md), kr_new.astype(md), wuv, b, t)
                mixes.append(jnp.concatenate([yc, od], axis=1))
                pre = "s" if sample else "p"
                outs[pre + "c"].append(c_new.reshape(b, t, D_KV_RANK))
                outs[pre + "r"].append(kr_new[:, :MLA_ROPE].reshape(b, t, MLA_ROPE))
                outs[pre + "s"].append(s_c)
                outs[pre + "sh"].append((pc_s if sample else pc_p)[:, -1])
            x = _matmul(all_rows(*mixes), w_out_odd[j].astype(md), res=x, tm=ROW_BLOCK)
        x = _peer_ffn(x, norm_ffn[l], _peer_prepare(peer_wq[l], peer_subkeys[l], peer_u[l], peer_v[l]))

    y = _rmsnorm(x, norm_final)
    y_prompt = y[:n_p].reshape(bp, tp, d_model)
    y_sample = y[n_p:n_all].reshape(bs, ts, d_model)
    st = lambda k: jnp.stack(outs[k])
    return (y_prompt, y_sample,
            st("pk"), st("pv"), st("pi"), st("pc"), st("pr"), st("ph"), st("ps"), st("psh"),
            st("sk"), st("sv"), st("si"), st("sc"), st("sr"), st("sh"), st("ss"), st("ssh"))
```

```python
import functools
import math

import jax
import jax.numpy as jnp
from jax import lax
from jax.experimental import pallas as pl
from jax.experimental.pallas import tpu as pltpu

F32 = jnp.float32
I32 = jnp.int32
MXU_DTYPE = jnp.bfloat16
GATE_DTYPE = jnp.bfloat16
LANES = 128
SUBLANES = 8
VMEM_LIMIT_BYTES = 56 * 1024 * 1024
NEG_INF = float("-inf")

RMS_EPS = 1e-6
PAGE_SIZE = 128
ROPE_THETA = 10000.0
PEER_HEADS = 8
PEER_NKEYS = 128
PEER_TOPK = 16
PEER_TOKEN_BLOCK = 512
PEER_GROUP = 16


def _cparams(*sem):
    return pltpu.CompilerParams(dimension_semantics=sem, vmem_limit_bytes=VMEM_LIMIT_BYTES)


def _round_up(n, m):
    return -(-n // m) * m


def _rms(x, g):
    return x * lax.rsqrt(jnp.mean(x * x, axis=-1, keepdims=True) + RMS_EPS) * g


def _dot(a, b):
    return jnp.dot(a, b, preferred_element_type=F32)


def _dot_nt(a, b):
    return lax.dot_general(a, b, (((1,), (1,)), ((), ())), preferred_element_type=F32)


def _matmul_kernel(*refs, norm, residual):
    it = iter(refs)
    a_ref = next(it)
    g_ref = next(it) if norm else None
    w_ref = next(it)
    r_ref = next(it) if residual else None
    o_ref = next(it)
    a = a_ref[...]
    if norm:
        a = _rms(a, g_ref[...])
    acc = _dot(a.astype(w_ref.dtype), w_ref[...])
    if residual:
        acc = acc + r_ref[...]
    o_ref[...] = acc


def _matmul(a, w, g=None, res=None, tm=256):
    n, k = a.shape
    m = w.shape[1]
    ins, specs = [a], [pl.BlockSpec((tm, k), lambda i: (i, 0))]
    if g is not None:
        ins.append(g.reshape(1, k).astype(F32))
        specs.append(pl.BlockSpec((1, k), lambda i: (0, 0)))
    ins.append(w)
    specs.append(pl.BlockSpec((k, m), lambda i: (0, 0)))
    if res is not None:
        ins.append(res)
        specs.append(pl.BlockSpec((tm, m), lambda i: (i, 0)))
    return pl.pallas_call(
        functools.partial(_matmul_kernel, norm=g is not None, residual=res is not None),
        out_shape=jax.ShapeDtypeStruct((n, m), F32),
        grid=(n // tm,),
        in_specs=specs,
        out_specs=pl.BlockSpec((tm, m), lambda i: (i, 0)),
        compiler_params=_cparams("parallel"),
        name="proj_matmul",
    )(*ins)


def _rmsnorm_kernel(x_ref, g_ref, o_ref):
    o_ref[...] = _rms(x_ref[...], g_ref[...])


def _rmsnorm(x, g, tm=512):
    n, d = x.shape
    return pl.pallas_call(
        _rmsnorm_kernel,
        out_shape=jax.ShapeDtypeStruct((n, d), F32),
        grid=(n // tm,),
        in_specs=[pl.BlockSpec((tm, d), lambda i: (i, 0)), pl.BlockSpec((1, d), lambda i: (0, 0))],
        out_specs=pl.BlockSpec((tm, d), lambda i: (i, 0)),
        compiler_params=_cparams("parallel"),
        name="final_rmsnorm",
    )(x, g.reshape(1, d).astype(F32))


def _peer_fold_kernel(sub_ref, wq_ref, o_ref):
    o_ref[...] = _dot_nt(sub_ref[...], wq_ref[...])


def _peer_fold(subkeys, wq):
    d_model = wq.shape[0]
    half = subkeys.shape[2]
    return pl.pallas_call(
        _peer_fold_kernel,
        out_shape=jax.ShapeDtypeStruct((2, PEER_HEADS, PEER_NKEYS, d_model), F32),
        grid=(2, PEER_HEADS),
        in_specs=[pl.BlockSpec((None, PEER_NKEYS, half), lambda p, h: (p, 0, 0)),
                  pl.BlockSpec((d_model, half), lambda p, h: (0, h * 2 + p))],
        out_specs=pl.BlockSpec((None, None, PEER_NKEYS, d_model), lambda p, h: (p, h, 0, 0)),
        compiler_params=_cparams("parallel", "parallel"),
        name="peer_fold",
    )(subkeys.astype(MXU_DTYPE), wq.astype(MXU_DTYPE))


def _peer_scores_kernel(x_ref, g_ref, m_ref, s_ref, h_ref):
    hb = _rms(x_ref[...], g_ref[...]).astype(MXU_DTYPE)
    s_ref[...] = _dot_nt(m_ref[...], hb)
    h_ref[...] = hb.T


def _peer_scores(x, g, mcat, tm=PEER_TOKEN_BLOCK):
    n, d = x.shape
    rows = mcat.shape[0]
    return pl.pallas_call(
        _peer_scores_kernel,
        out_shape=(jax.ShapeDtypeStruct((rows, n), F32), jax.ShapeDtypeStruct((d, n), MXU_DTYPE)),
        grid=(n // tm,),
        in_specs=[pl.BlockSpec((tm, d), lambda i: (i, 0)),
                  pl.BlockSpec((1, d), lambda i: (0, 0)),
                  pl.BlockSpec((rows, d), lambda i: (0, 0))],
        out_specs=(pl.BlockSpec((rows, tm), lambda i: (0, i)), pl.BlockSpec((d, tm), lambda i: (0, i))),
        compiler_params=_cparams("parallel"),
        name="peer_scores",
    )(x, g.reshape(1, d).astype(F32), mcat)


def _bitonic_merge_desc(v):
    n = len(v)
    if n == 1:
        return v
    half = n // 2
    hi = [jnp.maximum(v[i], v[i + half]) for i in range(half)]
    lo = [jnp.minimum(v[i], v[i + half]) for i in range(half)]
    return _bitonic_merge_desc(hi) + _bitonic_merge_desc(lo)


def _sort_desc(v):
    n = len(v)
    if n == 1:
        return v
    return _bitonic_merge_desc(_sort_desc(v[:n // 2]) + _sort_desc(v[n // 2:])[::-1])


def _merge_top(x, y):
    n = len(x)
    return _bitonic_merge_desc([jnp.maximum(x[i], y[n - 1 - i]) for i in range(n)])


def _peer_topk_kernel(s_ref, c1_ref, e1_ref, r2_ref, e2_ref):
    hk = PEER_HEADS * PEER_NKEYS
    tn = s_ref.shape[1]

    def top_values(base):
        runs = []
        for r0 in range(0, PEER_NKEYS, PEER_TOPK):
            rows = [s_ref[base + (r0 + i) * PEER_HEADS:base + (r0 + i + 1) * PEER_HEADS, :] for i in range(PEER_TOPK)]
            runs.append(_sort_desc(rows))
        while len(runs) > 1:
            runs = [_merge_top(runs[i], runs[i + 1]) for i in range(0, len(runs), 2)]
        return runs[0]

    a = top_values(0)
    b = top_values(hk)
    neg = jnp.full((PEER_HEADS, tn), NEG_INF, F32)
    lists = [[a[k] + b[l] if (k + 1) * (l + 1) <= PEER_TOPK else neg for l in range(PEER_TOPK)] for k in range(PEER_TOPK)]
    while len(lists) > 1:
        lists = [_merge_top(lists[i], lists[i + 1]) for i in range(0, len(lists), 2)]
    best = lists[0]
    tau = best[-1]
    z = jnp.zeros_like(tau)
    for c in best:
        z = z + jnp.exp(c - best[0])
    inv_z = 1.0 / z

    twice = lambda x: jnp.concatenate([x, x], axis=0)
    tau2, a0_2, inv_z2, b2 = twice(tau), twice(a[0]), twice(inv_z), [twice(x) for x in b]

    def first_half(i, carry):
        rows = pl.ds(pl.multiple_of(i * 2 * PEER_HEADS, 2 * PEER_HEADS), 2 * PEER_HEADS)
        x = s_ref[rows, :]
        cnt = jnp.zeros_like(x)
        for bl in b2:
            cnt = cnt + jnp.where(x + bl >= tau2, 1.0, 0.0)
        c1_ref[rows, :] = cnt.astype(c1_ref.dtype)
        e1_ref[rows, :] = (jnp.exp(x - a0_2) * inv_z2).astype(e1_ref.dtype)
        return carry

    lax.fori_loop(0, PEER_NKEYS // 2, first_half, 0)
    for h in range(PEER_HEADS):
        x = s_ref[pl.ds(2 * hk + h * PEER_NKEYS, PEER_NKEYS), :]
        rank = jnp.zeros_like(x)
        for bl in b:
            rank = rank + jnp.where(bl[h:h + 1, :] > x, 1.0, 0.0)
        rows = slice(h * PEER_NKEYS, (h + 1) * PEER_NKEYS)
        r2_ref[rows, :] = rank.astype(r2_ref.dtype)
        e2_ref[rows, :] = jnp.exp(x - b[0][h:h + 1, :]).astype(e2_ref.dtype)


def _peer_topk(s_t, tn=PEER_TOKEN_BLOCK):
    rows, n = s_t.shape
    hk = PEER_HEADS * PEER_NKEYS
    out = jax.ShapeDtypeStruct((hk, n), GATE_DTYPE)
    spec = pl.BlockSpec((hk, tn), lambda i: (0, i))
    return pl.pallas_call(
        _peer_topk_kernel,
        out_shape=(out,) * 4,
        grid=(n // tn,),
        in_specs=[pl.BlockSpec((rows, tn), lambda i: (0, i))],
        out_specs=(spec,) * 4,
        compiler_params=_cparams("parallel"),
        name="peer_topk",
    )(s_t)


def _gelu(x):
    return 0.5 * x * (1.0 + lax.erf(x * (1.0 / math.sqrt(2.0))))


PEER_QUAD = 4 * PEER_NKEYS
GATE_TILE = (64, 256)


def _peer_experts_kernel(x_ref, h_ref, c1_ref, e1_ref, r2_ref, e2_ref, u_ref, v_ref, o_ref, acc_ref, *a_refs):
    @pl.when(pl.program_id(1) == 0)
    def _():
        acc_ref[...] = jnp.zeros_like(acc_ref)

    hb = h_ref[...]
    tn = hb.shape[1]
    te, tt = GATE_TILE
    n_quads = u_ref.shape[0] // PEER_QUAD
    scores = lambda quad: _dot(u_ref[quad * PEER_QUAD:(quad + 1) * PEER_QUAD, :], hb)
    st_next = scores(0)
    for quad in range(n_quads):
        st, st_next = st_next, (scores(quad + 1) if quad + 1 < n_quads else None)
        a_ref = a_refs[quad % 2]
        for sub in range(PEER_QUAD // PEER_NKEYS):
            i1 = quad * (PEER_QUAD // PEER_NKEYS) + sub
            c1 = c1_ref[i1 * PEER_HEADS:(i1 + 1) * PEER_HEADS, :]
            e1 = e1_ref[i1 * PEER_HEADS:(i1 + 1) * PEER_HEADS, :]
            for r0 in range(0, PEER_NKEYS, te):
                for l0 in range(0, tn, tt):
                    lanes = slice(l0, l0 + tt)
                    gate = None
                    for h in range(PEER_HEADS):
                        rows = slice(h * PEER_NKEYS + r0, h * PEER_NKEYS + r0 + te)
                        sel = r2_ref[rows, lanes] < c1[h:h + 1, lanes]
                        term = jnp.where(sel, e2_ref[rows, lanes], 0.0) * e1[h:h + 1, lanes]
                        gate = term if gate is None else gate + term
                    rows = slice(sub * PEER_NKEYS + r0, sub * PEER_NKEYS + r0 + te)
                    a_ref[rows, lanes] = (_gelu(st[rows, lanes]).astype(gate.dtype) * gate).astype(a_ref.dtype)
        acc_ref[...] += _dot(v_ref[quad], a_ref[...])

    @pl.when(pl.program_id(1) == pl.num_programs(1) - 1)
    def _():
        o_ref[...] = x_ref[...] + acc_ref[...].T


def _peer_experts(x, h_t, c1, e1, r2, e2, u2, v3, tn=PEER_TOKEN_BLOCK):
    d, n = h_t.shape
    hk = PEER_HEADS * PEER_NKEYS
    ge = PEER_GROUP * PEER_NKEYS
    tok = lambda rows: pl.BlockSpec((rows, tn), lambda i, g: (0, i))
    grp = pl.BlockSpec((PEER_GROUP * PEER_HEADS, tn), lambda i, g: (g, i))
    return pl.pallas_call(
        _peer_experts_kernel,
        out_shape=jax.ShapeDtypeStruct((n, d), F32),
        grid=(n // tn, u2.shape[0] // ge),
        in_specs=[pl.BlockSpec((tn, d), lambda i, g: (i, 0)), tok(d), grp, grp, tok(hk), tok(hk),
                  pl.BlockSpec((ge, d), lambda i, g: (g, 0)),
                  pl.BlockSpec((ge // PEER_QUAD, d, PEER_QUAD), lambda i, g: (g, 0, 0))],
        out_specs=pl.BlockSpec((tn, d), lambda i, g: (i, 0)),
        scratch_shapes=[pltpu.VMEM((d, tn), F32)] + [pltpu.VMEM((PEER_QUAD, tn), MXU_DTYPE)] * 2,
        compiler_params=_cparams("parallel", "arbitrary"),
        name="peer_experts",
    )(x, h_t, c1, e1, r2, e2, u2, v3)


def _peer_prepare(wq, subkeys, u, v):
    d = wq.shape[0]
    hk = PEER_HEADS * PEER_NKEYS
    mf = _peer_fold(subkeys, wq)
    inter = jnp.transpose(mf, (0, 2, 1, 3)).reshape(2 * hk, d)
    mcat = jnp.concatenate([inter, mf[1].reshape(hk, d)], axis=0).astype(MXU_DTYPE)
    v3 = jnp.transpose(v.astype(MXU_DTYPE).reshape(-1, PEER_QUAD, d), (0, 2, 1))
    return mcat, u.astype(MXU_DTYPE), v3


def _peer_ffn(x, g, prep):
    mcat, u2, v3 = prep
    s_t, h_t = _peer_scores(x, g, mcat)
    return _peer_experts(x, h_t, *_peer_topk(s_t), u2, v3)


HGRN_CHUNK = 64
HGRN_SUB = 16
MASKED_EXPONENT = -1e30


def _cumsum_rows(x):
    rows = x.shape[0]
    row = lax.broadcasted_iota(I32, x.shape, 0)
    d = 1
    while d < rows:
        x = x + jnp.where(row >= d, pltpu.roll(x, d, 0), 0.0)
        d *= 2
    return x


def _hgrn_chunk(q, k, v, g, s_t, sub):
    c = q.shape[0]
    cum = _cumsum_rows(g)
    o = _dot_nt((q * jnp.exp(cum)).astype(MXU_DTYPE), s_t.astype(MXU_DTYPE))
    outs = []
    for blk in range(c // sub):
        r0 = blk * sub
        q_b, cum_b, k_b, v_b = q[r0:r0 + sub], cum[r0:r0 + sub], k[r0:r0 + sub], v[r0:r0 + sub]
        o_b = o[r0:r0 + sub]
        if blk > 0:
            base = cum[r0 - 1:r0]
            qs = q_b * jnp.exp(cum_b - base)
            ks = k[0:r0] * jnp.exp(base - cum[0:r0])
            att = _dot_nt(qs.astype(MXU_DTYPE), ks.astype(MXU_DTYPE))
            o_b = o_b + _dot(att.astype(MXU_DTYPE), v[0:r0].astype(MXU_DTYPE))
        row = lax.broadcasted_iota(I32, (sub, q.shape[1]), 0)
        for s in range(sub):
            dec = jnp.exp(jnp.where(row >= s, cum_b - cum_b[s:s + 1], MASKED_EXPONENT))
            att = jnp.sum(q_b * k_b[s:s + 1] * dec, axis=1, keepdims=True)
            o_b = o_b + att * v_b[s:s + 1]
        outs.append(o_b)
    o = outs[0] if len(outs) == 1 else jnp.concatenate(outs, axis=0)
    last = cum[c - 1:c]
    kd = k * jnp.exp(last - cum)
    upd = lax.dot_general(v.astype(MXU_DTYPE), kd.astype(MXU_DTYPE), (((0,), (0,)), ((), ())),
                          preferred_element_type=F32)
    return o, s_t * jnp.exp(last) + upd


def _hgrn_kernel(q_ref, f_ref, i_ref, g_ref, lb_ref, gain_ref, s0_ref, o_ref, s_ref, *, chunk, sub):
    @pl.when(pl.program_id(2) == 0)
    def _():
        s_ref[...] = s0_ref[...]

    lb = lb_ref[...]
    s_t = s_ref[...]
    for c0 in range(0, q_ref.shape[0], chunk):
        rows = slice(c0, c0 + chunk)
        f = lb + (1.0 - lb) * jax.nn.sigmoid(f_ref[rows, :])
        o, s_t = _hgrn_chunk(jax.nn.silu(q_ref[rows, :]), 1.0 - f, i_ref[rows, :], jnp.log(f), s_t, sub)
        o_ref[rows, :] = _rms(o, gain_ref[...]) * jax.nn.silu(g_ref[rows, :])
    s_ref[...] = s_t


def _hgrn(proj, row_start, batch, t, heads, lb, gain, s0_t):
    chunk = math.gcd(t, HGRN_CHUNK)
    sub = min(HGRN_SUB, chunk)
    tc = min(t, 4 * chunk)
    nt = t // tc
    rb0 = row_start // tc
    dk = LANES

    def slab(k):
        return pl.BlockSpec((tc, dk), lambda b, h, i, k=k: (rb0 + b * nt + i, k * heads + h))

    vec = pl.BlockSpec((1, dk), lambda b, h, i: (0, h))
    st = pl.BlockSpec((None, None, dk, dk), lambda b, h, i: (b, h, 0, 0))
    return pl.pallas_call(
        functools.partial(_hgrn_kernel, chunk=chunk, sub=sub),
        out_shape=(jax.ShapeDtypeStruct((batch * t, heads * dk), F32),
                   jax.ShapeDtypeStruct((batch, heads, dk, dk), F32)),
        grid=(batch, heads, nt),
        in_specs=[slab(0), slab(1), slab(2), slab(3), vec, vec, st],
        out_specs=(pl.BlockSpec((tc, dk), lambda b, h, i: (b * nt + i, h)), st),
        compiler_params=_cparams("parallel", "parallel", "arbitrary"),
        name="hgrn2",
    )(proj, proj, proj, proj, lb.reshape(1, -1), gain.reshape(1, -1), s0_t)


DSA_TOPK_MAX = 256
DSA_QUERY_BLOCK = 128
INT32_MIN = -2 ** 31


def _count(m):
    return jnp.sum(jnp.where(m, 1.0, 0.0), axis=1, keepdims=True)


def _float_key(x):
    u = lax.bitcast_convert_type(x, I32)
    return u ^ ((u >> 31) & I32(0x7FFFFFFF))


KEY_NEG_INF = INT32_MIN + 0x007FFFFF
INT32_MAX = 2 ** 31 - 1


def _topk_thresholds(count, k, idx_bits, rows):
    kf = float(k)
    v = jnp.where(count(lambda key, idx: key >= 0) >= kf, I32(0), I32(INT32_MIN))

    def value_bit(it, v):
        t = v | (I32(1) << (I32(30) - it))
        return jnp.where(count(lambda key, idx: key >= t) >= kf, t, v)

    v = lax.fori_loop(0, 31, value_bit, v)
    at_least = count(lambda key, idx: key >= v)

    def break_ties():
        need = kf - count(lambda key, idx: key > v)

        def index_bit(it, j):
            t = j | (I32(1) << (I32(idx_bits - 1) - it))
            return jnp.where(count(lambda key, idx: (key == v) & (idx < t)) < need, t, j)

        return lax.fori_loop(0, idx_bits, index_bit, jnp.zeros((rows, 1), I32))

    j = lax.cond(jnp.max(at_least) > kf, break_ties, lambda: jnp.full((rows, 1), INT32_MAX, I32))
    return v, j


def _selected(key, idx, v, j):
    return ((key > v) | ((key == v) & (idx <= j))) & (key > KEY_NEG_INF)


def _topk_mask(scores, k, idx_bits):
    key = _float_key(scores)
    idx = lax.broadcasted_iota(I32, scores.shape, 1)
    v, j = _topk_thresholds(lambda pred: _count(pred(key, idx)), k, idx_bits, scores.shape[0])
    return _selected(key, idx, v, j)


def _index_scores(qi, w, ki, heads, keys_transposed=False):
    d = qi.shape[1]
    dots = (_dot(qi, ki) if keys_transposed else _dot_nt(qi, ki)) * (d ** -0.5)
    terms = jnp.maximum(dots, 0.0) * (w * (heads ** -0.5))
    r = qi.shape[0] // heads
    acc = terms[0:r]
    for h in range(1, heads):
        acc = acc + terms[h * r:(h + 1) * r]
    return acc


def _masked_softmax_pv(s, mask, v):
    s = jnp.where(mask, s, NEG_INF)
    p = jnp.exp(s - jnp.max(s, axis=1, keepdims=True))
    return _dot(p.astype(MXU_DTYPE), v) / jnp.sum(p, axis=1, keepdims=True)


PAGES_PER_STEP = 32
ONLINE_SOFTMAX_FLOOR = -1e30
KEY_BLOCK = 512


def _dsa_prompt_kernel(qi_ref, w_ref, ki_ref, q_ref, k_ref, v_ref, o_ref, *, topk, hi, kb_size):
    heads, tq, dh = q_ref.shape
    kv_heads = k_ref.shape[0]
    t = ki_ref.shape[0]
    q0 = pl.program_id(1) * tq
    n_kb = (q0 + tq - 1) // kb_size + 1

    def attend(s_keys):
        qpos = q0 + lax.broadcasted_iota(I32, (tq, s_keys), 0)
        kpos = lax.broadcasted_iota(I32, (tq, s_keys), 1)
        scores = _index_scores(qi_ref[...], w_ref[...], ki_ref[0:s_keys, :], hi)
        mask = _topk_mask(jnp.where(kpos <= qpos, scores, NEG_INF), topk, max(1, (s_keys - 1).bit_length()))
        for h in range(heads):
            g = h // (heads // kv_heads)
            s = _dot_nt(q_ref[h], k_ref[g, 0:s_keys, :]) * (dh ** -0.5)
            o_ref[h] = _masked_softmax_pv(s, mask, v_ref[g, 0:s_keys, :])

    for n in range(1, t // kb_size + 1):
        pl.when(n_kb == n)(functools.partial(attend, n * kb_size))


def _dsa_prompt(qi2, wcol, ki, q4, k4, v4, hi, tq=128):
    b, t, d = ki.shape
    _, h, _, dh = q4.shape
    hkv = k4.shape[1]
    topk = min(DSA_TOPK_MAX, t // 4)
    kb = math.gcd(t, KEY_BLOCK)
    return pl.pallas_call(
        functools.partial(_dsa_prompt_kernel, topk=topk, hi=hi, kb_size=kb),
        out_shape=jax.ShapeDtypeStruct((b, h, t, dh), F32),
        grid=(b, t // tq),
        in_specs=[pl.BlockSpec((None, None, hi * tq, d), lambda n, i: (n, i, 0, 0)),
                  pl.BlockSpec((None, None, hi * tq, 1), lambda n, i: (n, i, 0, 0)),
                  pl.BlockSpec((None, t, d), lambda n, i: (n, 0, 0)),
                  pl.BlockSpec((None, h, tq, dh), lambda n, i: (n, 0, i, 0)),
                  pl.BlockSpec((None, hkv, t, dh), lambda n, i: (n, 0, 0, 0)),
                  pl.BlockSpec((None, hkv, t, dh), lambda n, i: (n, 0, 0, 0))],
        out_specs=pl.BlockSpec((None, h, tq, dh), lambda n, i: (n, 0, i, 0)),
        compiler_params=_cparams("parallel", "arbitrary"),
        name="dsa_prompt",
    )(qi2, wcol, ki, q4, k4, v4)


def _page_specs(block, layer, pages):
    zeros = (0,) * (len(block) - 2)
    return [pl.BlockSpec(block, lambda b, c, pt, r=r: (layer, pt[b, c * pages + r]) + zeros) for r in range(pages)]


DSA_PICK_BATCH = 8


def _dsa_scores_kernel(pt_ref, qi_ref, w_ref, kn_ref, *rest, pages, t_new, hi):
    page_refs, (sc_ref, new_ref) = rest[:pages], rest[pages:]
    kc = jnp.concatenate([r[...] for r in page_refs], axis=1).astype(MXU_DTYPE)
    sc_ref[...] = _index_scores(qi_ref[...], w_ref[...], kc, hi, keys_transposed=True)

    @pl.when(pl.program_id(1) == pl.num_programs(1) - 1)
    def _():
        new = _index_scores(qi_ref[...], w_ref[...], kn_ref[...], hi)
        qpos = lax.broadcasted_iota(I32, new.shape, 0)
        kpos = lax.broadcasted_iota(I32, new.shape, 1)
        new_ref[...] = jnp.where((kpos <= qpos) & (kpos < t_new), new, NEG_INF)


def _dsa_pick_kernel(sc_ref, new_ref, mask_ref, *, topk):
    nbat, n_chunks, rows, ch = sc_ref.shape
    scores = jnp.concatenate([jnp.concatenate([sc_ref[bb, i] for i in range(n_chunks)] + [new_ref[bb]], axis=1)
                              for bb in range(nbat)], axis=0)
    mask = jnp.where(_topk_mask(scores, topk, scores.shape[1].bit_length()), 1.0, 0.0)
    pad = jnp.zeros((rows, ch - new_ref.shape[2]), F32)
    for bb in range(nbat):
        for i in range(n_chunks):
            mask_ref[bb, i] = mask[bb * rows:(bb + 1) * rows, i * ch:(i + 1) * ch]
        mask_ref[bb, n_chunks] = jnp.concatenate([mask[bb * rows:(bb + 1) * rows, n_chunks * ch:], pad], axis=1)


def _dsa_select(page_table, qi2, wcol, ki_new_pad, pool_idx, layer, topk, t_new, hi):
    b, n_pages = page_table.shape
    pages = math.gcd(n_pages, PAGES_PER_STEP)
    nc = n_pages // pages
    ch = pages * PAGE_SIZE
    _, rows, d = qi2.shape
    t = rows // hi
    grid_spec = pltpu.PrefetchScalarGridSpec(
        num_scalar_prefetch=1,
        grid=(b, nc),
        in_specs=[pl.BlockSpec((None, rows, d), lambda n, c, pt: (n, 0, 0)),
                  pl.BlockSpec((None, rows, 1), lambda n, c, pt: (n, 0, 0)),
                  pl.BlockSpec((None, LANES, d), lambda n, c, pt: (n, 0, 0))]
        + _page_specs((None, None, d, PAGE_SIZE), layer, pages),
        out_specs=(pl.BlockSpec((None, None, t, ch), lambda n, c, pt: (n, c, 0, 0)),
                   pl.BlockSpec((None, t, LANES), lambda n, c, pt: (n, 0, 0))),
    )
    scores, new = pl.pallas_call(
        functools.partial(_dsa_scores_kernel, pages=pages, t_new=t_new, hi=hi),
        out_shape=(jax.ShapeDtypeStruct((b, nc, t, ch), F32), jax.ShapeDtypeStruct((b, t, LANES), F32)),
        grid_spec=grid_spec,
        compiler_params=_cparams("parallel", "arbitrary"),
        name="dsa_sample_scores",
    )(page_table, qi2, wcol, ki_new_pad, *([pool_idx] * pages))
    nbat = math.gcd(b, DSA_PICK_BATCH)
    return pl.pallas_call(
        functools.partial(_dsa_pick_kernel, topk=topk),
        out_shape=jax.ShapeDtypeStruct((b, nc + 1, t, ch), F32),
        grid=(b // nbat,),
        in_specs=[pl.BlockSpec((nbat, nc, t, ch), lambda n: (n, 0, 0, 0)),
                  pl.BlockSpec((nbat, t, LANES), lambda n: (n, 0, 0))],
        out_specs=pl.BlockSpec((nbat, nc + 1, t, ch), lambda n: (n, 0, 0, 0)),
        compiler_params=_cparams("parallel"),
        name="dsa_sample_select",
    )(scores, new)


def _online_softmax_step(m_ref, l_ref, acc_ref, j, s, mask, v, values_transposed=False):
    m_old = m_ref[j]
    m_new = jnp.maximum(m_old, jnp.max(jnp.where(mask, s, ONLINE_SOFTMAX_FLOOR), axis=1, keepdims=True))
    alpha = jnp.exp(m_old - m_new)
    p = jnp.where(mask, jnp.exp(s - m_new), 0.0)
    l_ref[j] = alpha * l_ref[j] + jnp.sum(p, axis=1, keepdims=True)
    pv = _dot_nt(p.astype(MXU_DTYPE), v) if values_transposed else _dot(p.astype(MXU_DTYPE), v)
    acc_ref[j] = alpha * acc_ref[j] + pv
    m_ref[j] = m_new


def _online_softmax_heads(m_ref, l_ref, acc_ref, scores, mask, pv_fn):
    rows = scores[0].shape[0]
    s = jnp.concatenate(scores, axis=0)
    mask = jnp.concatenate([mask] * (s.shape[0] // mask.shape[0]), axis=0)
    m_old = m_ref[...]
    m_new = jnp.maximum(m_old, jnp.max(jnp.where(mask, s, ONLINE_SOFTMAX_FLOOR), axis=1, keepdims=True))
    alpha = jnp.exp(m_old - m_new)
    p = jnp.where(mask, jnp.exp(s - m_new), 0.0)
    l_ref[...] = alpha * l_ref[...] + jnp.sum(p, axis=1, keepdims=True)
    pb = p.astype(MXU_DTYPE)
    pv = jnp.concatenate([pv_fn(g, pb[g * rows:(g + 1) * rows]) for g in range(len(scores))], axis=0)
    acc_ref[...] = alpha * acc_ref[...] + pv
    m_ref[...] = m_new


def _dsa_attend_kernel(pt_ref, q_ref, mc_ref, mn_ref, kn_ref, vn_ref, *rest, pages):
    k_pages, v_pages = rest[:pages], rest[pages:2 * pages]
    o_ref, m_ref, l_ref, acc_ref = rest[2 * pages:]
    c = pl.program_id(1)
    kv_heads, rows, dh = q_ref.shape
    scale = dh ** -0.5

    @pl.when(c == 0)
    def _():
        m_ref[...] = jnp.full_like(m_ref, ONLINE_SOFTMAX_FLOOR)
        l_ref[...] = jnp.zeros_like(l_ref)
        acc_ref[...] = jnp.zeros_like(acc_ref)

    def keys_t(refs, g):
        return jnp.concatenate([r[g] for r in refs], axis=1).astype(MXU_DTYPE)

    scores = [_dot(q_ref[g], keys_t(k_pages, g)) * scale for g in range(kv_heads)]
    _online_softmax_heads(m_ref, l_ref, acc_ref, scores, mc_ref[...] > 0.0,
                          lambda g, p: _dot_nt(p, keys_t(v_pages, g)))

    @pl.when(c == pl.num_programs(1) - 1)
    def _():
        new = [_dot_nt(q_ref[g], kn_ref[g]) * scale for g in range(kv_heads)]
        _online_softmax_heads(m_ref, l_ref, acc_ref, new, mn_ref[:, 0:LANES] > 0.0, lambda g, p: _dot(p, vn_ref[g]))
        o_ref[...] = (acc_ref[...] / l_ref[...]).reshape(o_ref.shape)


def _dsa_attend(page_table, q4, mask, k_new_pad, v_new_pad, pool_k, pool_v, layer):
    b, n_pages = page_table.shape
    pages = math.gcd(n_pages, PAGES_PER_STEP)
    nc = n_pages // pages
    _, hkv, rows, dh = q4.shape
    _, _, t, ch = mask.shape
    grid_spec = pltpu.PrefetchScalarGridSpec(
        num_scalar_prefetch=1,
        grid=(b, nc),
        in_specs=[pl.BlockSpec((None, hkv, rows, dh), lambda n, c, pt: (n, 0, 0, 0)),
                  pl.BlockSpec((None, None, t, ch), lambda n, c, pt: (n, c, 0, 0)),
                  pl.BlockSpec((None, None, t, ch), lambda n, c, pt: (n, nc, 0, 0)),
                  pl.BlockSpec((None, hkv, LANES, dh), lambda n, c, pt: (n, 0, 0, 0)),
                  pl.BlockSpec((None, hkv, LANES, dh), lambda n, c, pt: (n, 0, 0, 0))]
        + _page_specs((None, None, hkv, dh, PAGE_SIZE), layer, pages)
        + _page_specs((None, None, hkv, dh, PAGE_SIZE), layer, pages),
        out_specs=pl.BlockSpec((None, hkv, rows, dh), lambda n, c, pt: (n, 0, 0, 0)),
        scratch_shapes=[pltpu.VMEM((hkv * rows, 1), F32), pltpu.VMEM((hkv * rows, 1), F32),
                        pltpu.VMEM((hkv * rows, dh), F32)],
    )
    return pl.pallas_call(
        functools.partial(_dsa_attend_kernel, pages=pages),
        out_shape=jax.ShapeDtypeStruct((b, hkv, rows, dh), F32),
        grid_spec=grid_spec,
        compiler_params=_cparams("parallel", "arbitrary"),
        name="dsa_sample_attend",
    )(page_table, q4, mask, mask, k_new_pad, v_new_pad, *([pool_k] * pages), *([pool_v] * pages))


GN_EPS = 64e-5
RWKV_HEAD = 64
RWKV_BATCH_BLOCK = 4


def _dot_f32(a, b):
    return jnp.dot(a, b, preferred_element_type=F32, precision=lax.Precision.HIGHEST)


def _rwkv_prep_kernel(pc_ref, prev_ref, mu_ref, vec_ref, lora_ref, gup_ref, seg_ref,
                      r_ref, w_ref, k_ref, v_ref, kk_ref, b_ref, g_ref, bonus_ref):
    cw = r_ref.shape[1]
    pc = pc_ref[...]
    xm = pc + (prev_ref[...] - pc) * mu_ref[...]
    r, kc, vc = xm[:, 0:cw], xm[:, cw:2 * cw], xm[:, 2 * cw:3 * cw]
    wa = xm[:, 3 * cw:3 * cw + LANES]
    gd = xm[:, 3 * cw + LANES:]
    lane = lax.broadcasted_iota(I32, wa.shape, 1)
    wa = jnp.where(lane < LANES // 2, jnp.tanh(wa), wa)
    lo = _dot(wa.astype(MXU_DTYPE), lora_ref[...])
    w0, a0, k_k, k_a, r_k = (vec_ref[i:i + 1, :] for i in range(5))
    w_log = -jax.nn.softplus(-(w0 + lo[:, 0:cw])) - 0.5
    a = jax.nn.sigmoid(a0 + lo[:, cw:2 * cw])
    kk = kc * k_k
    norm = jnp.sqrt(_dot_f32(kk * kk, seg_ref[...]))
    kk = kk / jnp.maximum(norm, 1e-12)
    kc = kc * (1.0 + (a - 1.0) * k_a)
    r_ref[...] = r
    w_ref[...] = jnp.exp(-jnp.exp(w_log))
    k_ref[...] = kc
    v_ref[...] = vc
    kk_ref[...] = kk
    b_ref[...] = -(kk * a)
    g_ref[...] = _dot(jax.nn.sigmoid(gd).astype(MXU_DTYPE), gup_ref[...])
    bonus_ref[...] = _dot_f32(r * kc * r_k, seg_ref[...]) * vc


def _rwkv_prep(pc, prev, mu, vecs, lora, g_up, seg, tm):
    n, width = prev.shape
    cw = vecs.shape[1]
    row = lambda w: pl.BlockSpec((tm, w), lambda i: (i, 0))
    full = lambda a: pl.BlockSpec(a.shape, lambda i: (0, 0))
    out = jax.ShapeDtypeStruct((n, cw), F32)
    return pl.pallas_call(
        _rwkv_prep_kernel,
        out_shape=(out,) * 8,
        grid=(n // tm,),
        in_specs=[row(width), row(width), full(mu), full(vecs), full(lora), full(g_up), full(seg)],
        out_specs=(row(cw),) * 8,
        compiler_params=_cparams("parallel"),
        name="rwkv_prep",
    )(pc, prev, mu, vecs, lora, g_up, seg)


def _segment_sum(x, seg):
    hi = x.astype(MXU_DTYPE)
    lo = (x - hi.astype(F32)).astype(MXU_DTYPE)
    return _dot(hi, seg) + _dot(lo, seg)


def _rwkv_scan_kernel(*refs, nb):
    ins, (seg_ref, s0_ref, y_ref, s_ref) = refs[:6 * nb], refs[6 * nb:]

    @pl.when(pl.program_id(1) == 0)
    def _():
        s_ref[...] = s0_ref[...]

    _, rows, width = s_ref.shape
    pairs = rows // RWKV_HEAD
    lane = lax.broadcasted_iota(I32, (nb * rows, width), 1)
    sub = lax.broadcasted_iota(I32, (nb * rows, width), 0)
    diag = (sub & (RWKV_HEAD - 1)) == (lane & (RWKV_HEAD - 1))
    own = (lax.broadcasted_iota(I32, (2, width), 0) == 0) == (lax.broadcasted_iota(I32, (2, width), 1) < RWKV_HEAD)
    seg = seg_ref[...]

    def group(g, carry):
        t0 = pl.multiple_of(g * SUBLANES, SUBLANES)
        r8, w8, k8, v8, kk8, nb8 = ([ins[6 * bb + q][pl.ds(t0, SUBLANES), :] for bb in range(nb)] for q in range(6))
        s = s_ref[...].reshape(nb * rows, width)
        for i in range(SUBLANES):
            def per_row(x8):
                return jnp.concatenate([jnp.broadcast_to(x[i:i + 1, p * width:(p + 1) * width], (RWKV_HEAD, width))
                                        for x in x8 for p in range(pairs)], axis=0)
            sa = _segment_sum(s * per_row(kk8), seg)
            vcol = _segment_sum(jnp.where(diag, per_row(v8), 0.0), seg)
            s = s * per_row(w8) + sa * per_row(nb8) + vcol * per_row(k8)
            sb = s.astype(MXU_DTYPE)
            for bb in range(nb):
                for p in range(pairs):
                    r2 = jnp.where(own, r8[bb][i:i + 1, p * width:(p + 1) * width], 0.0)
                    r0 = (bb * pairs + p) * RWKV_HEAD
                    y = _dot_nt(r2.astype(MXU_DTYPE), sb[r0:r0 + RWKV_HEAD])
                    y_ref[bb, pl.ds(t0 + i, 1), 2 * p:2 * p + 2, :] = y[None]
        s_ref[...] = s.reshape(nb, rows, width)
        return carry

    lax.fori_loop(0, ins[0].shape[0] // SUBLANES, group, 0)


def _rwkv_scan(ins, row_start, batch, t, s0_packed, seg2, nb):
    cw = ins[0].shape[1]
    heads = cw // RWKV_HEAD
    tb = min(t, 64)
    nt = t // tb
    rb0 = row_start // tb
    rows = [pl.BlockSpec((tb, cw), lambda b, i, bb=bb: (rb0 + (b * nb + bb) * nt + i, 0)) for bb in range(nb)]
    st = pl.BlockSpec((nb,) + s0_packed.shape[1:], lambda b, i: (b, 0, 0))
    return pl.pallas_call(
        functools.partial(_rwkv_scan_kernel, nb=nb),
        out_shape=(jax.ShapeDtypeStruct((batch, t, heads, RWKV_HEAD), F32),
                   jax.ShapeDtypeStruct(s0_packed.shape, F32)),
        grid=(batch // nb, nt),
        in_specs=[spec for spec in rows for _ in range(6)] + [pl.BlockSpec(seg2.shape, lambda b, i: (0, 0)), st],
        out_specs=(pl.BlockSpec((nb, tb, heads, RWKV_HEAD), lambda b, i: (b, i, 0, 0)), st),
        compiler_params=_cparams("parallel", "arbitrary"),
        name="rwkv_scan",
    )(*(list(ins) * nb), seg2, s0_packed)


def _rwkv_post_kernel(y_ref, bonus_ref, g_ref, ln_ref, seg_ref, o_ref):
    y = y_ref[...]
    avg = seg_ref[...] * (1.0 / RWKV_HEAD)
    d = y - _dot_f32(y, avg)
    var = _dot_f32(d * d, avg)
    yn = d * lax.rsqrt(var + GN_EPS) * ln_ref[0:1, :] + ln_ref[1:2, :]
    o_ref[...] = (yn + bonus_ref[...]) * g_ref[...]


def _rwkv_post(y, bonus, g, ln, seg, row_start, tm):
    n, cw = y.shape
    rb0 = row_start // tm
    row = pl.BlockSpec((tm, cw), lambda i: (i, 0))
    off = pl.BlockSpec((tm, cw), lambda i: (rb0 + i, 0))
    full = lambda a: pl.BlockSpec(a.shape, lambda i: (0, 0))
    return pl.pallas_call(
        _rwkv_post_kernel,
        out_shape=jax.ShapeDtypeStruct((n, cw), F32),
        grid=(n // tm,),
        in_specs=[row, off, off, full(ln), full(seg)],
        out_specs=row,
        compiler_params=_cparams("parallel"),
        name="rwkv_post",
    )(y, bonus, g, ln, seg)


MLA_HEADS = 8
MLA_NOPE = 64
MLA_ROPE = 32
MLA_SCALE = (MLA_NOPE + MLA_ROPE) ** -0.5
MLA_PAGES_PER_STEP = 64


def _rope_tile(x, cos, sin):
    lane = lax.broadcasted_iota(I32, x.shape, 1)
    half = MLA_ROPE // 2
    rot = jnp.where(lane < half, pltpu.roll(x, LANES - half, 1), pltpu.roll(x, half, 1))
    return x * cos + rot * sin


def _mla_prep_kernel(qd_ref, ckv_ref, kr_ref, cos_ref, sin_ref, qn_ref, kvn_ref, wuq_ref, wuk_ref,
                     ql_ref, qr_ref, c_ref, krn_ref):
    cq = _rms(qd_ref[...], qn_ref[...])
    qh = _dot(cq.astype(MXU_DTYPE), wuq_ref[...])
    nope = MLA_HEADS * MLA_NOPE
    ql_ref[...] = _dot(qh[:, 0:nope].astype(MXU_DTYPE), wuk_ref[...]).astype(ql_ref.dtype)
    cos, sin = cos_ref[...], sin_ref[...]
    for h in range(MLA_HEADS):
        lanes = slice(nope + h * LANES, nope + (h + 1) * LANES)
        qr_ref[:, h * LANES:(h + 1) * LANES] = _rope_tile(qh[:, lanes], cos, sin).astype(qr_ref.dtype)
    c_ref[...] = _rms(ckv_ref[...], kvn_ref[...])
    krn_ref[...] = _rope_tile(kr_ref[...], cos, sin)


def _mla_prep(proj, col_blocks, row_start, nrows, cos, sin, q_norm, kv_norm, wuq, wuk, tm):
    qd0, ckv0, kr0 = col_blocks
    d_q, d_kv = q_norm.shape[1], kv_norm.shape[1]
    rb0 = row_start // tm
    nper = cos.shape[0] // tm
    full = lambda a: pl.BlockSpec(a.shape, lambda i: (0, 0))
    rows = lambda w: pl.BlockSpec((tm, w), lambda i: (i, 0))
    tab = pl.BlockSpec((tm, LANES), lambda i: (i % nper, 0))
    return pl.pallas_call(
        _mla_prep_kernel,
        out_shape=(jax.ShapeDtypeStruct((nrows, MLA_HEADS * d_kv), MXU_DTYPE),
                   jax.ShapeDtypeStruct((nrows, MLA_HEADS * LANES), MXU_DTYPE),
                   jax.ShapeDtypeStruct((nrows, d_kv), F32),
                   jax.ShapeDtypeStruct((nrows, LANES), F32)),
        grid=(nrows // tm,),
        in_specs=[pl.BlockSpec((tm, d_q), lambda i: (rb0 + i, qd0 * LANES // d_q)),
                  pl.BlockSpec((tm, d_kv), lambda i: (rb0 + i, ckv0 * LANES // d_kv)),
                  pl.BlockSpec((tm, LANES), lambda i: (rb0 + i, kr0)),
                  tab, tab, full(q_norm), full(kv_norm), full(wuq), full(wuk)],
        out_specs=(rows(MLA_HEADS * d_kv), rows(MLA_HEADS * LANES), rows(d_kv), rows(LANES)),
        compiler_params=_cparams("parallel"),
        name="mla_prep",
    )(proj, proj, proj, cos, sin, q_norm, kv_norm, wuq, wuk)


def _mla_prompt_kernel(ql_ref, qr_ref, c_ref, kr_ref, wuv_ref, o_ref, m_ref, l_ref, acc_ref, *, kb_size):
    tq = ql_ref.shape[0]
    d_kv = c_ref.shape[1]
    q0 = pl.program_id(1) * tq
    n_kb = (q0 + tq - 1) // kb_size + 1
    qpos = q0 + lax.broadcasted_iota(I32, (tq, kb_size), 0)
    lane = lax.broadcasted_iota(I32, (tq, kb_size), 1)
    m_ref[...] = jnp.full_like(m_ref, ONLINE_SOFTMAX_FLOOR)
    l_ref[...] = jnp.zeros_like(l_ref)
    acc_ref[...] = jnp.zeros_like(acc_ref)

    def block(kb, carry):
        k0 = pl.multiple_of(kb * kb_size, kb_size)
        c, kr = c_ref[pl.ds(k0, kb_size), :], kr_ref[pl.ds(k0, kb_size), :]
        scores = [(_dot_nt(ql_ref[:, h * d_kv:(h + 1) * d_kv], c)
                   + _dot_nt(qr_ref[:, h * LANES:(h + 1) * LANES], kr)) * MLA_SCALE for h in range(MLA_HEADS)]
        _online_softmax_heads(m_ref, l_ref, acc_ref, scores, k0 + lane <= qpos, lambda g, p: _dot(p, c))
        return carry

    lax.fori_loop(0, n_kb, block, 0)
    o_lat = (acc_ref[...] / l_ref[...]).astype(MXU_DTYPE)
    o_lat = jnp.concatenate([o_lat[h * tq:(h + 1) * tq] for h in range(MLA_HEADS)], axis=1)
    o_ref[...] = _dot(o_lat, wuv_ref[...])


def _mla_prompt(q_lat, q_rope, c, kr, wuv, batch, t, tq=128):
    d_kv = c.shape[1]
    nq = t // tq
    rows = MLA_HEADS * tq
    return pl.pallas_call(
        functools.partial(_mla_prompt_kernel, kb_size=math.gcd(t, KEY_BLOCK)),
        scratch_shapes=[pltpu.VMEM((rows, 1), F32), pltpu.VMEM((rows, 1), F32), pltpu.VMEM((rows, d_kv), F32)],
        out_shape=jax.ShapeDtypeStruct((batch * t, wuv.shape[1]), F32),
        grid=(batch, nq),
        in_specs=[pl.BlockSpec((tq, MLA_HEADS * d_kv), lambda b, i: (b * nq + i, 0)),
                  pl.BlockSpec((tq, MLA_HEADS * LANES), lambda b, i: (b * nq + i, 0)),
                  pl.BlockSpec((t, d_kv), lambda b, i: (b, 0)),
                  pl.BlockSpec((t, LANES), lambda b, i: (b, 0)),
                  pl.BlockSpec(wuv.shape, lambda b, i: (0, 0))],
        out_specs=pl.BlockSpec((tq, wuv.shape[1]), lambda b, i: (b * nq + i, 0)),
        compiler_params=_cparams("parallel", "arbitrary"),
        name="mla_prompt",
    )(q_lat, q_rope, c, kr, wuv)


def _mla_sample_kernel(pt_ref, ql_ref, qr_ref, cn_ref, krn_ref, *rest, pages, t_new):
    c_pages, kr_pages = rest[:pages], rest[pages:2 * pages]
    o_ref, m_ref, l_ref, acc_ref = rest[2 * pages:]
    step = pl.program_id(1)

    @pl.when(step == 0)
    def _():
        m_ref[...] = jnp.full_like(m_ref, ONLINE_SOFTMAX_FLOOR)
        l_ref[...] = jnp.zeros_like(l_ref)
        acc_ref[...] = jnp.zeros_like(acc_ref)

    ql, qr = ql_ref[...], qr_ref[...]
    cc = jnp.concatenate([r[...] for r in c_pages], axis=0).astype(MXU_DTYPE)
    kc = jnp.concatenate([r[...] for r in kr_pages], axis=1).astype(MXU_DTYPE)
    s = (_dot_nt(ql, cc) + _dot(qr[:, 0:MLA_ROPE], kc)) * MLA_SCALE
    _online_softmax_step(m_ref, l_ref, acc_ref, 0, s, jnp.full(s.shape, True), cc)

    @pl.when(step == pl.num_programs(1) - 1)
    def _():
        cn = cn_ref[...]
        s_new = (_dot_nt(ql, cn) + _dot_nt(qr, krn_ref[...])) * MLA_SCALE
        qpos = lax.broadcasted_iota(I32, s_new.shape, 0) % t_new
        kpos = lax.broadcasted_iota(I32, s_new.shape, 1)
        _online_softmax_step(m_ref, l_ref, acc_ref, 0, s_new, (kpos <= qpos) & (kpos < t_new), cn)
        o_ref[...] = acc_ref[0] / l_ref[0]


def _mla_sample(page_table, q_lat, q_rope, c_new_pad, kr_new_pad, pool_c, pool_kr, layer, t_new):
    b, n_pages = page_table.shape
    pages = math.gcd(n_pages, MLA_PAGES_PER_STEP)
    _, rows, d_kv = q_lat.shape
    one = lambda a: pl.BlockSpec((None,) + a.shape[1:], lambda n, c, pt: (n, 0, 0))
    grid_spec = pltpu.PrefetchScalarGridSpec(
        num_scalar_prefetch=1,
        grid=(b, n_pages // pages),
        in_specs=[one(q_lat), one(q_rope), one(c_new_pad), one(kr_new_pad)]
        + _page_specs((None, None, PAGE_SIZE, d_kv), layer, pages)
        + _page_specs((None, None, MLA_ROPE, PAGE_SIZE), layer, pages),
        out_specs=pl.BlockSpec((None, rows, d_kv), lambda n, c, pt: (n, 0, 0)),
        scratch_shapes=[pltpu.VMEM((1, rows, 1), F32), pltpu.VMEM((1, rows, 1), F32),
                        pltpu.VMEM((1, rows, d_kv), F32)],
    )
    return pl.pallas_call(
        functools.partial(_mla_sample_kernel, pages=pages, t_new=t_new),
        out_shape=jax.ShapeDtypeStruct((b, rows, d_kv), F32),
        grid_spec=grid_spec,
        compiler_params=_cparams("parallel", "arbitrary"),
        name="mla_sample",
    )(page_table, q_lat, q_rope, c_new_pad, kr_new_pad, *([pool_c] * pages), *([pool_kr] * pages))


ROW_BLOCK = 256
A_HEADS = 4
B_HEADS, B_KV_HEADS, B_DH = 8, 4, 64
IDX_HEADS, IDX_DIM = 8, 64
C_WIDTH = 512
RWKV_IN = 3 * C_WIDTH + 64 + 64 + 128
D_Q_RANK, D_KV_RANK = 384, 256
ODD_COLS = RWKV_IN + D_KV_RANK + 2 * LANES + D_Q_RANK
ODD_SLABS = ((RWKV_IN + D_KV_RANK + 2 * LANES) // LANES, RWKV_IN // LANES, (RWKV_IN + D_KV_RANK) // LANES)


def _pad_rows(a, rows=LANES):
    return jnp.pad(a, ((0, 0),) * (a.ndim - 2) + ((0, rows - a.shape[-2]), (0, 0)))


def _head_major(a, b, t, h):
    return jnp.transpose(a.reshape(b, t, h, -1), (0, 2, 1, 3))


def _token_major(a):
    b, h, t, w = a.shape
    return jnp.transpose(a, (0, 2, 1, 3)).reshape(b * t, h * w)


def _rope_tables(pos):
    half = MLA_ROPE // 2
    inv = ROPE_THETA ** (-jnp.arange(0, MLA_ROPE, 2, dtype=F32) / MLA_ROPE)
    ang = pos.astype(F32)[:, None] * inv[None, :]
    zeros = jnp.zeros((pos.shape[0], LANES - 2 * half), F32)
    cos, sin = jnp.cos(ang), jnp.sin(ang)
    return jnp.concatenate([cos, cos, zeros], axis=1), jnp.concatenate([-sin, sin, zeros], axis=1)


def kernel(x_prompt, x_sample, cache_dsa_k, cache_dsa_v, cache_dsa_idx, cache_mla_ckv, cache_mla_krope, state_hgrn, state_rwkv, state_shift, page_table, norm_mix, norm_ffn, norm_final, w_in_even, w_out_even, hgrn_lb, hgrn_norm, w_in_odd, w_out_odd, rwkv_mu, rwkv_w0, rwkv_w_up, rwkv_a0, rwkv_a_up, rwkv_g_up, rwkv_k_k, rwkv_k_a, rwkv_r_k, rwkv_ln_w, rwkv_ln_b, mla_q_norm, mla_w_uq, mla_kv_norm, mla_w_uk, mla_w_uv, peer_wq, peer_subkeys, peer_u, peer_v):
    bp, tp, d_model = x_prompt.shape
    bs, ts, _ = x_sample.shape
    n_p, n_s = bp * tp, bs * ts
    n_all = n_p + n_s
    n_pad = _round_up(n_all, PEER_TOKEN_BLOCK)
    tm_s = min(ROW_BLOCK, n_s)
    n_past = page_table.shape[1] * PAGE_SIZE
    depth = norm_mix.shape[0]
    md = MXU_DTYPE

    def all_rows(p, s):
        return jnp.concatenate([p, s, jnp.zeros((n_pad - n_all, p.shape[1]), p.dtype)], axis=0)

    x = all_rows(x_prompt.reshape(n_p, d_model), x_sample.reshape(n_s, d_model))

    lb_cum = jnp.cumsum(jax.nn.softmax(hgrn_lb.astype(F32), axis=0), axis=0)
    lower_bounds = lb_cum - lb_cum[:1]
    seg = jnp.kron(jnp.eye(C_WIDTH // RWKV_HEAD, dtype=F32), jnp.ones((RWKV_HEAD, RWKV_HEAD), F32))
    pool_dsa_k = jnp.transpose(cache_dsa_k, (0, 1, 3, 4, 2))
    pool_dsa_v = jnp.transpose(cache_dsa_v, (0, 1, 3, 4, 2))
    pool_dsa_idx = jnp.transpose(cache_dsa_idx, (0, 1, 3, 2))
    pool_mla_kr = jnp.transpose(cache_mla_krope, (0, 1, 3, 2))
    eye_h = jnp.eye(MLA_HEADS, dtype=F32)
    cos_p, sin_p = _rope_tables(jnp.arange(tp))
    cos_s, sin_s = (jnp.tile(a, (tm_s // ts, 1)) for a in _rope_tables(n_past + jnp.arange(ts)))

    outs = {k: [] for k in ("pk", "pv", "pi", "pc", "pr", "ph", "ps", "psh", "sk", "sv", "si", "sc", "sr", "sh", "ss", "ssh")}

    for l in range(depth):
        j = l // 2
        if l % 2 == 0:
            w_in = jnp.pad(w_in_even[j], ((0, 0), (0, _round_up(w_in_even.shape[2], LANES) - w_in_even.shape[2]))).astype(md)
            proj = _matmul(x, w_in, g=norm_mix[l], tm=ROW_BLOCK)
            aw = A_HEADS * LANES
            c_q, c_k, c_v = 4 * aw, 4 * aw + B_HEADS * B_DH, 4 * aw + (B_HEADS + B_KV_HEADS) * B_DH
            c_qi = c_v + B_KV_HEADS * B_DH
            c_ki = c_qi + IDX_HEADS * IDX_DIM
            c_wi = c_ki + IDX_DIM
            mixes = []
            for rows, b, t, s0_t, kk, kv, ki_key, kh in ((slice(0, n_p), bp, tp, None, "pk", "pv", "pi", "ph"),
                                                    (slice(n_p, n_all), bs, ts, state_hgrn[j], "sk", "sv", "si", "sh")):
                sample = s0_t is not None
                s0_t = jnp.swapaxes(s0_t, -1, -2).astype(F32) if sample else jnp.zeros((b, A_HEADS, LANES, LANES), F32)
                oa, s_a = _hgrn(proj, rows.start, b, t, A_HEADS, lower_bounds[j], hgrn_norm[j], s0_t)
                pr = proj[rows]
                qb, kb, vb = pr[:, c_q:c_k], pr[:, c_k:c_v], pr[:, c_v:c_qi]
                qi, ki, wi = pr[:, c_qi:c_ki], pr[:, c_ki:c_wi], pr[:, c_wi:c_wi + IDX_HEADS]
                q4 = _head_major(qb, b, t, B_HEADS).astype(md)
                k4 = _head_major(kb, b, t, B_KV_HEADS).astype(md)
                v4 = _head_major(vb, b, t, B_KV_HEADS).astype(md)
                ki3 = ki.reshape(b, t, IDX_DIM).astype(md)
                tq = t if sample else DSA_QUERY_BLOCK
                qi2 = jnp.transpose(qi.reshape(b, t // tq, tq, IDX_HEADS, IDX_DIM), (0, 1, 3, 2, 4))
                qi2 = qi2.reshape(b, t // tq, IDX_HEADS * tq, IDX_DIM).astype(md)
                wcol = jnp.transpose(wi.reshape(b, t // tq, tq, IDX_HEADS), (0, 1, 3, 2)).reshape(b, t // tq, IDX_HEADS * tq, 1)
                if sample:
                    topk = min(DSA_TOPK_MAX, (n_past + t) // 4)
                    mask = _dsa_select(page_table, qi2[:, 0], wcol[:, 0], _pad_rows(ki3), pool_dsa_idx, j, topk, t, IDX_HEADS)
                    group = B_HEADS // B_KV_HEADS
                    o4 = _dsa_attend(page_table, q4.reshape(b, B_KV_HEADS, group * t, B_DH), mask,
                                     _pad_rows(k4), _pad_rows(v4), pool_dsa_k, pool_dsa_v, j)
                    o4 = o4.reshape(b, B_HEADS, t, B_DH)
                else:
                    o4 = _dsa_prompt(qi2, wcol, ki3, q4, k4, v4, IDX_HEADS, tq)
                mixes.append(jnp.concatenate([oa, _token_major(o4)], axis=1))
                outs[kk].append(kb.reshape(b, t, B_KV_HEADS, B_DH))
                outs[kv].append(vb.reshape(b, t, B_KV_HEADS, B_DH))
                outs[ki_key].append(ki.reshape(b, t, IDX_DIM))
                outs[kh].append(jnp.swapaxes(s_a, -1, -2))
            x = _matmul(all_rows(*mixes), w_out_even[j].astype(md), res=x, tm=ROW_BLOCK)
        else:
            w = w_in_odd[j]
            c_qd, c_ckv = RWKV_IN, RWKV_IN + D_Q_RANK
            c_kr = c_ckv + D_KV_RANK
            w_in = jnp.concatenate([w[:, :RWKV_IN], w[:, c_ckv:c_kr], w[:, c_kr:c_kr + MLA_ROPE],
                                    jnp.zeros((d_model, 2 * LANES - MLA_ROPE), w.dtype), w[:, c_qd:c_ckv]], axis=1).astype(md)
            proj = _matmul(x, w_in, g=norm_mix[l], tm=ROW_BLOCK)
            pc_p = proj[:n_p, :RWKV_IN].reshape(bp, tp, RWKV_IN)
            pc_s = proj[n_p:n_all, :RWKV_IN].reshape(bs, ts, RWKV_IN)
            prev_p = jnp.concatenate([jnp.zeros((bp, 1, RWKV_IN), F32), pc_p[:, :-1]], axis=1)
            prev_s = jnp.concatenate([state_shift[j].astype(F32)[:, None], pc_s[:, :-1]], axis=1)
            prev = all_rows(prev_p.reshape(n_p, RWKV_IN), prev_s.reshape(n_s, RWKV_IN))
            zeros_l = jnp.zeros((rwkv_w_up.shape[1], C_WIDTH), F32)
            lora = jnp.concatenate([jnp.concatenate([rwkv_w_up[j], zeros_l], axis=1),
                                    jnp.concatenate([zeros_l, rwkv_a_up[j]], axis=1)], axis=0).astype(md)
            vecs = jnp.stack([rwkv_w0[j], rwkv_a0[j], rwkv_k_k[j], rwkv_k_a[j], rwkv_r_k[j].reshape(-1)]).astype(F32)
            prep = _rwkv_prep(proj, prev, rwkv_mu[j].reshape(1, -1).astype(F32), vecs, lora, rwkv_g_up[j].astype(md), seg, ROW_BLOCK)
            scan_in, g_all, bonus_all = prep[:6], prep[6], prep[7]
            ln = jnp.stack([rwkv_ln_w[j], rwkv_ln_b[j]]).astype(F32)

            wuq = mla_w_uq[j].reshape(D_Q_RANK, MLA_HEADS, MLA_NOPE + MLA_ROPE)
            wuq_rope = jnp.pad(wuq[:, :, MLA_NOPE:], ((0, 0), (0, 0), (0, LANES - MLA_ROPE))).reshape(D_Q_RANK, -1)
            wuq_p = jnp.concatenate([wuq[:, :, :MLA_NOPE].reshape(D_Q_RANK, -1), wuq_rope], axis=1).astype(md)
            wuk = jnp.einsum("chn,hg->hngc", mla_w_uk[j], eye_h).reshape(MLA_HEADS * MLA_NOPE, -1).astype(md)
            wuv = jnp.einsum("chv,hg->hcgv", mla_w_uv[j], eye_h).reshape(MLA_HEADS * D_KV_RANK, -1).astype(md)
            q_norm = mla_q_norm[j].reshape(1, -1).astype(F32)
            kv_norm = mla_kv_norm[j].reshape(1, -1).astype(F32)

            mixes = []
            for start, b, t, tm, cos, sin, sample in ((0, bp, tp, ROW_BLOCK, cos_p, sin_p, False),
                                                      (n_p, bs, ts, tm_s, cos_s, sin_s, True)):
                n = b * t
                if sample:
                    heads = C_WIDTH // RWKV_HEAD
                    s0 = state_rwkv[j].astype(F32).reshape(b, heads // 2, 2, RWKV_HEAD, RWKV_HEAD)
                    s0 = jnp.transpose(s0, (0, 1, 3, 2, 4)).reshape(b, heads // 2 * RWKV_HEAD, 2 * RWKV_HEAD)
                else:
                    s0 = jnp.zeros((b, C_WIDTH // LANES * RWKV_HEAD, LANES), F32)
                nb = math.gcd(b, RWKV_BATCH_BLOCK)
                y3, s_c = _rwkv_scan(scan_in, start, b, t, s0, seg[:LANES, :LANES].astype(md), nb)
                yc = _rwkv_post(y3.reshape(n, C_WIDTH), bonus_all, g_all, ln, seg, start, tm)
                s_c = jnp.transpose(s_c.reshape(b, -1, RWKV_HEAD, 2, RWKV_HEAD), (0, 1, 3, 2, 4))
                s_c = s_c.reshape(b, -1, RWKV_HEAD, RWKV_HEAD)

                q_lat, q_rope, c_new, kr_new = _mla_prep(proj, ODD_SLABS, start, n, cos, sin, q_norm, kv_norm, wuq_p, wuk, tm)
                if sample:
                    hq = lambda a: _head_major(a, b, t, MLA_HEADS).reshape(b, MLA_HEADS * t, -1)
                    o_lat = _mla_sample(page_table, hq(q_lat), hq(q_rope),
                                        _pad_rows(c_new.reshape(b, t, -1)).astype(md),
                                        _pad_rows(kr_new.reshape(b, t, -1)).astype(md),
                                        cache_mla_ckv, pool_mla_kr, j, t)
                    o_lat = _token_major(o_lat.reshape(b, MLA_HEADS, t, -1))
                    od = _matmul(o_lat, wuv, tm=tm)
                else:
                    od = _mla_prompt(q_lat, q_rope, c_new.astype(md), kr_new.astype(md), wuv, b, t)
                mixes.append(jnp.concatenate([yc, od], axis=1))
                pre = "s" if sample else "p"
                outs[pre + "c"].append(c_new.reshape(b, t, D_KV_RANK))
                outs[pre + "r"].append(kr_new[:, :MLA_ROPE].reshape(b, t, MLA_ROPE))
                outs[pre + "s"].append(s_c)
                outs[pre + "sh"].append((pc_s if sample else pc_p)[:, -1])
            x = _matmul(all_rows(*mixes), w_out_odd[j].astype(md), res=x, tm=ROW_BLOCK)
        x = _peer_ffn(x, norm_ffn[l], _peer_prepare(peer_wq[l], peer_subkeys[l], peer_u[l], peer_v[l]))

    y = _rmsnorm(x, norm_final)
    y_prompt = y[:n_p].reshape(bp, tp, d_model)
    y_sample = y[n_p:n_all].reshape(bs, ts, d_model)
    st = lambda k: jnp.stack(outs[k])
    return (y_prompt, y_sample,
            st("pk"), st("pv"), st("pi"), st("pc"), st("pr"), st("ph"), st("ps"), st("psh"),
            st("sk"), st("sv"), st("si"), st("sc"), st("sr"), st("sh"), st("ss"), st("ssh"))
```

```python
import functools
import math

import jax
import jax.numpy as jnp
from jax import lax
from jax.experimental import pallas as pl
from jax.experimental.pallas import tpu as pltpu

F32 = jnp.float32
I32 = jnp.int32
MXU_DTYPE = jnp.bfloat16
GATE_DTYPE = jnp.bfloat16
LANES = 128
SUBLANES = 8
VMEM_LIMIT_BYTES = 56 * 1024 * 1024
NEG_INF = float("-inf")

RMS_EPS = 1e-6
PAGE_SIZE = 128
ROPE_THETA = 10000.0
PEER_HEADS = 8
PEER_NKEYS = 128
PEER_TOPK = 16
PEER_TOKEN_BLOCK = 512
PEER_GROUP = 16


def _cparams(*sem):
    return pltpu.CompilerParams(dimension_semantics=sem, vmem_limit_bytes=VMEM_LIMIT_BYTES)


def _round_up(n, m):
    return -(-n // m) * m


def _rms(x, g):
    return x * lax.rsqrt(jnp.mean(x * x, axis=-1, keepdims=True) + RMS_EPS) * g


def _dot(a, b):
    return jnp.dot(a, b, preferred_element_type=F32)


def _dot_nt(a, b):
    return lax.dot_general(a, b, (((1,), (1,)), ((), ())), preferred_element_type=F32)


def _matmul_kernel(*refs, norm, residual):
    it = iter(refs)
    a_ref = next(it)
    g_ref = next(it) if norm else None
    w_ref = next(it)
    r_ref = next(it) if residual else None
    o_ref = next(it)
    a = a_ref[...]
    if norm:
        a = _rms(a, g_ref[...])
    acc = _dot(a.astype(w_ref.dtype), w_ref[...])
    if residual:
        acc = acc + r_ref[...]
    o_ref[...] = acc


def _matmul(a, w, g=None, res=None, tm=256):
    n, k = a.shape
    m = w.shape[1]
    ins, specs = [a], [pl.BlockSpec((tm, k), lambda i: (i, 0))]
    if g is not None:
        ins.append(g.reshape(1, k).astype(F32))
        specs.append(pl.BlockSpec((1, k), lambda i: (0, 0)))
    ins.append(w)
    specs.append(pl.BlockSpec((k, m), lambda i: (0, 0)))
    if res is not None:
        ins.append(res)
        specs.append(pl.BlockSpec((tm, m), lambda i: (i, 0)))
    return pl.pallas_call(
        functools.partial(_matmul_kernel, norm=g is not None, residual=res is not None),
        out_shape=jax.ShapeDtypeStruct((n, m), F32),
        grid=(n // tm,),
        in_specs=specs,
        out_specs=pl.BlockSpec((tm, m), lambda i: (i, 0)),
        compiler_params=_cparams("parallel"),
        name="proj_matmul",
    )(*ins)


def _rmsnorm_kernel(x_ref, g_ref, o_ref):
    o_ref[...] = _rms(x_ref[...], g_ref[...])


def _rmsnorm(x, g, tm=512):
    n, d = x.shape
    return pl.pallas_call(
        _rmsnorm_kernel,
        out_shape=jax.ShapeDtypeStruct((n, d), F32),
        grid=(n // tm,),
        in_specs=[pl.BlockSpec((tm, d), lambda i: (i, 0)), pl.BlockSpec((1, d), lambda i: (0, 0))],
        out_specs=pl.BlockSpec((tm, d), lambda i: (i, 0)),
        compiler_params=_cparams("parallel"),
        name="final_rmsnorm",
    )(x, g.reshape(1, d).astype(F32))


def _peer_fold_kernel(sub_ref, wq_ref, o_ref):
    o_ref[...] = _dot_nt(sub_ref[...], wq_ref[...])


def _peer_fold(subkeys, wq):
    d_model = wq.shape[0]
    half = subkeys.shape[2]
    return pl.pallas_call(
        _peer_fold_kernel,
        out_shape=jax.ShapeDtypeStruct((2, PEER_HEADS, PEER_NKEYS, d_model), F32),
        grid=(2, PEER_HEADS),
        in_specs=[pl.BlockSpec((None, PEER_NKEYS, half), lambda p, h: (p, 0, 0)),
                  pl.BlockSpec((d_model, half), lambda p, h: (0, h * 2 + p))],
        out_specs=pl.BlockSpec((None, None, PEER_NKEYS, d_model), lambda p, h: (p, h, 0, 0)),
        compiler_params=_cparams("parallel", "parallel"),
        name="peer_fold",
    )(subkeys.astype(MXU_DTYPE), wq.astype(MXU_DTYPE))


def _peer_scores_kernel(x_ref, g_ref, m_ref, s_ref, h_ref):
    hb = _rms(x_ref[...], g_ref[...]).astype(MXU_DTYPE)
    s_ref[...] = _dot_nt(m_ref[...], hb)
    h_ref[...] = hb.T


def _peer_scores(x, g, mcat, tm=PEER_TOKEN_BLOCK):
    n, d = x.shape
    rows = mcat.shape[0]
    return pl.pallas_call(
        _peer_scores_kernel,
        out_shape=(jax.ShapeDtypeStruct((rows, n), F32), jax.ShapeDtypeStruct((d, n), MXU_DTYPE)),
        grid=(n // tm,),
        in_specs=[pl.BlockSpec((tm, d), lambda i: (i, 0)),
                  pl.BlockSpec((1, d), lambda i: (0, 0)),
                  pl.BlockSpec((rows, d), lambda i: (0, 0))],
        out_specs=(pl.BlockSpec((rows, tm), lambda i: (0, i)), pl.BlockSpec((d, tm), lambda i: (0, i))),
        compiler_params=_cparams("parallel"),
        name="peer_scores",
    )(x, g.reshape(1, d).astype(F32), mcat)


def _bitonic_merge_desc(v):
    n = len(v)
    if n == 1:
        return v
    half = n // 2
    hi = [jnp.maximum(v[i], v[i + half]) for i in range(half)]
    lo = [jnp.minimum(v[i], v[i + half]) for i in range(half)]
    return _bitonic_merge_desc(hi) + _bitonic_merge_desc(lo)


def _sort_desc(v):
    n = len(v)
    if n == 1:
        return v
    return _bitonic_merge_desc(_sort_desc(v[:n // 2]) + _sort_desc(v[n // 2:])[::-1])


def _merge_top(x, y):
    n = len(x)
    return _bitonic_merge_desc([jnp.maximum(x[i], y[n - 1 - i]) for i in range(n)])


def _peer_topk_kernel(s_ref, c1_ref, e1_ref, r2_ref, e2_ref):
    hk = PEER_HEADS * PEER_NKEYS
    tn = s_ref.shape[1]

    def top_values(base):
        runs = []
        for r0 in range(0, PEER_NKEYS, PEER_TOPK):
            rows = [s_ref[base + (r0 + i) * PEER_HEADS:base + (r0 + i + 1) * PEER_HEADS, :] for i in range(PEER_TOPK)]
            runs.append(_sort_desc(rows))
        while len(runs) > 1:
            runs = [_merge_top(runs[i], runs[i + 1]) for i in range(0, len(runs), 2)]
        return runs[0]

    a = top_values(0)
    b = top_values(hk)
    neg = jnp.full((PEER_HEADS, tn), NEG_INF, F32)
    lists = [[a[k] + b[l] if (k + 1) * (l + 1) <= PEER_TOPK else neg for l in range(PEER_TOPK)] for k in range(PEER_TOPK)]
    while len(lists) > 1:
        lists = [_merge_top(lists[i], lists[i + 1]) for i in range(0, len(lists), 2)]
    best = lists[0]
    tau = best[-1]
    z = jnp.zeros_like(tau)
    for c in best:
        z = z + jnp.exp(c - best[0])
    inv_z = 1.0 / z

    twice = lambda x: jnp.concatenate([x, x], axis=0)
    tau2, a0_2, inv_z2, b2 = twice(tau), twice(a[0]), twice(inv_z), [twice(x) for x in b]

    def first_half(i, carry):
        rows = pl.ds(pl.multiple_of(i * 2 * PEER_HEADS, 2 * PEER_HEADS), 2 * PEER_HEADS)
        x = s_ref[rows, :]
        cnt = jnp.zeros_like(x)
        for bl in b2:
            cnt = cnt + jnp.where(x + bl >= tau2, 1.0, 0.0)
        c1_ref[rows, :] = cnt.astype(c1_ref.dtype)
        e1_ref[rows, :] = (jnp.exp(x - a0_2) * inv_z2).astype(e1_ref.dtype)
        return carry

    lax.fori_loop(0, PEER_NKEYS // 2, first_half, 0)
    for h in range(PEER_HEADS):
        x = s_ref[pl.ds(2 * hk + h * PEER_NKEYS, PEER_NKEYS), :]
        rank = jnp.zeros_like(x)
        for bl in b:
            rank = rank + jnp.where(bl[h:h + 1, :] > x, 1.0, 0.0)
        rows = slice(h * PEER_NKEYS, (h + 1) * PEER_NKEYS)
        r2_ref[rows, :] = rank.astype(r2_ref.dtype)
        e2_ref[rows, :] = jnp.exp(x - b[0][h:h + 1, :]).astype(e2_ref.dtype)


def _peer_topk(s_t, tn=PEER_TOKEN_BLOCK):
    rows, n = s_t.shape
    hk = PEER_HEADS * PEER_NKEYS
    out = jax.ShapeDtypeStruct((hk, n), GATE_DTYPE)
    spec = pl.BlockSpec((hk, tn), lambda i: (0, i))
    return pl.pallas_call(
        _peer_topk_kernel,
        out_shape=(out,) * 4,
        grid=(n // tn,),
        in_specs=[pl.BlockSpec((rows, tn), lambda i: (0, i))],
        out_specs=(spec,) * 4,
        compiler_params=_cparams("parallel"),
        name="peer_topk",
    )(s_t)


def _gelu(x):
    return 0.5 * x * (1.0 + lax.erf(x * (1.0 / math.sqrt(2.0))))


PEER_QUAD = 4 * PEER_NKEYS
GATE_TILE = (64, 256)


def _peer_experts_kernel(x_ref, h_ref, c1_ref, e1_ref, r2_ref, e2_ref, u_ref, v_ref, o_ref, acc_ref, *a_refs):
    @pl.when(pl.program_id(1) == 0)
    def _():
        acc_ref[...] = jnp.zeros_like(acc_ref)

    hb = h_ref[...]
    tn = hb.shape[1]
    te, tt = GATE_TILE
    n_quads = u_ref.shape[0] // PEER_QUAD
    scores = lambda quad: _dot(u_ref[quad * PEER_QUAD:(quad + 1) * PEER_QUAD, :], hb)
    st_next = scores(0)
    for quad in range(n_quads):
        st, st_next = st_next, (scores(quad + 1) if quad + 1 < n_quads else None)
        a_ref = a_refs[quad % 2]
        for sub in range(PEER_QUAD // PEER_NKEYS):
            i1 = quad * (PEER_QUAD // PEER_NKEYS) + sub
            c1 = c1_ref[i1 * PEER_HEADS:(i1 + 1) * PEER_HEADS, :]
            e1 = e1_ref[i1 * PEER_HEADS:(i1 + 1) * PEER_HEADS, :]
            for r0 in range(0, PEER_NKEYS, te):
                for l0 in range(0, tn, tt):
                    lanes = slice(l0, l0 + tt)
                    gate = None
                    for h in range(PEER_HEADS):
                        rows = slice(h * PEER_NKEYS + r0, h * PEER_NKEYS + r0 + te)
                        sel = r2_ref[rows, lanes] < c1[h:h + 1, lanes]
                        term = jnp.where(sel, e2_ref[rows, lanes], 0.0) * e1[h:h + 1, lanes]
                        gate = term if gate is None else gate + term
                    rows = slice(sub * PEER_NKEYS + r0, sub * PEER_NKEYS + r0 + te)
                    a_ref[rows, lanes] = (_gelu(st[rows, lanes]).astype(gate.dtype) * gate).astype(a_ref.dtype)
        acc_ref[...] += _dot(v_ref[quad], a_ref[...])

    @pl.when(pl.program_id(1) == pl.num_programs(1) - 1)
    def _():
        o_ref[...] = x_ref[...] + acc_ref[...].T


def _peer_experts(x, h_t, c1, e1, r2, e2, u2, v3, tn=PEER_TOKEN_BLOCK):
    d, n = h_t.shape
    hk = PEER_HEADS * PEER_NKEYS
    ge = PEER_GROUP * PEER_NKEYS
    tok = lambda rows: pl.BlockSpec((rows, tn), lambda i, g: (0, i))
    grp = pl.BlockSpec((PEER_GROUP * PEER_HEADS, tn), lambda i, g: (g, i))
    return pl.pallas_call(
        _peer_experts_kernel,
        out_shape=jax.ShapeDtypeStruct((n, d), F32),
        grid=(n // tn, u2.shape[0] // ge),
        in_specs=[pl.BlockSpec((tn, d), lambda i, g: (i, 0)), tok(d), grp, grp, tok(hk), tok(hk),
                  pl.BlockSpec((ge, d), lambda i, g: (g, 0)),
                  pl.BlockSpec((ge // PEER_QUAD, d, PEER_QUAD), lambda i, g: (g, 0, 0))],
        out_specs=pl.BlockSpec((tn, d), lambda i, g: (i, 0)),
        scratch_shapes=[pltpu.VMEM((d, tn), F32)] + [pltpu.VMEM((PEER_QUAD, tn), MXU_DTYPE)] * 2,
        compiler_params=_cparams("parallel", "arbitrary"),
        name="peer_experts",
    )(x, h_t, c1, e1, r2, e2, u2, v3)


def _peer_prepare(wq, subkeys, u, v):
    d = wq.shape[0]
    hk = PEER_HEADS * PEER_NKEYS
    mf = _peer_fold(subkeys, wq)
    inter = jnp.transpose(mf, (0, 2, 1, 3)).reshape(2 * hk, d)
    mcat = jnp.concatenate([inter, mf[1].reshape(hk, d)], axis=0).astype(MXU_DTYPE)
    v3 = jnp.transpose(v.astype(MXU_DTYPE).reshape(-1, PEER_QUAD, d), (0, 2, 1))
    return mcat, u.astype(MXU_DTYPE), v3


def _peer_ffn(x, g, prep):
    mcat, u2, v3 = prep
    s_t, h_t = _peer_scores(x, g, mcat)
    return _peer_experts(x, h_t, *_peer_topk(s_t), u2, v3)


HGRN_CHUNK = 64
HGRN_SUB = 16
MASKED_EXPONENT = -1e30


def _cumsum_rows(x):
    rows = x.shape[0]
    row = lax.broadcasted_iota(I32, x.shape, 0)
    d = 1
    while d < rows:
        x = x + jnp.where(row >= d, pltpu.roll(x, d, 0), 0.0)
        d *= 2
    return x


def _hgrn_chunk(q, k, v, g, s_t, sub):
    c = q.shape[0]
    cum = _cumsum_rows(g)
    o = _dot_nt((q * jnp.exp(cum)).astype(MXU_DTYPE), s_t.astype(MXU_DTYPE))
    outs = []
    for blk in range(c // sub):
        r0 = blk * sub
        q_b, cum_b, k_b, v_b = q[r0:r0 + sub], cum[r0:r0 + sub], k[r0:r0 + sub], v[r0:r0 + sub]
        o_b = o[r0:r0 + sub]
        if blk > 0:
            base = cum[r0 - 1:r0]
            qs = q_b * jnp.exp(cum_b - base)
            ks = k[0:r0] * jnp.exp(base - cum[0:r0])
            att = _dot_nt(qs.astype(MXU_DTYPE), ks.astype(MXU_DTYPE))
            o_b = o_b + _dot(att.astype(MXU_DTYPE), v[0:r0].astype(MXU_DTYPE))
        row = lax.broadcasted_iota(I32, (sub, q.shape[1]), 0)
        for s in range(sub):
            dec = jnp.exp(jnp.where(row >= s, cum_b - cum_b[s:s + 1], MASKED_EXPONENT))
            att = jnp.sum(q_b * k_b[s:s + 1] * dec, axis=1, keepdims=True)
            o_b = o_b + att * v_b[s:s + 1]
        outs.append(o_b)
    o = outs[0] if len(outs) == 1 else jnp.concatenate(outs, axis=0)
    last = cum[c - 1:c]
    kd = k * jnp.exp(last - cum)
    upd = lax.dot_general(v.astype(MXU_DTYPE), kd.astype(MXU_DTYPE), (((0,), (0,)), ((), ())),
                          preferred_element_type=F32)
    return o, s_t * jnp.exp(last) + upd


def _hgrn_kernel(q_ref, f_ref, i_ref, g_ref, lb_ref, gain_ref, s0_ref, o_ref, s_ref, *, chunk, sub):
    @pl.when(pl.program_id(2) == 0)
    def _():
        s_ref[...] = s0_ref[...]

    lb = lb_ref[...]
    s_t = s_ref[...]
    for c0 in range(0, q_ref.shape[0], chunk):
        rows = slice(c0, c0 + chunk)
        f = lb + (1.0 - lb) * jax.nn.sigmoid(f_ref[rows, :])
        o, s_t = _hgrn_chunk(jax.nn.silu(q_ref[rows, :]), 1.0 - f, i_ref[rows, :], jnp.log(f), s_t, sub)
        o_ref[rows, :] = _rms(o, gain_ref[...]) * jax.nn.silu(g_ref[rows, :])
    s_ref[...] = s_t


def _hgrn(proj, row_start, batch, t, heads, lb, gain, s0_t):
    chunk = math.gcd(t, HGRN_CHUNK)
    sub = min(HGRN_SUB, chunk)
    tc = min(t, 4 * chunk)
    nt = t // tc
    rb0 = row_start // tc
    dk = LANES

    def slab(k):
        return pl.BlockSpec((tc, dk), lambda b, h, i, k=k: (rb0 + b * nt + i, k * heads + h))

    vec = pl.BlockSpec((1, dk), lambda b, h, i: (0, h))
    st = pl.BlockSpec((None, None, dk, dk), lambda b, h, i: (b, h, 0, 0))
    return pl.pallas_call(
        functools.partial(_hgrn_kernel, chunk=chunk, sub=sub),
        out_shape=(jax.ShapeDtypeStruct((batch * t, heads * dk), F32),
                   jax.ShapeDtypeStruct((batch, heads, dk, dk), F32)),
        grid=(batch, heads, nt),
        in_specs=[slab(0), slab(1), slab(2), slab(3), vec, vec, st],
        out_specs=(pl.BlockSpec((tc, dk), lambda b, h, i: (b * nt + i, h)), st),
        compiler_params=_cparams("parallel", "parallel", "arbitrary"),
        name="hgrn2",
    )(proj, proj, proj, proj, lb.reshape(1, -1), gain.reshape(1, -1), s0_t)


DSA_TOPK_MAX = 256
DSA_QUERY_BLOCK = 128
INT32_MIN = -2 ** 31


def _count(m):
    return jnp.sum(jnp.where(m, 1.0, 0.0), axis=1, keepdims=True)


def _float_key(x):
    u = lax.bitcast_convert_type(x, I32)
    return u ^ ((u >> 31) & I32(0x7FFFFFFF))


KEY_NEG_INF = INT32_MIN + 0x007FFFFF
INT32_MAX = 2 ** 31 - 1


def _topk_thresholds(count, k, idx_bits, rows):
    kf = float(k)
    v = jnp.where(count(lambda key, idx: key >= 0) >= kf, I32(0), I32(INT32_MIN))

    def value_bit(it, v):
        t = v | (I32(1) << (I32(30) - it))
        return jnp.where(count(lambda key, idx: key >= t) >= kf, t, v)

    v = lax.fori_loop(0, 31, value_bit, v)
    at_least = count(lambda key, idx: key >= v)

    def break_ties():
        need = kf - count(lambda key, idx: key > v)

        def index_bit(it, j):
            t = j | (I32(1) << (I32(idx_bits - 1) - it))
            return jnp.where(count(lambda key, idx: (key == v) & (idx < t)) < need, t, j)

        return lax.fori_loop(0, idx_bits, index_bit, jnp.zeros((rows, 1), I32))

    j = lax.cond(jnp.max(at_least) > kf, break_ties, lambda: jnp.full((rows, 1), INT32_MAX, I32))
    return v, j


def _selected(key, idx, v, j):
    return ((key > v) | ((key == v) & (idx <= j))) & (key > KEY_NEG_INF)


def _topk_mask(scores, k, idx_bits):
    key = _float_key(scores)
    idx = lax.broadcasted_iota(I32, scores.shape, 1)
    v, j = _topk_thresholds(lambda pred: _count(pred(key, idx)), k, idx_bits, scores.shape[0])
    return _selected(key, idx, v, j)


def _index_scores(qi, w, ki, heads, keys_transposed=False):
    d = qi.shape[1]
    dots = (_dot(qi, ki) if keys_transposed else _dot_nt(qi, ki)) * (d ** -0.5)
    terms = jnp.maximum(dots, 0.0) * (w * (heads ** -0.5))
    r = qi.shape[0] // heads
    acc = terms[0:r]
    for h in range(1, heads):
        acc = acc + terms[h * r:(h + 1) * r]
    return acc


def _masked_softmax_pv(s, mask, v):
    s = jnp.where(mask, s, NEG_INF)
    p = jnp.exp(s - jnp.max(s, axis=1, keepdims=True))
    return _dot(p.astype(MXU_DTYPE), v) / jnp.sum(p, axis=1, keepdims=True)


PAGES_PER_STEP = 32
ONLINE_SOFTMAX_FLOOR = -1e30
KEY_BLOCK = 512


def _dsa_prompt_kernel(qi_ref, w_ref, ki_ref, q_ref, k_ref, v_ref, o_ref, *, topk, hi, kb_size):
    heads, tq, dh = q_ref.shape
    kv_heads = k_ref.shape[0]
    t = ki_ref.shape[0]
    q0 = pl.program_id(1) * tq
    n_kb = (q0 + tq - 1) // kb_size + 1

    def attend(s_keys):
        qpos = q0 + lax.broadcasted_iota(I32, (tq, s_keys), 0)
        kpos = lax.broadcasted_iota(I32, (tq, s_keys), 1)
        scores = _index_scores(qi_ref[...], w_ref[...], ki_ref[0:s_keys, :], hi)
        mask = _topk_mask(jnp.where(kpos <= qpos, scores, NEG_INF), topk, max(1, (s_keys - 1).bit_length()))
        for h in range(heads):
            g = h // (heads // kv_heads)
            s = _dot_nt(q_ref[h], k_ref[g, 0:s_keys, :]) * (dh ** -0.5)
            o_ref[h] = _masked_softmax_pv(s, mask, v_ref[g, 0:s_keys, :])

    for n in range(1, t // kb_size + 1):
        pl.when(n_kb == n)(functools.partial(attend, n * kb_size))


def _dsa_prompt(qi2, wcol, ki, q4, k4, v4, hi, tq=128):
    b, t, d = ki.shape
    _, h, _, dh = q4.shape
    hkv = k4.shape[1]
    topk = min(DSA_TOPK_MAX, t // 4)
    kb = math.gcd(t, KEY_BLOCK)
    return pl.pallas_call(
        functools.partial(_dsa_prompt_kernel, topk=topk, hi=hi, kb_size=kb),
        out_shape=jax.ShapeDtypeStruct((b, h, t, dh), F32),
        grid=(b, t // tq),
        in_specs=[pl.BlockSpec((None, None, hi * tq, d), lambda n, i: (n, i, 0, 0)),
                  pl.BlockSpec((None, None, hi * tq, 1), lambda n, i: (n, i, 0, 0)),
                  pl.BlockSpec((None, t, d), lambda n, i: (n, 0, 0)),
                  pl.BlockSpec((None, h, tq, dh), lambda n, i: (n, 0, i, 0)),
                  pl.BlockSpec((None, hkv, t, dh), lambda n, i: (n, 0, 0, 0)),
                  pl.BlockSpec((None, hkv, t, dh), lambda n, i: (n, 0, 0, 0))],
        out_specs=pl.BlockSpec((None, h, tq, dh), lambda n, i: (n, 0, i, 0)),
        compiler_params=_cparams("parallel", "arbitrary"),
        name="dsa_prompt",
    )(qi2, wcol, ki, q4, k4, v4)


def _page_specs(block, layer, pages):
    zeros = (0,) * (len(block) - 2)
    return [pl.BlockSpec(block, lambda b, c, pt, r=r: (layer, pt[b, c * pages + r]) + zeros) for r in range(pages)]


DSA_PICK_BATCH = 8


def _dsa_scores_kernel(pt_ref, qi_ref, w_ref, kn_ref, *rest, pages, t_new, hi):
    page_refs, (sc_ref, new_ref) = rest[:pages], rest[pages:]
    kc = jnp.concatenate([r[...] for r in page_refs], axis=1).astype(MXU_DTYPE)
    sc_ref[...] = _index_scores(qi_ref[...], w_ref[...], kc, hi, keys_transposed=True)

    @pl.when(pl.program_id(1) == pl.num_programs(1) - 1)
    def _():
        new = _index_scores(qi_ref[...], w_ref[...], kn_ref[...], hi)
        qpos = lax.broadcasted_iota(I32, new.shape, 0)
        kpos = lax.broadcasted_iota(I32, new.shape, 1)
        new_ref[...] = jnp.where((kpos <= qpos) & (kpos < t_new), new, NEG_INF)


def _dsa_pick_kernel(sc_ref, new_ref, mask_ref, *, topk):
    nbat, n_chunks, rows, ch = sc_ref.shape
    scores = jnp.concatenate([jnp.concatenate([sc_ref[bb, i] for i in range(n_chunks)] + [new_ref[bb]], axis=1)
                              for bb in range(nbat)], axis=0)
    mask = jnp.where(_topk_mask(scores, topk, scores.shape[1].bit_length()), 1.0, 0.0)
    pad = jnp.zeros((rows, ch - new_ref.shape[2]), F32)
    for bb in range(nbat):
        for i in range(n_chunks):
            mask_ref[bb, i] = mask[bb * rows:(bb + 1) * rows, i * ch:(i + 1) * ch]
        mask_ref[bb, n_chunks] = jnp.concatenate([mask[bb * rows:(bb + 1) * rows, n_chunks * ch:], pad], axis=1)


def _dsa_select(page_table, qi2, wcol, ki_new_pad, pool_idx, layer, topk, t_new, hi):
    b, n_pages = page_table.shape
    pages = math.gcd(n_pages, PAGES_PER_STEP)
    nc = n_pages // pages
    ch = pages * PAGE_SIZE
    _, rows, d = qi2.shape
    t = rows // hi
    grid_spec = pltpu.PrefetchScalarGridSpec(
        num_scalar_prefetch=1,
        grid=(b, nc),
        in_specs=[pl.BlockSpec((None, rows, d), lambda n, c, pt: (n, 0, 0)),
                  pl.BlockSpec((None, rows, 1), lambda n, c, pt: (n, 0, 0)),
                  pl.BlockSpec((None, LANES, d), lambda n, c, pt: (n, 0, 0))]
        + _page_specs((None, None, d, PAGE_SIZE), layer, pages),
        out_specs=(pl.BlockSpec((None, None, t, ch), lambda n, c, pt: (n, c, 0, 0)),
                   pl.BlockSpec((None, t, LANES), lambda n, c, pt: (n, 0, 0))),
    )
    scores, new = pl.pallas_call(
        functools.partial(_dsa_scores_kernel, pages=pages, t_new=t_new, hi=hi),
        out_shape=(jax.ShapeDtypeStruct((b, nc, t, ch), F32), jax.ShapeDtypeStruct((b, t, LANES), F32)),
        grid_spec=grid_spec,
        compiler_params=_cparams("parallel", "arbitrary"),
        name="dsa_sample_scores",
    )(page_table, qi2, wcol, ki_new_pad, *([pool_idx] * pages))
    nbat = math.gcd(b, DSA_PICK_BATCH)
    return pl.pallas_call(
        functools.partial(_dsa_pick_kernel, topk=topk),
        out_shape=jax.ShapeDtypeStruct((b, nc + 1, t, ch), F32),
        grid=(b // nbat,),
        in_specs=[pl.BlockSpec((nbat, nc, t, ch), lambda n: (n, 0, 0, 0)),
                  pl.BlockSpec((nbat, t, LANES), lambda n: (n, 0, 0))],
        out_specs=pl.BlockSpec((nbat, nc + 1, t, ch), lambda n: (n, 0, 0, 0)),
        compiler_params=_cparams("parallel"),
        name="dsa_sample_select",
    )(scores, new)


def _online_softmax_step(m_ref, l_ref, acc_ref, j, s, mask, v, values_transposed=False):
    m_old = m_ref[j]
    m_new = jnp.maximum(m_old, jnp.max(jnp.where(mask, s, ONLINE_SOFTMAX_FLOOR), axis=1, keepdims=True))
    alpha = jnp.exp(m_old - m_new)
    p = jnp.where(mask, jnp.exp(s - m_new), 0.0)
    l_ref[j] = alpha * l_ref[j] + jnp.sum(p, axis=1, keepdims=True)
    pv = _dot_nt(p.astype(MXU_DTYPE), v) if values_transposed else _dot(p.astype(MXU_DTYPE), v)
    acc_ref[j] = alpha * acc_ref[j] + pv
    m_ref[j] = m_new


def _online_softmax_heads(m_ref, l_ref, acc_ref, scores, mask, pv_fn):
    rows = scores[0].shape[0]
    s = jnp.concatenate(scores, axis=0)
    mask = jnp.concatenate([mask] * (s.shape[0] // mask.shape[0]), axis=0)
    m_old = m_ref[...]
    m_new = jnp.maximum(m_old, jnp.max(jnp.where(mask, s, ONLINE_SOFTMAX_FLOOR), axis=1, keepdims=True))
    alpha = jnp.exp(m_old - m_new)
    p = jnp.where(mask, jnp.exp(s - m_new), 0.0)
    l_ref[...] = alpha * l_ref[...] + jnp.sum(p, axis=1, keepdims=True)
    pb = p.astype(MXU_DTYPE)
    pv = jnp.concatenate([pv_fn(g, pb[g * rows:(g + 1) * rows]) for g in range(len(scores))], axis=0)
    acc_ref[...] = alpha * acc_ref[...] + pv
    m_ref[...] = m_new


def _dsa_attend_kernel(pt_ref, q_ref, mc_ref, mn_ref, kn_ref, vn_ref, *rest, pages):
    k_pages, v_pages = rest[:pages], rest[pages:2 * pages]
    o_ref, m_ref, l_ref, acc_ref = rest[2 * pages:]
    c = pl.program_id(1)
    kv_heads, rows, dh = q_ref.shape
    scale = dh ** -0.5

    @pl.when(c == 0)
    def _():
        m_ref[...] = jnp.full_like(m_ref, ONLINE_SOFTMAX_FLOOR)
        l_ref[...] = jnp.zeros_like(l_ref)
        acc_ref[...] = jnp.zeros_like(acc_ref)

    def keys_t(refs, g):
        return jnp.concatenate([r[g] for r in refs], axis=1).astype(MXU_DTYPE)

    scores = [_dot(q_ref[g], keys_t(k_pages, g)) * scale for g in range(kv_heads)]
    _online_softmax_heads(m_ref, l_ref, acc_ref, scores, mc_ref[...] > 0.0,
                          lambda g, p: _dot_nt(p, keys_t(v_pages, g)))

    @pl.when(c == pl.num_programs(1) - 1)
    def _():
        new = [_dot_nt(q_ref[g], kn_ref[g]) * scale for g in range(kv_heads)]
        _online_softmax_heads(m_ref, l_ref, acc_ref, new, mn_ref[:, 0:LANES] > 0.0, lambda g, p: _dot(p, vn_ref[g]))
        o_ref[...] = (acc_ref[...] / l_ref[...]).reshape(o_ref.shape)


def _dsa_attend(page_table, q4, mask, k_new_pad, v_new_pad, pool_k, pool_v, layer):
    b, n_pages = page_table.shape
    pages = math.gcd(n_pages, PAGES_PER_STEP)
    nc = n_pages // pages
    _, hkv, rows, dh = q4.shape
    _, _, t, ch = mask.shape
    grid_spec = pltpu.PrefetchScalarGridSpec(
        num_scalar_prefetch=1,
        grid=(b, nc),
        in_specs=[pl.BlockSpec((None, hkv, rows, dh), lambda n, c, pt: (n, 0, 0, 0)),
                  pl.BlockSpec((None, None, t, ch), lambda n, c, pt: (n, c, 0, 0)),
                  pl.BlockSpec((None, None, t, ch), lambda n, c, pt: (n, nc, 0, 0)),
                  pl.BlockSpec((None, hkv, LANES, dh), lambda n, c, pt: (n, 0, 0, 0)),
                  pl.BlockSpec((None, hkv, LANES, dh), lambda n, c, pt: (n, 0, 0, 0))]
        + _page_specs((None, None, hkv, dh, PAGE_SIZE), layer, pages)
        + _page_specs((None, None, hkv, dh, PAGE_SIZE), layer, pages),
        out_specs=pl.BlockSpec((None, hkv, rows, dh), lambda n, c, pt: (n, 0, 0, 0)),
        scratch_shapes=[pltpu.VMEM((hkv * rows, 1), F32), pltpu.VMEM((hkv * rows, 1), F32),
                        pltpu.VMEM((hkv * rows, dh), F32)],
    )
    return pl.pallas_call(
        functools.partial(_dsa_attend_kernel, pages=pages),
        out_shape=jax.ShapeDtypeStruct((b, hkv, rows, dh), F32),
        grid_spec=grid_spec,
        compiler_params=_cparams("parallel", "arbitrary"),
        name="dsa_sample_attend",
    )(page_table, q4, mask, mask, k_new_pad, v_new_pad, *([pool_k] * pages), *([pool_v] * pages))


GN_EPS = 64e-5
RWKV_HEAD = 64
RWKV_BATCH_BLOCK = 4


def _dot_f32(a, b):
    return jnp.dot(a, b, preferred_element_type=F32, precision=lax.Precision.HIGHEST)


def _rwkv_prep_kernel(pc_ref, prev_ref, mu_ref, vec_ref, lora_ref, gup_ref, seg_ref,
                      r_ref, w_ref, k_ref, v_ref, kk_ref, b_ref, g_ref, bonus_ref):
    cw = r_ref.shape[1]
    pc = pc_ref[...]
    xm = pc + (prev_ref[...] - pc) * mu_ref[...]
    r, kc, vc = xm[:, 0:cw], xm[:, cw:2 * cw], xm[:, 2 * cw:3 * cw]
    wa = xm[:, 3 * cw:3 * cw + LANES]
    gd = xm[:, 3 * cw + LANES:]
    lane = lax.broadcasted_iota(I32, wa.shape, 1)
    wa = jnp.where(lane < LANES // 2, jnp.tanh(wa), wa)
    lo = _dot(wa.astype(MXU_DTYPE), lora_ref[...])
    w0, a0, k_k, k_a, r_k = (vec_ref[i:i + 1, :] for i in range(5))
    w_log = -jax.nn.softplus(-(w0 + lo[:, 0:cw])) - 0.5
    a = jax.nn.sigmoid(a0 + lo[:, cw:2 * cw])
    kk = kc * k_k
    norm = jnp.sqrt(_dot_f32(kk * kk, seg_ref[...]))
    kk = kk / jnp.maximum(norm, 1e-12)
    kc = kc * (1.0 + (a - 1.0) * k_a)
    r_ref[...] = r
    w_ref[...] = jnp.exp(-jnp.exp(w_log))
    k_ref[...] = kc
    v_ref[...] = vc
    kk_ref[...] = kk
    b_ref[...] = -(kk * a)
    g_ref[...] = _dot(jax.nn.sigmoid(gd).astype(MXU_DTYPE), gup_ref[...])
    bonus_ref[...] = _dot_f32(r * kc * r_k, seg_ref[...]) * vc


def _rwkv_prep(pc, prev, mu, vecs, lora, g_up, seg, tm):
    n, width = prev.shape
    cw = vecs.shape[1]
    row = lambda w: pl.BlockSpec((tm, w), lambda i: (i, 0))
    full = lambda a: pl.BlockSpec(a.shape, lambda i: (0, 0))
    out = jax.ShapeDtypeStruct((n, cw), F32)
    return pl.pallas_call(
        _rwkv_prep_kernel,
        out_shape=(out,) * 8,
        grid=(n // tm,),
        in_specs=[row(width), row(width), full(mu), full(vecs), full(lora), full(g_up), full(seg)],
        out_specs=(row(cw),) * 8,
        compiler_params=_cparams("parallel"),
        name="rwkv_prep",
    )(pc, prev, mu, vecs, lora, g_up, seg)


def _segment_sum(x, seg):
    hi = x.astype(MXU_DTYPE)
    lo = (x - hi.astype(F32)).astype(MXU_DTYPE)
    return _dot(hi, seg) + _dot(lo, seg)


def _rwkv_scan_kernel(*refs, nb):
    ins, (seg_ref, s0_ref, y_ref, s_ref) = refs[:6 * nb], refs[6 * nb:]

    @pl.when(pl.program_id(1) == 0)
    def _():
        s_ref[...] = s0_ref[...]

    _, rows, width = s_ref.shape
    pairs = rows // RWKV_HEAD
    lane = lax.broadcasted_iota(I32, (nb * rows, width), 1)
    sub = lax.broadcasted_iota(I32, (nb * rows, width), 0)
    diag = (sub & (RWKV_HEAD - 1)) == (lane & (RWKV_HEAD - 1))
    own = (lax.broadcasted_iota(I32, (2, width), 0) == 0) == (lax.broadcasted_iota(I32, (2, width), 1) < RWKV_HEAD)
    seg = seg_ref[...]

    def group(g, carry):
        t0 = pl.multiple_of(g * SUBLANES, SUBLANES)
        r8, w8, k8, v8, kk8, nb8 = ([ins[6 * bb + q][pl.ds(t0, SUBLANES), :] for bb in range(nb)] for q in range(6))
        s = s_ref[...].reshape(nb * rows, width)
        for i in range(SUBLANES):
            def per_row(x8):
                return jnp.concatenate([jnp.broadcast_to(x[i:i + 1, p * width:(p + 1) * width], (RWKV_HEAD, width))
                                        for x in x8 for p in range(pairs)], axis=0)
            sa = _segment_sum(s * per_row(kk8), seg)
            vcol = _segment_sum(jnp.where(diag, per_row(v8), 0.0), seg)
            s = s * per_row(w8) + sa * per_row(nb8) + vcol * per_row(k8)
            sb = s.astype(MXU_DTYPE)
            for bb in range(nb):
                for p in range(pairs):
                    r2 = jnp.where(own, r8[bb][i:i + 1, p * width:(p + 1) * width], 0.0)
                    r0 = (bb * pairs + p) * RWKV_HEAD
                    y = _dot_nt(r2.astype(MXU_DTYPE), sb[r0:r0 + RWKV_HEAD])
                    y_ref[bb, pl.ds(t0 + i, 1), 2 * p:2 * p + 2, :] = y[None]
        s_ref[...] = s.reshape(nb, rows, width)
        return carry

    lax.fori_loop(0, ins[0].shape[0] // SUBLANES, group, 0)


def _rwkv_scan(ins, row_start, batch, t, s0_packed, seg2, nb):
    cw = ins[0].shape[1]
    heads = cw // RWKV_HEAD
    tb = min(t, 64)
    nt = t // tb
    rb0 = row_start // tb
    rows = [pl.BlockSpec((tb, cw), lambda b, i, bb=bb: (rb0 + (b * nb + bb) * nt + i, 0)) for bb in range(nb)]
    st = pl.BlockSpec((nb,) + s0_packed.shape[1:], lambda b, i: (b, 0, 0))
    return pl.pallas_call(
        functools.partial(_rwkv_scan_kernel, nb=nb),
        out_shape=(jax.ShapeDtypeStruct((batch, t, heads, RWKV_HEAD), F32),
                   jax.ShapeDtypeStruct(s0_packed.shape, F32)),
        grid=(batch // nb, nt),
        in_specs=[spec for spec in rows for _ in range(6)] + [pl.BlockSpec(seg2.shape, lambda b, i: (0, 0)), st],
        out_specs=(pl.BlockSpec((nb, tb, heads, RWKV_HEAD), lambda b, i: (b, i, 0, 0)), st),
        compiler_params=_cparams("parallel", "arbitrary"),
        name="rwkv_scan",
    )(*(list(ins) * nb), seg2, s0_packed)


def _rwkv_post_kernel(y_ref, bonus_ref, g_ref, ln_ref, seg_ref, o_ref):
    y = y_ref[...]
    avg = seg_ref[...] * (1.0 / RWKV_HEAD)
    d = y - _dot_f32(y, avg)
    var = _dot_f32(d * d, avg)
    yn = d * lax.rsqrt(var + GN_EPS) * ln_ref[0:1, :] + ln_ref[1:2, :]
    o_ref[...] = (yn + bonus_ref[...]) * g_ref[...]


def _rwkv_post(y, bonus, g, ln, seg, row_start, tm):
    n, cw = y.shape
    rb0 = row_start // tm
    row = pl.BlockSpec((tm, cw), lambda i: (i, 0))
    off = pl.BlockSpec((tm, cw), lambda i: (rb0 + i, 0))
    full = lambda a: pl.BlockSpec(a.shape, lambda i: (0, 0))
    return pl.pallas_call(
        _rwkv_post_kernel,
        out_shape=jax.ShapeDtypeStruct((n, cw), F32),
        grid=(n // tm,),
        in_specs=[row, off, off, full(ln), full(seg)],
        out_specs=row,
        compiler_params=_cparams("parallel"),
        name="rwkv_post",
    )(y, bonus, g, ln, seg)


MLA_HEADS = 8
MLA_NOPE = 64
MLA_ROPE = 32
MLA_SCALE = (MLA_NOPE + MLA_ROPE) ** -0.5
MLA_PAGES_PER_STEP = 64


def _rope_tile(x, cos, sin):
    lane = lax.broadcasted_iota(I32, x.shape, 1)
    half = MLA_ROPE // 2
    rot = jnp.where(lane < half, pltpu.roll(x, LANES - half, 1), pltpu.roll(x, half, 1))
    return x * cos + rot * sin


def _mla_prep_kernel(qd_ref, ckv_ref, kr_ref, cos_ref, sin_ref, qn_ref, kvn_ref, wuq_ref, wuk_ref,
                     ql_ref, qr_ref, c_ref, krn_ref):
    cq = _rms(qd_ref[...], qn_ref[...])
    qh = _dot(cq.astype(MXU_DTYPE), wuq_ref[...])
    nope = MLA_HEADS * MLA_NOPE
    ql_ref[...] = _dot(qh[:, 0:nope].astype(MXU_DTYPE), wuk_ref[...]).astype(ql_ref.dtype)
    cos, sin = cos_ref[...], sin_ref[...]
    for h in range(MLA_HEADS):
        lanes = slice(nope + h * LANES, nope + (h + 1) * LANES)
        qr_ref[:, h * LANES:(h + 1) * LANES] = _rope_tile(qh[:, lanes], cos, sin).astype(qr_ref.dtype)
    c_ref[...] = _rms(ckv_ref[...], kvn_ref[...])
    krn_ref[...] = _rope_tile(kr_ref[...], cos, sin)


def _mla_prep(proj, col_blocks, row_start, nrows, cos, sin, q_norm, kv_norm, wuq, wuk, tm):
    qd0, ckv0, kr0 = col_blocks
    d_q, d_kv = q_norm.shape[1], kv_norm.shape[1]
    rb0 = row_start // tm
    nper = cos.shape[0] // tm
    full = lambda a: pl.BlockSpec(a.shape, lambda i: (0, 0))
    rows = lambda w: pl.BlockSpec((tm, w), lambda i: (i, 0))
    tab = pl.BlockSpec((tm, LANES), lambda i: (i % nper, 0))
    return pl.pallas_call(
        _mla_prep_kernel,
        out_shape=(jax.ShapeDtypeStruct((nrows, MLA_HEADS * d_kv), MXU_DTYPE),
                   jax.ShapeDtypeStruct((nrows, MLA_HEADS * LANES), MXU_DTYPE),
                   jax.ShapeDtypeStruct((nrows, d_kv), F32),
                   jax.ShapeDtypeStruct((nrows, LANES), F32)),
        grid=(nrows // tm,),
        in_specs=[pl.BlockSpec((tm, d_q), lambda i: (rb0 + i, qd0 * LANES // d_q)),
                  pl.BlockSpec((tm, d_kv), lambda i: (rb0 + i, ckv0 * LANES // d_kv)),
                  pl.BlockSpec((tm, LANES), lambda i: (rb0 + i, kr0)),
                  tab, tab, full(q_norm), full(kv_norm), full(wuq), full(wuk)],
        out_specs=(rows(MLA_HEADS * d_kv), rows(MLA_HEADS * LANES), rows(d_kv), rows(LANES)),
        compiler_params=_cparams("parallel"),
        name="mla_prep",
    )(proj, proj, proj, cos, sin, q_norm, kv_norm, wuq, wuk)


def _mla_prompt_kernel(ql_ref, qr_ref, c_ref, kr_ref, wuv_ref, o_ref, m_ref, l_ref, acc_ref, *, kb_size):
    tq = ql_ref.shape[0]
    d_kv = c_ref.shape[1]
    q0 = pl.program_id(1) * tq
    n_kb = (q0 + tq - 1) // kb_size + 1
    qpos = q0 + lax.broadcasted_iota(I32, (tq, kb_size), 0)
    lane = lax.broadcasted_iota(I32, (tq, kb_size), 1)
    m_ref[...] = jnp.full_like(m_ref, ONLINE_SOFTMAX_FLOOR)
    l_ref[...] = jnp.zeros_like(l_ref)
    acc_ref[...] = jnp.zeros_like(acc_ref)

    def block(kb, carry):
        k0 = pl.multiple_of(kb * kb_size, kb_size)
        c, kr = c_ref[pl.ds(k0, kb_size), :], kr_ref[pl.ds(k0, kb_size), :]
        scores = [(_dot_nt(ql_ref[:, h * d_kv:(h + 1) * d_kv], c)
                   + _dot_nt(qr_ref[:, h * LANES:(h + 1) * LANES], kr)) * MLA_SCALE for h in range(MLA_HEADS)]
        _online_softmax_heads(m_ref, l_ref, acc_ref, scores, k0 + lane <= qpos, lambda g, p: _dot(p, c))
        return carry

    lax.fori_loop(0, n_kb, block, 0)
    o_lat = (acc_ref[...] / l_ref[...]).astype(MXU_DTYPE)
    o_lat = jnp.concatenate([o_lat[h * tq:(h + 1) * tq] for h in range(MLA_HEADS)], axis=1)
    o_ref[...] = _dot(o_lat, wuv_ref[...])


def _mla_prompt(q_lat, q_rope, c, kr, wuv, batch, t, tq=256):
    d_kv = c.shape[1]
    nq = t // tq
    rows = MLA_HEADS * tq
    return pl.pallas_call(
        functools.partial(_mla_prompt_kernel, kb_size=math.gcd(t, KEY_BLOCK)),
        scratch_shapes=[pltpu.VMEM((rows, 1), F32), pltpu.VMEM((rows, 1), F32), pltpu.VMEM((rows, d_kv), F32)],
        out_shape=jax.ShapeDtypeStruct((batch * t, wuv.shape[1]), F32),
        grid=(batch, nq),
        in_specs=[pl.BlockSpec((tq, MLA_HEADS * d_kv), lambda b, i: (b * nq + i, 0)),
                  pl.BlockSpec((tq, MLA_HEADS * LANES), lambda b, i: (b * nq + i, 0)),
                  pl.BlockSpec((t, d_kv), lambda b, i: (b, 0)),
                  pl.BlockSpec((t, LANES), lambda b, i: (b, 0)),
                  pl.BlockSpec(wuv.shape, lambda b, i: (0, 0))],
        out_specs=pl.BlockSpec((tq, wuv.shape[1]), lambda b, i: (b * nq + i, 0)),
        compiler_params=_cparams("parallel", "arbitrary"),
        name="mla_prompt",
    )(q_lat, q_rope, c, kr, wuv)


def _mla_sample_kernel(pt_ref, ql_ref, qr_ref, cn_ref, krn_ref, *rest, pages, t_new):
    c_pages, kr_pages = rest[:pages], rest[pages:2 * pages]
    o_ref, m_ref, l_ref, acc_ref = rest[2 * pages:]
    step = pl.program_id(1)

    @pl.when(step == 0)
    def _():
        m_ref[...] = jnp.full_like(m_ref, ONLINE_SOFTMAX_FLOOR)
        l_ref[...] = jnp.zeros_like(l_ref)
        acc_ref[...] = jnp.zeros_like(acc_ref)

    ql, qr = ql_ref[...], qr_ref[...]
    cc = jnp.concatenate([r[...] for r in c_pages], axis=0).astype(MXU_DTYPE)
    kc = jnp.concatenate([r[...] for r in kr_pages], axis=1).astype(MXU_DTYPE)
    s = (_dot_nt(ql, cc) + _dot(qr[:, 0:MLA_ROPE], kc)) * MLA_SCALE
    _online_softmax_step(m_ref, l_ref, acc_ref, 0, s, jnp.full(s.shape, True), cc)

    @pl.when(step == pl.num_programs(1) - 1)
    def _():
        cn = cn_ref[...]
        s_new = (_dot_nt(ql, cn) + _dot_nt(qr, krn_ref[...])) * MLA_SCALE
        qpos = lax.broadcasted_iota(I32, s_new.shape, 0) % t_new
        kpos = lax.broadcasted_iota(I32, s_new.shape, 1)
        _online_softmax_step(m_ref, l_ref, acc_ref, 0, s_new, (kpos <= qpos) & (kpos < t_new), cn)
        o_ref[...] = acc_ref[0] / l_ref[0]


def _mla_sample(page_table, q_lat, q_rope, c_new_pad, kr_new_pad, pool_c, pool_kr, layer, t_new):
    b, n_pages = page_table.shape
    pages = math.gcd(n_pages, MLA_PAGES_PER_STEP)
    _, rows, d_kv = q_lat.shape
    one = lambda a: pl.BlockSpec((None,) + a.shape[1:], lambda n, c, pt: (n, 0, 0))
    grid_spec = pltpu.PrefetchScalarGridSpec(
        num_scalar_prefetch=1,
        grid=(b, n_pages // pages),
        in_specs=[one(q_lat), one(q_rope), one(c_new_pad), one(kr_new_pad)]
        + _page_specs((None, None, PAGE_SIZE, d_kv), layer, pages)
        + _page_specs((None, None, MLA_ROPE, PAGE_SIZE), layer, pages),
        out_specs=pl.BlockSpec((None, rows, d_kv), lambda n, c, pt: (n, 0, 0)),
        scratch_shapes=[pltpu.VMEM((1, rows, 1), F32), pltpu.VMEM((1, rows, 1), F32),
                        pltpu.VMEM((1, rows, d_kv), F32)],
    )
    return pl.pallas_call(
        functools.partial(_mla_sample_kernel, pages=pages, t_new=t_new),
        out_shape=jax.ShapeDtypeStruct((b, rows, d_kv), F32),
        grid_spec=grid_spec,
        compiler_params=_cparams("parallel", "arbitrary"),
        name="mla_sample",
    )(page_table, q_lat, q_rope, c_new_pad, kr_new_pad, *([pool_c] * pages), *([pool_kr] * pages))


ROW_BLOCK = 256
A_HEADS = 4
B_HEADS, B_KV_HEADS, B_DH = 8, 4, 64
IDX_HEADS, IDX_DIM = 8, 64
C_WIDTH = 512
RWKV_IN = 3 * C_WIDTH + 64 + 64 + 128
D_Q_RANK, D_KV_RANK = 384, 256
ODD_COLS = RWKV_IN + D_KV_RANK + 2 * LANES + D_Q_RANK
ODD_SLABS = ((RWKV_IN + D_KV_RANK + 2 * LANES) // LANES, RWKV_IN // LANES, (RWKV_IN + D_KV_RANK) // LANES)


def _pad_rows(a, rows=LANES):
    return jnp.pad(a, ((0, 0),) * (a.ndim - 2) + ((0, rows - a.shape[-2]), (0, 0)))


def _head_major(a, b, t, h):
    return jnp.transpose(a.reshape(b, t, h, -1), (0, 2, 1, 3))


def _token_major(a):
    b, h, t, w = a.shape
    return jnp.transpose(a, (0, 2, 1, 3)).reshape(b * t, h * w)


def _rope_tables(pos):
    half = MLA_ROPE // 2
    inv = ROPE_THETA ** (-jnp.arange(0, MLA_ROPE, 2, dtype=F32) / MLA_ROPE)
    ang = pos.astype(F32)[:, None] * inv[None, :]
    zeros = jnp.zeros((pos.shape[0], LANES - 2 * half), F32)
    cos, sin = jnp.cos(ang), jnp.sin(ang)
    return jnp.concatenate([cos, cos, zeros], axis=1), jnp.concatenate([-sin, sin, zeros], axis=1)


def kernel(x_prompt, x_sample, cache_dsa_k, cache_dsa_v, cache_dsa_idx, cache_mla_ckv, cache_mla_krope, state_hgrn, state_rwkv, state_shift, page_table, norm_mix, norm_ffn, norm_final, w_in_even, w_out_even, hgrn_lb, hgrn_norm, w_in_odd, w_out_odd, rwkv_mu, rwkv_w0, rwkv_w_up, rwkv_a0, rwkv_a_up, rwkv_g_up, rwkv_k_k, rwkv_k_a, rwkv_r_k, rwkv_ln_w, rwkv_ln_b, mla_q_norm, mla_w_uq, mla_kv_norm, mla_w_uk, mla_w_uv, peer_wq, peer_subkeys, peer_u, peer_v):
    bp, tp, d_model = x_prompt.shape
    bs, ts, _ = x_sample.shape
    n_p, n_s = bp * tp, bs * ts
    n_all = n_p + n_s
    n_pad = _round_up(n_all, PEER_TOKEN_BLOCK)
    tm_s = min(ROW_BLOCK, n_s)
    n_past = page_table.shape[1] * PAGE_SIZE
    depth = norm_mix.shape[0]
    md = MXU_DTYPE

    def all_rows(p, s):
        return jnp.concatenate([p, s, jnp.zeros((n_pad - n_all, p.shape[1]), p.dtype)], axis=0)

    x = all_rows(x_prompt.reshape(n_p, d_model), x_sample.reshape(n_s, d_model))

    lb_cum = jnp.cumsum(jax.nn.softmax(hgrn_lb.astype(F32), axis=0), axis=0)
    lower_bounds = lb_cum - lb_cum[:1]
    seg = jnp.kron(jnp.eye(C_WIDTH // RWKV_HEAD, dtype=F32), jnp.ones((RWKV_HEAD, RWKV_HEAD), F32))
    pool_dsa_k = jnp.transpose(cache_dsa_k, (0, 1, 3, 4, 2))
    pool_dsa_v = jnp.transpose(cache_dsa_v, (0, 1, 3, 4, 2))
    pool_dsa_idx = jnp.transpose(cache_dsa_idx, (0, 1, 3, 2))
    pool_mla_kr = jnp.transpose(cache_mla_krope, (0, 1, 3, 2))
    eye_h = jnp.eye(MLA_HEADS, dtype=F32)
    cos_p, sin_p = _rope_tables(jnp.arange(tp))
    cos_s, sin_s = (jnp.tile(a, (tm_s // ts, 1)) for a in _rope_tables(n_past + jnp.arange(ts)))

    outs = {k: [] for k in ("pk", "pv", "pi", "pc", "pr", "ph", "ps", "psh", "sk", "sv", "si", "sc", "sr", "sh", "ss", "ssh")}

    for l in range(depth):
        j = l // 2
        if l % 2 == 0:
            w_in = jnp.pad(w_in_even[j], ((0, 0), (0, _round_up(w_in_even.shape[2], LANES) - w_in_even.shape[2]))).astype(md)
            proj = _matmul(x, w_in, g=norm_mix[l], tm=ROW_BLOCK)
            aw = A_HEADS * LANES
            c_q, c_k, c_v = 4 * aw, 4 * aw + B_HEADS * B_DH, 4 * aw + (B_HEADS + B_KV_HEADS) * B_DH
            c_qi = c_v + B_KV_HEADS * B_DH
            c_ki = c_qi + IDX_HEADS * IDX_DIM
            c_wi = c_ki + IDX_DIM
            mixes = []
            for rows, b, t, s0_t, kk, kv, ki_key, kh in ((slice(0, n_p), bp, tp, None, "pk", "pv", "pi", "ph"),
                                                    (slice(n_p, n_all), bs, ts, state_hgrn[j], "sk", "sv", "si", "sh")):
                sample = s0_t is not None
                s0_t = jnp.swapaxes(s0_t, -1, -2).astype(F32) if sample else jnp.zeros((b, A_HEADS, LANES, LANES), F32)
                oa, s_a = _hgrn(proj, rows.start, b, t, A_HEADS, lower_bounds[j], hgrn_norm[j], s0_t)
                pr = proj[rows]
                qb, kb, vb = pr[:, c_q:c_k], pr[:, c_k:c_v], pr[:, c_v:c_qi]
                qi, ki, wi = pr[:, c_qi:c_ki], pr[:, c_ki:c_wi], pr[:, c_wi:c_wi + IDX_HEADS]
                q4 = _head_major(qb, b, t, B_HEADS).astype(md)
                k4 = _head_major(kb, b, t, B_KV_HEADS).astype(md)
                v4 = _head_major(vb, b, t, B_KV_HEADS).astype(md)
                ki3 = ki.reshape(b, t, IDX_DIM).astype(md)
                tq = t if sample else DSA_QUERY_BLOCK
                qi2 = jnp.transpose(qi.reshape(b, t // tq, tq, IDX_HEADS, IDX_DIM), (0, 1, 3, 2, 4))
                qi2 = qi2.reshape(b, t // tq, IDX_HEADS * tq, IDX_DIM).astype(md)
                wcol = jnp.transpose(wi.reshape(b, t // tq, tq, IDX_HEADS), (0, 1, 3, 2)).reshape(b, t // tq, IDX_HEADS * tq, 1)
                if sample:
                    topk = min(DSA_TOPK_MAX, (n_past + t) // 4)
                    mask = _dsa_select(page_table, qi2[:, 0], wcol[:, 0], _pad_rows(ki3), pool_dsa_idx, j, topk, t, IDX_HEADS)
                    group = B_HEADS // B_KV_HEADS
                    o4 = _dsa_attend(page_table, q4.reshape(b, B_KV_HEADS, group * t, B_DH), mask,
                                     _pad_rows(k4), _pad_rows(v4), pool_dsa_k, pool_dsa_v, j)
                    o4 = o4.reshape(b, B_HEADS, t, B_DH)
                else:
                    o4 = _dsa_prompt(qi2, wcol, ki3, q4, k4, v4, IDX_HEADS, tq)
                mixes.append(jnp.concatenate([oa, _token_major(o4)], axis=1))
                outs[kk].append(kb.reshape(b, t, B_KV_HEADS, B_DH))
                outs[kv].append(vb.reshape(b, t, B_KV_HEADS, B_DH))
                outs[ki_key].append(ki.reshape(b, t, IDX_DIM))
                outs[kh].append(jnp.swapaxes(s_a, -1, -2))
            x = _matmul(all_rows(*mixes), w_out_even[j].astype(md), res=x, tm=ROW_BLOCK)
        else:
            w = w_in_odd[j]
            c_qd, c_ckv = RWKV_IN, RWKV_IN + D_Q_RANK
            c_kr = c_ckv + D_KV_RANK
            w_in = jnp.concatenate([w[:, :RWKV_IN], w[:, c_ckv:c_kr], w[:, c_kr:c_kr + MLA_ROPE],
                                    jnp.zeros((d_model, 2 * LANES - MLA_ROPE), w.dtype), w[:, c_qd:c_ckv]], axis=1).astype(md)
            proj = _matmul(x, w_in, g=norm_mix[l], tm=ROW_BLOCK)
            pc_p = proj[:n_p, :RWKV_IN].reshape(bp, tp, RWKV_IN)
            pc_s = proj[n_p:n_all, :RWKV_IN].reshape(bs, ts, RWKV_IN)
            prev_p = jnp.concatenate([jnp.zeros((bp, 1, RWKV_IN), F32), pc_p[:, :-1]], axis=1)
            prev_s = jnp.concatenate([state_shift[j].astype(F32)[:, None], pc_s[:, :-1]], axis=1)
            prev = all_rows(prev_p.reshape(n_p, RWKV_IN), prev_s.reshape(n_s, RWKV_IN))
            zeros_l = jnp.zeros((rwkv_w_up.shape[1], C_WIDTH), F32)
            lora = jnp.concatenate([jnp.concatenate([rwkv_w_up[j], zeros_l], axis=1),
                                    jnp.concatenate([zeros_l, rwkv_a_up[j]], axis=1)], axis=0).astype(md)
            vecs = jnp.stack([rwkv_w0[j], rwkv_a0[j], rwkv_k_k[j], rwkv_k_a[j], rwkv_r_k[j].reshape(-1)]).astype(F32)
            prep = _rwkv_prep(proj, prev, rwkv_mu[j].reshape(1, -1).astype(F32), vecs, lora, rwkv_g_up[j].astype(md), seg, ROW_BLOCK)
            scan_in, g_all, bonus_all = prep[:6], prep[6], prep[7]
            ln = jnp.stack([rwkv_ln_w[j], rwkv_ln_b[j]]).astype(F32)

            wuq = mla_w_uq[j].reshape(D_Q_RANK, MLA_HEADS, MLA_NOPE + MLA_ROPE)
            wuq_rope = jnp.pad(wuq[:, :, MLA_NOPE:], ((0, 0), (0, 0), (0, LANES - MLA_ROPE))).reshape(D_Q_RANK, -1)
            wuq_p = jnp.concatenate([wuq[:, :, :MLA_NOPE].reshape(D_Q_RANK, -1), wuq_rope], axis=1).astype(md)
            wuk = jnp.einsum("chn,hg->hngc", mla_w_uk[j], eye_h).reshape(MLA_HEADS * MLA_NOPE, -1).astype(md)
            wuv = jnp.einsum("chv,hg->hcgv", mla_w_uv[j], eye_h).reshape(MLA_HEADS * D_KV_RANK, -1).astype(md)
            q_norm = mla_q_norm[j].reshape(1, -1).astype(F32)
            kv_norm = mla_kv_norm[j].reshape(1, -1).astype(F32)

            mixes = []
            for start, b, t, tm, cos, sin, sample in ((0, bp, tp, ROW_BLOCK, cos_p, sin_p, False),
                                                      (n_p, bs, ts, tm_s, cos_s, sin_s, True)):
                n = b * t
                if sample:
                    heads = C_WIDTH // RWKV_HEAD
                    s0 = state_rwkv[j].astype(F32).reshape(b, heads // 2, 2, RWKV_HEAD, RWKV_HEAD)
                    s0 = jnp.transpose(s0, (0, 1, 3, 2, 4)).reshape(b, heads // 2 * RWKV_HEAD, 2 * RWKV_HEAD)
                else:
                    s0 = jnp.zeros((b, C_WIDTH // LANES * RWKV_HEAD, LANES), F32)
                nb = math.gcd(b, RWKV_BATCH_BLOCK)
                y3, s_c = _rwkv_scan(scan_in, start, b, t, s0, seg[:LANES, :LANES].astype(md), nb)
                yc = _rwkv_post(y3.reshape(n, C_WIDTH), bonus_all, g_all, ln, seg, start, tm)
                s_c = jnp.transpose(s_c.reshape(b, -1, RWKV_HEAD, 2, RWKV_HEAD), (0, 1, 3, 2, 4))
                s_c = s_c.reshape(b, -1, RWKV_HEAD, RWKV_HEAD)

                q_lat, q_rope, c_new, kr_new = _mla_prep(proj, ODD_SLABS, start, n, cos, sin, q_norm, kv_norm, wuq_p, wuk, tm)
                if sample:
                    hq = lambda a: _head_major(a, b, t, MLA_HEADS).reshape(b, MLA_HEADS * t, -1)
                    o_lat = _mla_sample(page_table, hq(q_lat), hq(q_rope),
                                        _pad_rows(c_new.reshape(b, t, -1)).astype(md),
                                        _pad_rows(kr_new.reshape(b, t, -1)).astype(md),
                                        cache_mla_ckv, pool_mla_kr, j, t)
                    o_lat = _token_major(o_lat.reshape(b, MLA_HEADS, t, -1))
                    od = _matmul(o_lat, wuv, tm=tm)
                else:
                    od = _mla_prompt(q_lat, q_rope, c_new.astype(md), kr_new.astype(md), wuv, b, t)
                mixes.append(jnp.concatenate([yc, od], axis=1))
                pre = "s" if sample else "p"
                outs[pre + "c"].append(c_new.reshape(b, t, D_KV_RANK))
                outs[pre + "r"].append(kr_new[:, :MLA_ROPE].reshape(b, t, MLA_ROPE))
                outs[pre + "s"].append(s_c)
                outs[pre + "sh"].append((pc_s if sample else pc_p)[:, -1])
            x = _matmul(all_rows(*mixes), w_out_odd[j].astype(md), res=x, tm=ROW_BLOCK)
        x = _peer_ffn(x, norm_ffn[l], _peer_prepare(peer_wq[l], peer_subkeys[l], peer_u[l], peer_v[l]))

    y = _rmsnorm(x, norm_final)
    y_prompt = y[:n_p].reshape(bp, tp, d_model)
    y_sample = y[n_p:n_all].reshape(bs, ts, d_model)
    st = lambda k: jnp.stack(outs[k])
    return (y_prompt, y_sample,
            st("pk"), st("pv"), st("pi"), st("pc"), st("pr"), st("ph"), st("ps"), st("psh"),
            st("sk"), st("sv"), st("si"), st("sc"), st("sr"), st("sh"), st("ss"), st("ssh"))
```
